```python
import math
import jax, jax.numpy as jnp
from jax import lax
import numpy as np

D_MODEL = 1024
BATCH = 8
SEQ = 4096
DEPTH = 1

HEAD_DIM = 64
D_MIX = D_MODEL
MOBA_HEADS = (D_MIX // 2) // HEAD_DIM
NSA_HEADS = (D_MIX // 2) // HEAD_DIM
NSA_KV_GROUPS = 2
MOBA_BLOCK = 256
MOBA_TOPK = 3
MOBA_QCHUNK = 32
NSA_CMP_LEN = 32
NSA_CMP_STRIDE = 16
NSA_CMP_HIDDEN = 256
NSA_SLC_BLOCK = 64
NSA_SLC_TOPN = 16
NSA_WINDOW = 512
NSA_QCHUNK = 64
NSA_N_GATES = 3
REL_BUCKETS = 32
REL_MAX_EXACT = REL_BUCKETS // 2
REL_MAX_DIST = 128
N_BIAS_HEADS = MOBA_HEADS + NSA_HEADS
D_FF = 2816
RMS_EPS = 1e-6
PROJ_SIZES = (MOBA_HEADS * HEAD_DIM,) * 3 + (NSA_HEADS * HEAD_DIM,) + (NSA_KV_GROUPS * HEAD_DIM,) * 6 + (NSA_HEADS * NSA_N_GATES,)
D_IN_PROJ = sum(PROJ_SIZES)
PROJ_SPLITS = tuple(int(s) for s in np.cumsum(PROJ_SIZES)[:-1])

kernel_name = 'hybrid_moba_nsa_macaron'


def rmsnorm(x, g):
    xf = x.astype(jnp.float32)
    y = xf * lax.rsqrt(jnp.mean(xf * xf, axis=-1, keepdims=True) + RMS_EPS)
    return (y * g.astype(jnp.float32)).astype(x.dtype)


def swiglu(h, w_gate, w_up, w_down):
    return (jax.nn.silu(h @ w_gate) * (h @ w_up)) @ w_down


def rel_bucket(dist):
    n = jnp.maximum(dist, 0)
    is_small = n < REL_MAX_EXACT
    nf = jnp.maximum(n, 1).astype(jnp.float32)
    large = REL_MAX_EXACT + (jnp.log(nf / REL_MAX_EXACT) / math.log(REL_MAX_DIST / REL_MAX_EXACT) * (REL_BUCKETS - REL_MAX_EXACT)).astype(jnp.int32)
    return jnp.where(is_small, n, jnp.minimum(large, REL_BUCKETS - 1))


def masked_softmax(logits, mask):
    lf = jnp.where(mask, logits.astype(jnp.float32), -jnp.inf)
    m = jnp.max(lf, axis=-1, keepdims=True)
    m = jnp.where(jnp.isfinite(m), m, 0.0)
    p = jnp.exp(lf - m)
    den = jnp.sum(p, axis=-1, keepdims=True)
    return p / jnp.where(den > 0, den, 1.0)


def slc_overlap(n_cmp, n_slc):
    r = NSA_SLC_BLOCK // NSA_CMP_STRIDE
    mc = NSA_CMP_LEN // NSA_CMP_STRIDE
    j = np.arange(n_slc)
    offs = (np.arange(r)[:, None] + np.arange(mc)[None, :]).reshape(-1)
    c = r * j[:, None] + offs[None, :]
    w = np.zeros((n_slc * r + mc, n_slc), np.float32)
    np.add.at(w, (c, np.broadcast_to(j[:, None], c.shape)), 1.0)
    return w[:n_cmp]


def moba_attention(q, k, v, tab):
    B, H, S, dh = q.shape
    nblk = -(-S // MOBA_BLOCK)
    s_pad = nblk * MOBA_BLOCK
    pad = ((0, 0), (0, 0), (0, s_pad - S), (0, 0))
    kp = jnp.pad(k, pad)
    vp = jnp.pad(v, pad)
    k_blocks = kp.reshape(B, H, nblk, MOBA_BLOCK, dh)
    v_blocks = vp.reshape(B, H, nblk, MOBA_BLOCK, dh)
    k_mean = jnp.mean(k_blocks.astype(jnp.float32), axis=3).astype(k.dtype)
    topk = min(MOBA_TOPK, nblk)
    n_sel = topk * MOBA_BLOCK
    C = MOBA_QCHUNK
    n_chunks = S // C
    q_chunks = q.reshape(B, H, n_chunks, C, dh).transpose(2, 0, 1, 3, 4)
    bi = jnp.arange(B)[:, None, None, None]
    hi = jnp.arange(H)[None, :, None, None]
    blk_ids = jnp.arange(nblk)
    offs = jnp.arange(MOBA_BLOCK)
    scale = dh ** -0.5

    def chunk_fn(args):
        qc, c = args
        t = c * C + jnp.arange(C)
        cur = (c * C) // MOBA_BLOCK
        gate = jnp.einsum('bhcd,bhnd->bhcn', qc, k_mean).astype(jnp.float32)
        gate = jnp.where(blk_ids < cur, gate, -jnp.inf)
        _, sel = lax.top_k(gate, topk)
        k_sel = k_blocks[bi, hi, sel].reshape(B, H, C, n_sel, dh)
        v_sel = v_blocks[bi, hi, sel].reshape(B, H, C, n_sel, dh)
        pos_sel = (sel[..., None] * MOBA_BLOCK + offs).reshape(B, H, C, n_sel)
        mask_sel = jnp.repeat(sel < cur, MOBA_BLOCK, axis=-1)
        bias_sel = tab[hi, rel_bucket(t[:, None] - pos_sel)]
        k_own = lax.dynamic_slice_in_dim(kp, cur * MOBA_BLOCK, MOBA_BLOCK, axis=2)
        v_own = lax.dynamic_slice_in_dim(vp, cur * MOBA_BLOCK, MOBA_BLOCK, axis=2)
        d_own = t[:, None] - (cur * MOBA_BLOCK + offs)[None, :]
        mask_own = jnp.broadcast_to(d_own >= 0, (B, H, C, MOBA_BLOCK))
        bias_own = tab[:, rel_bucket(d_own)]
        lg_sel = jnp.einsum('bhcd,bhckd->bhck', qc, k_sel).astype(jnp.float32) * scale + bias_sel
        lg_own = jnp.einsum('bhcd,bhkd->bhck', qc, k_own).astype(jnp.float32) * scale + bias_own
        p = masked_softmax(jnp.concatenate([lg_sel, lg_own], axis=-1), jnp.concatenate([mask_sel, mask_own], axis=-1)).astype(v.dtype)
        return jnp.einsum('bhck,bhckd->bhcd', p[..., :n_sel], v_sel) + jnp.einsum('bhck,bhkd->bhcd', p[..., n_sel:], v_own)

    out = lax.map(chunk_fn, (q_chunks, jnp.arange(n_chunks)))
    return out.transpose(1, 2, 0, 3, 4).reshape(B, H, S, dh)


def nsa_attention(q, kc_raw, vc_raw, ks, vs, kw, vw, gates, tab, pos_k, w1_k, w2_k, pos_v, w1_v, w2_v):
    B, H, S, dh = q.shape
    G = kc_raw.shape[1]
    R = H // G
    scale = dh ** -0.5
    n_cmp = (S - NSA_CMP_LEN) // NSA_CMP_STRIDE + 1
    idx = (np.arange(n_cmp)[:, None] * NSA_CMP_STRIDE + np.arange(NSA_CMP_LEN)[None, :]).astype(np.int32)

    def compress(raw, pos, w1, w2):
        blocks = raw[:, :, idx] + pos
        return jax.nn.silu(blocks.reshape(B, G, n_cmp, NSA_CMP_LEN * dh) @ w1) @ w2

    k_cmp = compress(kc_raw, pos_k, w1_k, w2_k)
    v_cmp = compress(vc_raw, pos_v, w1_v, w2_v)
    cmp_end = jnp.asarray(np.arange(n_cmp, dtype=np.int32) * NSA_CMP_STRIDE + NSA_CMP_LEN - 1)
    n_slc = S // NSA_SLC_BLOCK
    n_top = min(NSA_SLC_TOPN, n_slc)
    n_sel = n_top * NSA_SLC_BLOCK
    overlap = jnp.asarray(slc_overlap(n_cmp, n_slc))
    ks_blocks = ks.reshape(B, G, n_slc, NSA_SLC_BLOCK, dh)
    vs_blocks = vs.reshape(B, G, n_slc, NSA_SLC_BLOCK, dh)
    wpad = ((0, 0), (0, 0), (NSA_WINDOW, 0), (0, 0))
    kw_pad = jnp.pad(kw, wpad)
    vw_pad = jnp.pad(vw, wpad)
    tab_g = tab.reshape(G, R, REL_BUCKETS)
    C = NSA_QCHUNK
    n_chunks = S // C
    q_chunks = q.reshape(B, G, R, n_chunks, C, dh).transpose(3, 0, 1, 2, 4, 5)
    g_chunks = gates.reshape(B, G, R, n_chunks, C, NSA_N_GATES).transpose(3, 0, 1, 2, 4, 5)
    bi = jnp.arange(B)[:, None, None, None]
    gi = jnp.arange(G)[None, :, None, None]
    gi5 = jnp.arange(G)[:, None, None, None]
    ri5 = jnp.arange(R)[:, None, None]
    slc_ids = jnp.arange(n_slc)
    offs = jnp.arange(NSA_SLC_BLOCK)
    win_offs = jnp.arange(NSA_WINDOW + C)

    def chunk_fn(args):
        qc, gc, c = args
        t = c * C + jnp.arange(C)
        d_cmp = t[:, None] - cmp_end[None, :]
        lg = jnp.einsum('bgrcd,bgnd->bgrcn', qc, k_cmp).astype(jnp.float32) * scale + tab_g[:, :, rel_bucket(d_cmp)]
        p_cmp = masked_softmax(lg, d_cmp >= 0)
        o_cmp = jnp.einsum('bgrcn,bgnd->bgrcd', p_cmp.astype(v_cmp.dtype), v_cmp)
        imp = jnp.einsum('bgrcn,nj->bgcj', p_cmp, overlap)
        cur = t // NSA_SLC_BLOCK
        forced = (slc_ids[None, :] == 0) | (slc_ids[None, :] == cur[:, None]) | (slc_ids[None, :] == cur[:, None] - 1)
        allowed = slc_ids[None, :] <= cur[:, None]
        score = jnp.where(forced, jnp.inf, jnp.where(allowed, imp, -jnp.inf))
        _, sel = lax.top_k(score, n_top)
        k_sel = ks_blocks[bi, gi, sel].reshape(B, G, C, n_sel, dh)
        v_sel = vs_blocks[bi, gi, sel].reshape(B, G, C, n_sel, dh)
        pos_sel = (sel[..., None] * NSA_SLC_BLOCK + offs).reshape(B, G, C, n_sel)
        d_sel = t[:, None] - pos_sel
        lg = jnp.einsum('bgrcd,bgckd->bgrck', qc, k_sel).astype(jnp.float32) * scale + tab_g[gi5, ri5, rel_bucket(d_sel)[:, :, None]]
        p_slc = masked_softmax(lg, (d_sel >= 0)[:, :, None]).astype(v_sel.dtype)
        o_slc = jnp.einsum('bgrck,bgckd->bgrcd', p_slc, v_sel)
        k_win = lax.dynamic_slice_in_dim(kw_pad, c * C, NSA_WINDOW + C, axis=2)
        v_win = lax.dynamic_slice_in_dim(vw_pad, c * C, NSA_WINDOW + C, axis=2)
        pos_win = c * C - NSA_WINDOW + win_offs
        d_win = t[:, None] - pos_win[None, :]
        mask_win = (pos_win[None, :] >= 0) & (d_win >= 0) & (d_win < NSA_WINDOW)
        lg = jnp.einsum('bgrcd,bgkd->bgrck', qc, k_win).astype(jnp.float32) * scale + tab_g[:, :, rel_bucket(d_win)]
        p_win = masked_softmax(lg, mask_win).astype(v_win.dtype)
        o_win = jnp.einsum('bgrck,bgkd->bgrcd', p_win, v_win)
        return gc[..., 0:1] * o_cmp + gc[..., 1:2] * o_slc + gc[..., 2:3] * o_win

    out = lax.map(chunk_fn, (q_chunks, g_chunks, jnp.arange(n_chunks)))
    return out.transpose(1, 2, 3, 0, 4, 5).reshape(B, H, S, dh)


def hybrid_mixer(h, w_in, pos_k, w1_k, w2_k, pos_v, w1_v, w2_v, w_out, rel_bias):
    B, S, _ = h.shape
    proj = h @ w_in
    qa, ka, va, qb, kcb, vcb, ksb, vsb, kwb, vwb, gb = jnp.split(proj, PROJ_SPLITS, axis=-1)

    def heads(t, n):
        return t.reshape(B, S, n, HEAD_DIM).transpose(0, 2, 1, 3)

    out_a = moba_attention(heads(qa, MOBA_HEADS), heads(ka, MOBA_HEADS), heads(va, MOBA_HEADS), rel_bias[:, :MOBA_HEADS].T)
    gates = jax.nn.sigmoid(gb.astype(jnp.float32)).astype(h.dtype).reshape(B, S, NSA_HEADS, NSA_N_GATES).transpose(0, 2, 1, 3)
    out_b = nsa_attention(heads(qb, NSA_HEADS), heads(kcb, NSA_KV_GROUPS), heads(vcb, NSA_KV_GROUPS), heads(ksb, NSA_KV_GROUPS), heads(vsb, NSA_KV_GROUPS), heads(kwb, NSA_KV_GROUPS), heads(vwb, NSA_KV_GROUPS), gates, rel_bias[:, MOBA_HEADS:].T, pos_k, w1_k, w2_k, pos_v, w1_v, w2_v)
    o = jnp.concatenate([out_a.transpose(0, 2, 1, 3).reshape(B, S, MOBA_HEADS * HEAD_DIM), out_b.transpose(0, 2, 1, 3).reshape(B, S, NSA_HEADS * HEAD_DIM)], axis=-1)
    return o @ w_out


def setup_inputs(seed: int = 0) -> dict:
    key = jax.random.key(seed)
    ks = jax.random.split(key, 20)

    def nrm(k, shape, s):
        return jax.random.normal(k, shape, jnp.float32) * s

    def gain(k, shape):
        return 1.0 + 0.05 * jax.random.normal(k, shape, jnp.float32)

    L = DEPTH
    flat_cmp = NSA_CMP_LEN * HEAD_DIM
    return {
        'x': nrm(ks[0], (BATCH, SEQ, D_MODEL), 1.0),
        'norm_ffn1': gain(ks[1], (L, D_MODEL)),
        'w_ffn1_gate': nrm(ks[2], (L, D_MODEL, D_FF), D_MODEL ** -0.5),
        'w_ffn1_up': nrm(ks[3], (L, D_MODEL, D_FF), D_MODEL ** -0.5),
        'w_ffn1_down': nrm(ks[4], (L, D_FF, D_MODEL), D_FF ** -0.5),
        'norm_mix': gain(ks[5], (L, D_MODEL)),
        'w_in': nrm(ks[6], (L, D_MODEL, D_IN_PROJ), D_MODEL ** -0.5),
        'cmp_pos_k': nrm(ks[7], (L, NSA_CMP_LEN, HEAD_DIM), 0.5),
        'cmp_w1_k': nrm(ks[8], (L, flat_cmp, NSA_CMP_HIDDEN), flat_cmp ** -0.5),
        'cmp_w2_k': nrm(ks[9], (L, NSA_CMP_HIDDEN, HEAD_DIM), NSA_CMP_HIDDEN ** -0.5),
        'cmp_pos_v': nrm(ks[10], (L, NSA_CMP_LEN, HEAD_DIM), 0.5),
        'cmp_w1_v': nrm(ks[11], (L, flat_cmp, NSA_CMP_HIDDEN), flat_cmp ** -0.5),
        'cmp_w2_v': nrm(ks[12], (L, NSA_CMP_HIDDEN, HEAD_DIM), NSA_CMP_HIDDEN ** -0.5),
        'w_out': nrm(ks[13], (L, D_MIX, D_MODEL), D_MIX ** -0.5),
        'norm_ffn2': gain(ks[14], (L, D_MODEL)),
        'w_ffn2_gate': nrm(ks[15], (L, D_MODEL, D_FF), D_MODEL ** -0.5),
        'w_ffn2_up': nrm(ks[16], (L, D_MODEL, D_FF), D_MODEL ** -0.5),
        'w_ffn2_down': nrm(ks[17], (L, D_FF, D_MODEL), D_FF ** -0.5),
        'rel_bias': nrm(ks[18], (REL_BUCKETS, N_BIAS_HEADS), 0.5),
        'norm_final': gain(ks[19], (D_MODEL,)),
    }


def reference(x, norm_ffn1, w_ffn1_gate, w_ffn1_up, w_ffn1_down, norm_mix, w_in, cmp_pos_k, cmp_w1_k, cmp_w2_k, cmp_pos_v, cmp_w1_v, cmp_w2_v, w_out, norm_ffn2, w_ffn2_gate, w_ffn2_up, w_ffn2_down, rel_bias, norm_final):
    for l in range(DEPTH):
        x = x + 0.5 * swiglu(rmsnorm(x, norm_ffn1[l]), w_ffn1_gate[l], w_ffn1_up[l], w_ffn1_down[l])
        x = x + hybrid_mixer(rmsnorm(x, norm_mix[l]), w_in[l], cmp_pos_k[l], cmp_w1_k[l], cmp_w2_k[l], cmp_pos_v[l], cmp_w1_v[l], cmp_w2_v[l], w_out[l], rel_bias)
        x = x + 0.5 * swiglu(rmsnorm(x, norm_ffn2[l]), w_ffn2_gate[l], w_ffn2_up[l], w_ffn2_down[l])
    return rmsnorm(x, norm_final)
```

```python
import functools
import math

import numpy as np
import jax
import jax.numpy as jnp
from jax import lax
from jax.experimental import pallas as pl
from jax.experimental.pallas import tpu as pltpu

D_MODEL = 1024
D_FF = 2816
HEAD_DIM = 64
HEAD_PAD = 128
ONES_LANE = HEAD_DIM
MOBA_HEADS = 8
NSA_HEADS = 8
NSA_GROUPS = 2
NSA_REP = NSA_HEADS // NSA_GROUPS
MOBA_BLOCK = 256
MOBA_TOPK = 3
CMP_LEN = 32
CMP_STRIDE = 16
CMP_HIDDEN = 256
SLC_BLOCK = 64
SLC_TOPN = 16
WINDOW = 512
N_GATES = 3
REL_BUCKETS = 32
REL_MAX_EXACT = REL_BUCKETS // 2
REL_MAX_DIST = 128
RMS_EPS = 1e-6
TQ = 256
NEG = -1e30
VMEM_LIMIT = 56 * 1024 * 1024

SLOT_QA, SLOT_KA, SLOT_VA, SLOT_QB = 0, 8, 16, 24
SLOT_KC, SLOT_VC, SLOT_KS, SLOT_VS, SLOT_KW, SLOT_VW = 32, 34, 36, 38, 40, 42
N_SLOTS = 44

F32 = jnp.float32
BF16 = jnp.bfloat16


def _dot(a, b):
    return jnp.dot(a, b, preferred_element_type=F32)


def _dot_nt(a, b):
    return lax.dot_general(a, b, (((1,), (1,)), ((), ())), preferred_element_type=F32)


def _split_bf16(x):
    hi = x.astype(BF16)
    lo = (x - hi.astype(F32)).astype(BF16)
    return hi, lo


def _rmsnorm(x, g):
    ms = jnp.mean(x * x, axis=-1, keepdims=True)
    return x * lax.rsqrt(ms + RMS_EPS) * g


def _ffn_body(x_ref, g_ref, wg_ref, wu_ref, wd_ref, gf_ref, o_ref, h_ref, acc_ref, *, final_norm):
    j = pl.program_id(1)

    @pl.when(j == 0)
    def _():
        h_ref[...] = _rmsnorm(x_ref[...], g_ref[...]).astype(BF16)
        acc_ref[...] = jnp.zeros_like(acc_ref)

    h = h_ref[...]
    a = _dot(h, wg_ref[...])
    u = _dot(h, wu_ref[...])
    z = (a * jax.nn.sigmoid(a)) * u
    acc_ref[...] += _dot(z.astype(BF16), wd_ref[...])

    @pl.when(j == pl.num_programs(1) - 1)
    def _():
        y = x_ref[...] + 0.5 * acc_ref[...]
        if final_norm:
            y = _rmsnorm(y, gf_ref[...])
        o_ref[...] = y


def _ffn(x2d, g, wg, wu, wd, gf, *, final_norm, tm=512, tf=1408):
    n = x2d.shape[0]
    grid = (n // tm, D_FF // tf)
    return pl.pallas_call(
        functools.partial(_ffn_body, final_norm=final_norm),
        grid=grid,
        in_specs=[
            pl.BlockSpec((tm, D_MODEL), lambda i, j: (i, 0)),
            pl.BlockSpec((1, D_MODEL), lambda i, j: (0, 0)),
            pl.BlockSpec((D_MODEL, tf), lambda i, j: (0, j)),
            pl.BlockSpec((D_MODEL, tf), lambda i, j: (0, j)),
            pl.BlockSpec((tf, D_MODEL), lambda i, j: (j, 0)),
            pl.BlockSpec((1, D_MODEL), lambda i, j: (0, 0)),
        ],
        out_specs=pl.BlockSpec((tm, D_MODEL), lambda i, j: (i, 0)),
        out_shape=jax.ShapeDtypeStruct((n, D_MODEL), F32),
        scratch_shapes=[pltpu.VMEM((tm, D_MODEL), BF16), pltpu.VMEM((tm, D_MODEL), F32)],
        compiler_params=pltpu.CompilerParams(
            dimension_semantics=("parallel", "arbitrary"), vmem_limit_bytes=VMEM_LIMIT),
        name="ffn_final" if final_norm else "ffn",
    )(x2d, g, wg, wu, wd, gf)


def _inproj_body(x_ref, g_ref, w_ref, ones_ref, wgate_ref, p_ref, gates_ref, h_ref):
    j = pl.program_id(1)

    @pl.when(j == 0)
    def _():
        h = _rmsnorm(x_ref[...], g_ref[...]).astype(BF16)
        h_ref[...] = h
        gates_ref[...] = jax.nn.sigmoid(_dot(h, wgate_ref[...]))

    p_ref[...] = (_dot(h_ref[...], w_ref[...]) + ones_ref[...]).astype(BF16)


def _inproj(x2d, g, w, ones_row, wgate, *, tm=512, tn=512):
    n = x2d.shape[0]
    nc = w.shape[1]
    ng = wgate.shape[1]
    return pl.pallas_call(
        _inproj_body,
        grid=(n // tm, nc // tn),
        in_specs=[
            pl.BlockSpec((tm, D_MODEL), lambda i, j: (i, 0)),
            pl.BlockSpec((1, D_MODEL), lambda i, j: (0, 0)),
            pl.BlockSpec((D_MODEL, tn), lambda i, j: (0, j)),
            pl.BlockSpec((1, tn), lambda i, j: (0, j)),
            pl.BlockSpec((D_MODEL, ng), lambda i, j: (0, 0)),
        ],
        out_specs=[
            pl.BlockSpec((tm, tn), lambda i, j: (i, j)),
            pl.BlockSpec((tm, ng), lambda i, j: (i, 0)),
        ],
        out_shape=[jax.ShapeDtypeStruct((n, nc), BF16), jax.ShapeDtypeStruct((n, ng), F32)],
        scratch_shapes=[pltpu.VMEM((tm, D_MODEL), BF16)],
        compiler_params=pltpu.CompilerParams(
            dimension_semantics=("parallel", "arbitrary"), vmem_limit_bytes=VMEM_LIMIT),
        name="inproj",
    )(x2d, g, w, ones_row, wgate)


def _compress_body(raw_ref, pos_ref, w1_ref, w2_ref, o_ref, xs_ref, *, ncp):
    kv = pl.program_id(1)
    xs_ref[...] = raw_ref[0].astype(F32)
    half = CMP_LEN // 2
    y0 = jnp.zeros((ncp, CMP_HIDDEN), F32)
    y1 = jnp.zeros((ncp, CMP_HIDDEN), F32)
    for p in range(half):
        xp = xs_ref[pl.ds(p, ncp, stride=CMP_STRIDE), :]
        y0 = y0 + _dot((xp + pos_ref[0, p:p + 1, :]).astype(BF16), w1_ref[0, p])
        y1 = y1 + _dot((xp + pos_ref[0, half + p:half + p + 1, :]).astype(BF16), w1_ref[0, half + p])
    h1 = y0 + pltpu.roll(y1, ncp - 1, 0)
    act = (h1 * jax.nn.sigmoid(h1)).astype(BF16)
    out = _dot(act, w2_ref[0])
    lane = lax.broadcasted_iota(jnp.int32, out.shape, 1)
    out = out + jnp.where(lane == ONES_LANE, kv.astype(F32), 0.0)
    o_ref[0, 0, 0] = out.astype(BF16)


def _compress(proj, pos, w1, w2):
    b, s, _ = proj.shape
    ncp = s // CMP_STRIDE
    return pl.pallas_call(
        functools.partial(_compress_body, ncp=ncp),
        grid=(b, 2, NSA_GROUPS),
        in_specs=[
            pl.BlockSpec((1, s, HEAD_PAD), lambda bi, kv, g: (bi, 0, SLOT_KC + 2 * kv + g)),
            pl.BlockSpec((1, CMP_LEN, HEAD_PAD), lambda bi, kv, g: (kv, 0, 0)),
            pl.BlockSpec((1, CMP_LEN, HEAD_PAD, CMP_HIDDEN), lambda bi, kv, g: (kv, 0, 0, 0)),
            pl.BlockSpec((1, CMP_HIDDEN, HEAD_PAD), lambda bi, kv, g: (kv, 0, 0)),
        ],
        out_specs=pl.BlockSpec((1, 1, 1, ncp, HEAD_PAD), lambda bi, kv, g: (bi, kv, g, 0, 0)),
        out_shape=jax.ShapeDtypeStruct((b, 2, NSA_GROUPS, ncp, HEAD_PAD), BF16),
        scratch_shapes=[pltpu.VMEM((s, HEAD_PAD), F32)],
        compiler_params=pltpu.CompilerParams(
            dimension_semantics=("parallel", "arbitrary", "arbitrary"), vmem_limit_bytes=VMEM_LIMIT),
        name="compress",
    )(proj, pos, w1, w2)


def _flash_init(s, v, m_ref, acc_ref, idx=None):
    m = jnp.max(s, axis=1, keepdims=True)
    p = jnp.exp(s - m)
    pv = _dot(p.astype(BF16), v)
    if idx is None:
        m_ref[...] = m
        acc_ref[...] = pv
    else:
        m_ref[idx] = m
        acc_ref[idx] = pv


def _flash_update(s, v, m_ref, acc_ref, idx=None):
    m_prev = m_ref[...] if idx is None else m_ref[idx]
    acc_prev = acc_ref[...] if idx is None else acc_ref[idx]
    m_new = jnp.maximum(m_prev, jnp.max(s, axis=1, keepdims=True))
    alpha = jnp.exp(m_prev - m_new)
    p = jnp.exp(s - m_new)
    acc = alpha * acc_prev + _dot(p.astype(BF16), v)
    if idx is None:
        m_ref[...] = m_new
        acc_ref[...] = acc
    else:
        m_ref[idx] = m_new
        acc_ref[idx] = acc


def _normalize(acc):
    return acc / acc[:, ONES_LANE:ONES_LANE + 1]


def _pick_col(x, j):
    lane = lax.broadcasted_iota(jnp.int32, x.shape, 1)
    return jnp.sum(jnp.where(lane == j, x, 0.0), axis=1, keepdims=True)


def _moba_body(c31_ref, q_ref, k_ref, v_ref, bown_ref, bprev_ref, o_ref, kmean_ref, m_ref, acc_ref, *, nb):
    h = pl.program_id(1)
    i = pl.program_id(2)

    @pl.when(i == 0)
    def _():
        kmean_ref[...] = jnp.zeros_like(kmean_ref)
        for n in range(nb):
            kb = k_ref[0, n * MOBA_BLOCK:(n + 1) * MOBA_BLOCK, :].astype(F32)
            kmean_ref[n:n + 1, :] = jnp.mean(kb, axis=0, keepdims=True)

    q = q_ref[0]
    kh, kl = _split_bf16(kmean_ref[...])
    gate = _dot_nt(q, kh) + _dot_nt(q, kl)
    lane = lax.broadcasted_iota(jnp.int32, gate.shape, 1)
    cnt = jnp.zeros(gate.shape, F32)
    for n in range(nb):
        col = gate[:, n:n + 1]
        ahead = jnp.where(col > gate, 1.0, jnp.where((col == gate) & (lane > n), 1.0, 0.0))
        cnt = cnt + ahead * jnp.where(n < i, 1.0, 0.0)
    sel = (lane < i) & (cnt < MOBA_TOPK)
    c31 = c31_ref[h]
    rb_far = jnp.where(sel, c31, NEG)
    rb_prev = jnp.where(sel, 0.0, NEG)

    def tile(j):
        off = pl.multiple_of(j * MOBA_BLOCK, MOBA_BLOCK)
        return k_ref[0, pl.ds(off, MOBA_BLOCK), :], v_ref[0, pl.ds(off, MOBA_BLOCK), :]

    k_i, v_i = tile(i)
    _flash_init(_dot_nt(q, k_i) + bown_ref[0], v_i, m_ref, acc_ref)

    @pl.when(i >= 1)
    def _():
        k_j, v_j = tile(i - 1)
        s = _dot_nt(q, k_j) + bprev_ref[0] + _pick_col(rb_prev, i - 1)
        _flash_update(s, v_j, m_ref, acc_ref)

    def far(j, carry):
        k_j, v_j = tile(j)
        s = _dot_nt(q, k_j) + _pick_col(rb_far, j)
        _flash_update(s, v_j, m_ref, acc_ref)
        return carry

    lax.fori_loop(0, jnp.maximum(i - 1, 0), far, 0)
    o_ref[0] = _normalize(acc_ref[...]).astype(BF16)


def _moba(proj, bias_own, bias_prev, c31):
    b, s, _ = proj.shape
    nb = s // MOBA_BLOCK
    return pl.pallas_call(
        functools.partial(_moba_body, nb=nb),
        grid=(b, MOBA_HEADS, nb),
        in_specs=[
            pl.BlockSpec(memory_space=pltpu.SMEM),
            pl.BlockSpec((1, TQ, HEAD_PAD), lambda bi, h, i: (bi, i, SLOT_QA + h)),
            pl.BlockSpec((1, s, HEAD_PAD), lambda bi, h, i: (bi, 0, SLOT_KA + h)),
            pl.BlockSpec((1, s, HEAD_PAD), lambda bi, h, i: (bi, 0, SLOT_VA + h)),
            pl.BlockSpec((1, TQ, TQ), lambda bi, h, i: (h, 0, 0)),
            pl.BlockSpec((1, TQ, TQ), lambda bi, h, i: (h, 0, 0)),
        ],
        out_specs=pl.BlockSpec((1, TQ, HEAD_PAD), lambda bi, h, i: (bi, i, h)),
        out_shape=jax.ShapeDtypeStruct((b, s, MOBA_HEADS * HEAD_PAD), BF16),
        scratch_shapes=[
            pltpu.VMEM((HEAD_PAD, HEAD_PAD), F32),
            pltpu.VMEM((TQ, 1), F32),
            pltpu.VMEM((TQ, HEAD_PAD), F32),
        ],
        compiler_params=pltpu.CompilerParams(
            dimension_semantics=("parallel", "parallel", "arbitrary"), vmem_limit_bytes=VMEM_LIMIT),
        name="moba",
    )(c31, proj, proj, proj, bias_own, bias_prev)


def _nsa_body(c31_ref, q_ref, ks_ref, vs_ref, kw_ref, vw_ref, kc_ref, vc_ref, gates_ref,
              bown_ref, bprev_ref, bcmp_ref, ovt_ref, o_ref,
              ms_ref, accs_ref, mw_ref, accw_ref, ocmp_ref, *, ncp, nslc):
    g = pl.program_id(1)
    i = pl.program_id(2)
    t0 = i * TQ
    qs = [q_ref[0, :, r * HEAD_PAD:(r + 1) * HEAD_PAD] for r in range(NSA_REP)]
    c31 = [c31_ref[MOBA_HEADS + g * NSA_REP + r] for r in range(NSA_REP)]

    kc = kc_ref[0, 0, 0]
    vc = vc_ref[0, 0, 0]
    half_w = ncp // 2
    lane_c = lax.broadcasted_iota(jnp.int32, (TQ, ncp), 1)
    rel = lane_c - i * (TQ // CMP_STRIDE)
    shift = lax.rem(i * (TQ // CMP_STRIDE) + half_w, ncp)
    psum = jnp.zeros((TQ, ncp), F32)
    for r in range(NSA_REP):
        band = pltpu.roll(bcmp_ref[r], shift, 1)
        bias = jnp.where(rel < -half_w, c31[r], jnp.where(rel >= half_w, NEG, band))
        s = _dot_nt(qs[r], kc) + bias
        m = jnp.max(s, axis=1, keepdims=True)
        p = jnp.where(s > 0.5 * NEG, jnp.exp(s - m), 0.0)
        den = jnp.sum(p, axis=1, keepdims=True)
        pn = p / jnp.where(den > 0.0, den, 1.0)
        ocmp_ref[r] = _dot(pn.astype(BF16), vc)
        psum = psum + pn

    ph, plo = _split_bf16(psum)
    ovt = ovt_ref[...]
    imp_t = _dot_nt(ovt, ph) + _dot_nt(ovt, plo)
    blk = lax.broadcasted_iota(jnp.int32, imp_t.shape, 0)
    cur = (t0 + lax.broadcasted_iota(jnp.int32, imp_t.shape, 1)) // SLC_BLOCK
    forced = (blk == 0) | (blk == cur) | (blk == cur - 1)
    score = jnp.where(forced, 1e30, jnp.where(blk <= cur, imp_t, -1.0))
    score = jnp.where(blk < nslc, score, -2.0)
    cnt = jnp.zeros(score.shape, F32)
    for j in range(nslc):
        row = score[j:j + 1, :]
        cnt = cnt + jnp.where(row > score, 1.0, jnp.where((row == score) & (blk > j), 1.0, 0.0))
    selb_t = jnp.where((cnt < SLC_TOPN) & (blk < nslc), 0.0, NEG)
    selb = jnp.transpose(selb_t).astype(BF16)

    def slc_mask(j):
        eb = lax.broadcasted_iota(jnp.int32, (HEAD_PAD, TQ), 0)
        ec = lax.broadcasted_iota(jnp.int32, (HEAD_PAD, TQ), 1)
        e = jnp.where(eb == j * (TQ // SLC_BLOCK) + ec // SLC_BLOCK, 1.0, 0.0).astype(BF16)
        return _dot(selb, e)

    def tile(k_ref, v_ref, j):
        off = pl.multiple_of(j * TQ, TQ)
        return k_ref[0, pl.ds(off, TQ), :], v_ref[0, pl.ds(off, TQ), :]

    k_i, v_i = tile(ks_ref, vs_ref, i)
    mask_i = slc_mask(i)
    for r in range(NSA_REP):
        _flash_init(_dot_nt(qs[r], k_i) + mask_i + bown_ref[r], v_i, ms_ref, accs_ref, r)

    @pl.when(i >= 1)
    def _():
        k_j, v_j = tile(ks_ref, vs_ref, i - 1)
        mask_j = slc_mask(i - 1)
        for r in range(NSA_REP):
            _flash_update(_dot_nt(qs[r], k_j) + mask_j + bprev_ref[r], v_j, ms_ref, accs_ref, r)

    def far(j, carry):
        k_j, v_j = tile(ks_ref, vs_ref, j)
        mask_j = slc_mask(j)
        for r in range(NSA_REP):
            _flash_update(_dot_nt(qs[r], k_j) + mask_j + c31[r], v_j, ms_ref, accs_ref, r)
        return carry

    lax.fori_loop(0, jnp.maximum(i - 1, 0), far, 0)

    k_i, v_i = tile(kw_ref, vw_ref, i)
    for r in range(NSA_REP):
        _flash_init(_dot_nt(qs[r], k_i) + bown_ref[r], v_i, mw_ref, accw_ref, r)

    @pl.when(i >= 1)
    def _():
        k_j, v_j = tile(kw_ref, vw_ref, i - 1)
        for r in range(NSA_REP):
            _flash_update(_dot_nt(qs[r], k_j) + bprev_ref[r], v_j, mw_ref, accw_ref, r)

    @pl.when(i >= 2)
    def _():
        k_j, v_j = tile(kw_ref, vw_ref, i - 2)
        rr = lax.broadcasted_iota(jnp.int32, (TQ, TQ), 0)
        cc = lax.broadcasted_iota(jnp.int32, (TQ, TQ), 1)
        for r in range(NSA_REP):
            s = _dot_nt(qs[r], k_j) + jnp.where(rr < cc, c31[r], NEG)
            _flash_update(s, v_j, mw_ref, accw_ref, r)

    gt = gates_ref[0]
    for r in range(NSA_REP):
        g_cmp = gt[:, N_GATES * r + 0:N_GATES * r + 1]
        g_slc = gt[:, N_GATES * r + 1:N_GATES * r + 2]
        g_win = gt[:, N_GATES * r + 2:N_GATES * r + 3]
        out = g_cmp * ocmp_ref[r] + g_slc * _normalize(accs_ref[r]) + g_win * _normalize(accw_ref[r])
        o_ref[0, :, r * HEAD_PAD:(r + 1) * HEAD_PAD] = out.astype(BF16)


def _nsa(proj, cmp_kv, gates, bias_own, bias_prev, bias_cmp, ovt, c31):
    b, s, _ = proj.shape
    nq = s // TQ
    ncp = s // CMP_STRIDE
    nslc = s // SLC_BLOCK
    qw = NSA_REP * HEAD_PAD
    kv_spec = lambda slot: pl.BlockSpec((1, s, HEAD_PAD), lambda bi, g, i: (bi, 0, slot + g))
    head_blk = MOBA_HEADS // NSA_REP
    return pl.pallas_call(
        functools.partial(_nsa_body, ncp=ncp, nslc=nslc),
        grid=(b, NSA_GROUPS, nq),
        in_specs=[
            pl.BlockSpec(memory_space=pltpu.SMEM),
            pl.BlockSpec((1, TQ, qw), lambda bi, g, i: (bi, i, SLOT_QB // NSA_REP + g)),
            kv_spec(SLOT_KS), kv_spec(SLOT_VS), kv_spec(SLOT_KW), kv_spec(SLOT_VW),
            pl.BlockSpec((1, 1, 1, ncp, HEAD_PAD), lambda bi, g, i: (bi, 0, g, 0, 0)),
            pl.BlockSpec((1, 1, 1, ncp, HEAD_PAD), lambda bi, g, i: (bi, 1, g, 0, 0)),
            pl.BlockSpec((1, TQ, HEAD_PAD), lambda bi, g, i: (bi, i, g)),
            pl.BlockSpec((NSA_REP, TQ, TQ), lambda bi, g, i: (head_blk + g, 0, 0)),
            pl.BlockSpec((NSA_REP, TQ, TQ), lambda bi, g, i: (head_blk + g, 0, 0)),
            pl.BlockSpec((NSA_REP, TQ, ncp), lambda bi, g, i: (g, 0, 0)),
            pl.BlockSpec((HEAD_PAD, ncp), lambda bi, g, i: (0, 0)),
        ],
        out_specs=pl.BlockSpec((1, TQ, qw), lambda bi, g, i: (bi, i, g)),
        out_shape=jax.ShapeDtypeStruct((b, s, NSA_HEADS * HEAD_PAD), BF16),
        scratch_shapes=[
            pltpu.VMEM((NSA_REP, TQ, 1), F32),
            pltpu.VMEM((NSA_REP, TQ, HEAD_PAD), F32),
            pltpu.VMEM((NSA_REP, TQ, 1), F32),
            pltpu.VMEM((NSA_REP, TQ, HEAD_PAD), F32),
            pltpu.VMEM((NSA_REP, TQ, HEAD_PAD), F32),
        ],
        compiler_params=pltpu.CompilerParams(
            dimension_semantics=("parallel", "parallel", "arbitrary"), vmem_limit_bytes=VMEM_LIMIT),
        name="nsa",
    )(c31, proj, proj, proj, proj, proj, cmp_kv, cmp_kv, gates, bias_own, bias_prev, bias_cmp, ovt)


def _outproj_body(x_ref, oa_ref, ob_ref, wa_ref, wb_ref, o_ref):
    o_ref[...] = x_ref[...] + _dot(oa_ref[...], wa_ref[...]) + _dot(ob_ref[...], wb_ref[...])


def _outproj(x2d, oa, ob, wa, wb, *, tm=512):
    n = x2d.shape[0]
    ka = oa.shape[1]
    kb = ob.shape[1]
    return pl.pallas_call(
        _outproj_body,
        grid=(n // tm,),
        in_specs=[
            pl.BlockSpec((tm, D_MODEL), lambda i: (i, 0)),
            pl.BlockSpec((tm, ka), lambda i: (i, 0)),
            pl.BlockSpec((tm, kb), lambda i: (i, 0)),
            pl.BlockSpec((ka, D_MODEL), lambda i: (0, 0)),
            pl.BlockSpec((kb, D_MODEL), lambda i: (0, 0)),
        ],
        out_specs=pl.BlockSpec((tm, D_MODEL), lambda i: (i, 0)),
        out_shape=jax.ShapeDtypeStruct((n, D_MODEL), F32),
        compiler_params=pltpu.CompilerParams(
            dimension_semantics=("parallel",), vmem_limit_bytes=VMEM_LIMIT),
        name="outproj",
    )(x2d, oa, ob, wa, wb)


def _rel_bucket(dist):
    n = jnp.maximum(dist, 0)
    nf = jnp.maximum(n, 1).astype(F32)
    large = REL_MAX_EXACT + (jnp.log(nf / REL_MAX_EXACT) / math.log(REL_MAX_DIST / REL_MAX_EXACT)
                             * (REL_BUCKETS - REL_MAX_EXACT)).astype(jnp.int32)
    return jnp.where(n < REL_MAX_EXACT, n, jnp.minimum(large, REL_BUCKETS - 1))


def _bias_tables(rel_bias, ncp):
    tab = rel_bias.T.astype(F32)
    r = jnp.arange(TQ)[:, None]
    c = jnp.arange(TQ)[None, :]
    d_own = r - c
    own = jnp.where(d_own >= 0, tab[:, _rel_bucket(d_own)], NEG)
    prev = tab[:, _rel_bucket(d_own + TQ)]
    u = jnp.arange(ncp)[None, :]
    d_cmp = r - CMP_STRIDE * (u - ncp // 2) - (CMP_LEN - 1)
    cmp_band = jnp.where(d_cmp >= 0, tab[MOBA_HEADS:, _rel_bucket(d_cmp)], NEG)
    return own, prev, cmp_band, tab[:, REL_BUCKETS - 1]


def _pad_heads(w, n_heads):
    rows = w.shape[0]
    w = w.reshape(rows, n_heads, HEAD_DIM)
    w = jnp.pad(w, ((0, 0), (0, 0), (0, HEAD_PAD - HEAD_DIM)))
    return w.reshape(rows, n_heads * HEAD_PAD)


def _slc_overlap_t(ncp, nslc):
    ratio = SLC_BLOCK // CMP_STRIDE
    mc = CMP_LEN // CMP_STRIDE
    w = np.zeros((HEAD_PAD, nslc * ratio + mc), np.float32)
    for j in range(nslc):
        for a in range(ratio):
            for m in range(mc):
                w[j, ratio * j + a + m] += 1.0
    n_cmp = ncp - 1
    out = np.zeros((HEAD_PAD, ncp), np.float32)
    out[:, :n_cmp] = w[:, :n_cmp]
    return out


def kernel(x, norm_ffn1, w_ffn1_gate, w_ffn1_up, w_ffn1_down, norm_mix, w_in, cmp_pos_k, cmp_w1_k, cmp_w2_k,
           cmp_pos_v, cmp_w1_v, cmp_w2_v, w_out, norm_ffn2, w_ffn2_gate, w_ffn2_up, w_ffn2_down, rel_bias,
           norm_final):
    b, s, d = x.shape
    assert d == D_MODEL and s % (2 * TQ) == 0 and s // SLC_BLOCK <= HEAD_PAD and norm_ffn1.shape[0] == 1
    n = b * s
    ncp = s // CMP_STRIDE
    nslc = s // SLC_BLOCK
    scale = HEAD_DIM ** -0.5

    sizes = (512, 512, 512, 512) + (128,) * 6 + (24,)
    splits = np.cumsum(sizes)[:-1].tolist()
    wqa, wka, wva, wqb, wkc, wvc, wks, wvs, wkw, wvw, wgb = jnp.split(w_in[0], splits, axis=1)
    w_pad = jnp.concatenate([
        _pad_heads(wqa * scale, 8), _pad_heads(wka, 8), _pad_heads(wva, 8), _pad_heads(wqb * scale, 8),
        _pad_heads(wkc, 2), _pad_heads(wvc, 2), _pad_heads(wks, 2), _pad_heads(wvs, 2),
        _pad_heads(wkw, 2), _pad_heads(wvw, 2)], axis=1).astype(BF16)
    ones_np = np.zeros((1, N_SLOTS * HEAD_PAD), np.float32)
    for slot in list(range(SLOT_VA, SLOT_VA + 8)) + [SLOT_VS, SLOT_VS + 1, SLOT_VW, SLOT_VW + 1]:
        ones_np[0, slot * HEAD_PAD + ONES_LANE] = 1.0
    ones_row = jnp.asarray(ones_np)
    wgate = wgb.reshape(D_MODEL, NSA_GROUPS, NSA_REP * N_GATES)
    wgate = jnp.pad(wgate, ((0, 0), (0, 0), (0, HEAD_PAD - NSA_REP * N_GATES)))
    wgate = wgate.reshape(D_MODEL, NSA_GROUPS * HEAD_PAD).astype(BF16)

    def cmp_params(pos, w1, w2):
        pos_p = jnp.pad(pos[0], ((0, 0), (0, HEAD_PAD - HEAD_DIM)))
        w1_p = jnp.pad(w1[0].reshape(CMP_LEN, HEAD_DIM, CMP_HIDDEN), ((0, 0), (0, HEAD_PAD - HEAD_DIM), (0, 0)))
        w2_p = jnp.pad(w2[0], ((0, 0), (0, HEAD_PAD - HEAD_DIM)))
        return pos_p, w1_p.astype(BF16), w2_p.astype(BF16)

    pk, w1k, w2k = cmp_params(cmp_pos_k, cmp_w1_k, cmp_w2_k)
    pv, w1v, w2v = cmp_params(cmp_pos_v, cmp_w1_v, cmp_w2_v)
    cmp_pos = jnp.stack([pk, pv])
    cmp_w1 = jnp.stack([w1k, w1v])
    cmp_w2 = jnp.stack([w2k, w2v])

    wo = w_out[0]
    wo_a = _pad_heads(wo[:MOBA_HEADS * HEAD_DIM].T, MOBA_HEADS).T.astype(BF16)
    wo_b = _pad_heads(wo[MOBA_HEADS * HEAD_DIM:].T, NSA_HEADS).T.astype(BF16)

    bias_own, bias_prev, bias_cmp, c31 = _bias_tables(rel_bias, ncp)
    ovt = jnp.asarray(_slc_overlap_t(ncp, nslc)).astype(BF16)

    x2d = x.reshape(n, D_MODEL)
    nf = norm_final.reshape(1, D_MODEL)
    x1 = _ffn(x2d, norm_ffn1, w_ffn1_gate[0].astype(BF16), w_ffn1_up[0].astype(BF16),
              w_ffn1_down[0].astype(BF16), nf, final_norm=False)
    proj, gates = _inproj(x1, norm_mix, w_pad, ones_row, wgate)
    proj = proj.reshape(b, s, N_SLOTS * HEAD_PAD)
    gates = gates.reshape(b, s, NSA_GROUPS * HEAD_PAD)
    cmp_kv = _compress(proj, cmp_pos, cmp_w1, cmp_w2)
    o_a = _moba(proj, bias_own, bias_prev, c31)
    o_b = _nsa(proj, cmp_kv, gates, bias_own, bias_prev, bias_cmp, ovt, c31)
    x2 = _outproj(x1, o_a.reshape(n, -1), o_b.reshape(n, -1), wo_a, wo_b)
    y = _ffn(x2, norm_ffn2, w_ffn2_gate[0].astype(BF16), w_ffn2_up[0].astype(BF16),
             w_ffn2_down[0].astype(BF16), nf, final_norm=True)
    return y.reshape(b, s, D_MODEL)
```

```python
import functools
import math

import numpy as np
import jax
import jax.numpy as jnp
from jax import lax
from jax.experimental import pallas as pl
from jax.experimental.pallas import tpu as pltpu

D_MODEL = 1024
D_FF = 2816
HEAD_DIM = 64
HEAD_PAD = 128
ONES_LANE = HEAD_DIM
MOBA_HEADS = 8
NSA_HEADS = 8
NSA_GROUPS = 2
NSA_REP = NSA_HEADS // NSA_GROUPS
MOBA_BLOCK = 256
MOBA_TOPK = 3
CMP_LEN = 32
CMP_STRIDE = 16
CMP_HIDDEN = 256
SLC_BLOCK = 64
SLC_TOPN = 16
WINDOW = 512
N_GATES = 3
REL_BUCKETS = 32
REL_MAX_EXACT = REL_BUCKETS // 2
REL_MAX_DIST = 128
RMS_EPS = 1e-6
TQ = 256
NEG = -1e30
VMEM_LIMIT = 56 * 1024 * 1024

SLOT_QA, SLOT_KA, SLOT_VA, SLOT_QB = 0, 8, 16, 24
SLOT_KC, SLOT_VC, SLOT_KS, SLOT_VS, SLOT_KW, SLOT_VW = 32, 34, 36, 38, 40, 42
N_SLOTS = 44

F32 = jnp.float32
BF16 = jnp.bfloat16


def _dot(a, b):
    return jnp.dot(a, b, preferred_element_type=F32)


def _dot_nt(a, b):
    return lax.dot_general(a, b, (((1,), (1,)), ((), ())), preferred_element_type=F32)


def _split_bf16(x):
    hi = x.astype(BF16)
    lo = (x - hi.astype(F32)).astype(BF16)
    return hi, lo


def _rmsnorm(x, g):
    ms = jnp.mean(x * x, axis=-1, keepdims=True)
    return x * lax.rsqrt(ms + RMS_EPS) * g


def _ffn_body(x_ref, g_ref, wg_ref, wu_ref, wd_ref, gf_ref, o_ref, h_ref, acc_ref, *, final_norm):
    j = pl.program_id(1)

    @pl.when(j == 0)
    def _():
        h_ref[...] = _rmsnorm(x_ref[...], g_ref[...]).astype(BF16)
        acc_ref[...] = jnp.zeros_like(acc_ref)

    h = h_ref[...]
    a = _dot(h, wg_ref[...])
    u = _dot(h, wu_ref[...])
    z = (a * jax.nn.sigmoid(a)) * u
    acc_ref[...] += _dot(z.astype(BF16), wd_ref[...])

    @pl.when(j == pl.num_programs(1) - 1)
    def _():
        y = x_ref[...] + 0.5 * acc_ref[...]
        if final_norm:
            y = _rmsnorm(y, gf_ref[...])
        o_ref[...] = y


def _ffn(x2d, g, wg, wu, wd, gf, *, final_norm, tm=512, tf=1408):
    n = x2d.shape[0]
    grid = (n // tm, D_FF // tf)
    return pl.pallas_call(
        functools.partial(_ffn_body, final_norm=final_norm),
        grid=grid,
        in_specs=[
            pl.BlockSpec((tm, D_MODEL), lambda i, j: (i, 0)),
            pl.BlockSpec((1, D_MODEL), lambda i, j: (0, 0)),
            pl.BlockSpec((D_MODEL, tf), lambda i, j: (0, j)),
            pl.BlockSpec((D_MODEL, tf), lambda i, j: (0, j)),
            pl.BlockSpec((tf, D_MODEL), lambda i, j: (j, 0)),
            pl.BlockSpec((1, D_MODEL), lambda i, j: (0, 0)),
        ],
        out_specs=pl.BlockSpec((tm, D_MODEL), lambda i, j: (i, 0)),
        out_shape=jax.ShapeDtypeStruct((n, D_MODEL), F32),
        scratch_shapes=[pltpu.VMEM((tm, D_MODEL), BF16), pltpu.VMEM((tm, D_MODEL), F32)],
        compiler_params=pltpu.CompilerParams(
            dimension_semantics=("parallel", "arbitrary"), vmem_limit_bytes=VMEM_LIMIT),
        name="ffn_final" if final_norm else "ffn",
    )(x2d, g, wg, wu, wd, gf)


def _inproj_body(x_ref, g_ref, w_ref, ones_ref, wgate_ref, p_ref, gates_ref, h_ref):
    j = pl.program_id(1)

    @pl.when(j == 0)
    def _():
        h = _rmsnorm(x_ref[...], g_ref[...]).astype(BF16)
        h_ref[...] = h
        gates_ref[...] = jax.nn.sigmoid(_dot(h, wgate_ref[...]))

    p_ref[...] = (_dot(h_ref[...], w_ref[...]) + ones_ref[...]).astype(BF16)


def _inproj(x2d, g, w, ones_row, wgate, *, tm=512, tn=512):
    n = x2d.shape[0]
    nc = w.shape[1]
    ng = wgate.shape[1]
    return pl.pallas_call(
        _inproj_body,
        grid=(n // tm, nc // tn),
        in_specs=[
            pl.BlockSpec((tm, D_MODEL), lambda i, j: (i, 0)),
            pl.BlockSpec((1, D_MODEL), lambda i, j: (0, 0)),
            pl.BlockSpec((D_MODEL, tn), lambda i, j: (0, j)),
            pl.BlockSpec((1, tn), lambda i, j: (0, j)),
            pl.BlockSpec((D_MODEL, ng), lambda i, j: (0, 0)),
        ],
        out_specs=[
            pl.BlockSpec((tm, tn), lambda i, j: (i, j)),
            pl.BlockSpec((tm, ng), lambda i, j: (i, 0)),
        ],
        out_shape=[jax.ShapeDtypeStruct((n, nc), BF16), jax.ShapeDtypeStruct((n, ng), F32)],
        scratch_shapes=[pltpu.VMEM((tm, D_MODEL), BF16)],
        compiler_params=pltpu.CompilerParams(
            dimension_semantics=("parallel", "arbitrary"), vmem_limit_bytes=VMEM_LIMIT),
        name="inproj",
    )(x2d, g, w, ones_row, wgate)


def _compress_body(raw_ref, pos_ref, w1_ref, w2_ref, o_ref, xs_ref, *, ncp):
    kv = pl.program_id(1)
    xs_ref[...] = raw_ref[0].astype(F32)
    half = CMP_LEN // 2
    y0 = jnp.zeros((ncp, CMP_HIDDEN), F32)
    y1 = jnp.zeros((ncp, CMP_HIDDEN), F32)
    for p in range(half):
        xp = xs_ref[pl.ds(p, ncp, stride=CMP_STRIDE), :]
        y0 = y0 + _dot((xp + pos_ref[0, p:p + 1, :]).astype(BF16), w1_ref[0, p])
        y1 = y1 + _dot((xp + pos_ref[0, half + p:half + p + 1, :]).astype(BF16), w1_ref[0, half + p])
    h1 = y0 + pltpu.roll(y1, ncp - 1, 0)
    act = (h1 * jax.nn.sigmoid(h1)).astype(BF16)
    out = _dot(act, w2_ref[0])
    lane = lax.broadcasted_iota(jnp.int32, out.shape, 1)
    out = out + jnp.where(lane == ONES_LANE, kv.astype(F32), 0.0)
    o_ref[0, 0, 0] = out.astype(BF16)


def _compress(proj, pos, w1, w2):
    b, s, _ = proj.shape
    ncp = s // CMP_STRIDE
    return pl.pallas_call(
        functools.partial(_compress_body, ncp=ncp),
        grid=(b, 2, NSA_GROUPS),
        in_specs=[
            pl.BlockSpec((1, s, HEAD_PAD), lambda bi, kv, g: (bi, 0, SLOT_KC + 2 * kv + g)),
            pl.BlockSpec((1, CMP_LEN, HEAD_PAD), lambda bi, kv, g: (kv, 0, 0)),
            pl.BlockSpec((1, CMP_LEN, HEAD_PAD, CMP_HIDDEN), lambda bi, kv, g: (kv, 0, 0, 0)),
            pl.BlockSpec((1, CMP_HIDDEN, HEAD_PAD), lambda bi, kv, g: (kv, 0, 0)),
        ],
        out_specs=pl.BlockSpec((1, 1, 1, ncp, HEAD_PAD), lambda bi, kv, g: (bi, kv, g, 0, 0)),
        out_shape=jax.ShapeDtypeStruct((b, 2, NSA_GROUPS, ncp, HEAD_PAD), BF16),
        scratch_shapes=[pltpu.VMEM((s, HEAD_PAD), F32)],
        compiler_params=pltpu.CompilerParams(
            dimension_semantics=("parallel", "arbitrary", "arbitrary"), vmem_limit_bytes=VMEM_LIMIT),
        name="compress",
    )(proj, pos, w1, w2)


def _flash_init(s, v, m_ref, acc_ref, idx=None):
    m = jnp.max(s, axis=1, keepdims=True)
    p = jnp.exp(s - m)
    pv = _dot(p.astype(BF16), v)
    if idx is None:
        m_ref[...] = m
        acc_ref[...] = pv
    else:
        m_ref[idx] = m
        acc_ref[idx] = pv


def _flash_update(s, v, m_ref, acc_ref, idx=None):
    m_prev = m_ref[...] if idx is None else m_ref[idx]
    acc_prev = acc_ref[...] if idx is None else acc_ref[idx]
    m_new = jnp.maximum(m_prev, jnp.max(s, axis=1, keepdims=True))
    alpha = jnp.exp(m_prev - m_new)
    p = jnp.exp(s - m_new)
    acc = alpha * acc_prev + _dot(p.astype(BF16), v)
    if idx is None:
        m_ref[...] = m_new
        acc_ref[...] = acc
    else:
        m_ref[idx] = m_new
        acc_ref[idx] = acc


def _normalize(acc):
    return acc / acc[:, ONES_LANE:ONES_LANE + 1]


def _pick_col(x, j):
    lane = lax.broadcasted_iota(jnp.int32, x.shape, 1)
    return jnp.sum(jnp.where(lane == j, x, 0.0), axis=1, keepdims=True)


def _moba_body(c31_ref, q_ref, k_ref, v_ref, bown_ref, bprev_ref, o_ref, kmean_ref, m_ref, acc_ref, *, nb):
    h = pl.program_id(1)
    i = pl.program_id(2)

    @pl.when(i == 0)
    def _():
        kmean_ref[...] = jnp.zeros_like(kmean_ref)
        for n in range(nb):
            kb = k_ref[0, n * MOBA_BLOCK:(n + 1) * MOBA_BLOCK, :].astype(F32)
            kmean_ref[n:n + 1, :] = jnp.mean(kb, axis=0, keepdims=True)

    q = q_ref[0]
    kh, kl = _split_bf16(kmean_ref[...])
    gate = _dot_nt(q, kh) + _dot_nt(q, kl)
    lane = lax.broadcasted_iota(jnp.int32, gate.shape, 1)
    cnt = jnp.zeros(gate.shape, F32)
    for n in range(nb):
        col = gate[:, n:n + 1]
        ahead = jnp.where(col > gate, 1.0, jnp.where((col == gate) & (lane > n), 1.0, 0.0))
        cnt = cnt + ahead * jnp.where(n < i, 1.0, 0.0)
    sel = (lane < i) & (cnt < MOBA_TOPK)
    c31 = c31_ref[h]
    rb_far = jnp.where(sel, c31, NEG)
    rb_prev = jnp.where(sel, 0.0, NEG)

    def tile(j):
        off = pl.multiple_of(j * MOBA_BLOCK, MOBA_BLOCK)
        return k_ref[0, pl.ds(off, MOBA_BLOCK), :], v_ref[0, pl.ds(off, MOBA_BLOCK), :]

    k_i, v_i = tile(i)
    _flash_init(_dot_nt(q, k_i) + bown_ref[0], v_i, m_ref, acc_ref)

    @pl.when(i >= 1)
    def _():
        k_j, v_j = tile(i - 1)
        s = _dot_nt(q, k_j) + bprev_ref[0] + _pick_col(rb_prev, i - 1)
        _flash_update(s, v_j, m_ref, acc_ref)

    def far(j, carry):
        k_j, v_j = tile(j)
        s = _dot_nt(q, k_j) + _pick_col(rb_far, j)
        _flash_update(s, v_j, m_ref, acc_ref)
        return carry

    lax.fori_loop(0, jnp.maximum(i - 1, 0), far, 0)
    o_ref[0] = _normalize(acc_ref[...]).astype(BF16)


def _moba(proj, bias_own, bias_prev, c31):
    b, s, _ = proj.shape
    nb = s // MOBA_BLOCK
    return pl.pallas_call(
        functools.partial(_moba_body, nb=nb),
        grid=(b, MOBA_HEADS, nb),
        in_specs=[
            pl.BlockSpec(memory_space=pltpu.SMEM),
            pl.BlockSpec((1, TQ, HEAD_PAD), lambda bi, h, i: (bi, i, SLOT_QA + h)),
            pl.BlockSpec((1, s, HEAD_PAD), lambda bi, h, i: (bi, 0, SLOT_KA + h)),
            pl.BlockSpec((1, s, HEAD_PAD), lambda bi, h, i: (bi, 0, SLOT_VA + h)),
            pl.BlockSpec((1, TQ, TQ), lambda bi, h, i: (h, 0, 0)),
            pl.BlockSpec((1, TQ, TQ), lambda bi, h, i: (h, 0, 0)),
        ],
        out_specs=pl.BlockSpec((1, TQ, HEAD_PAD), lambda bi, h, i: (bi, i, h)),
        out_shape=jax.ShapeDtypeStruct((b, s, MOBA_HEADS * HEAD_PAD), BF16),
        scratch_shapes=[
            pltpu.VMEM((HEAD_PAD, HEAD_PAD), F32),
            pltpu.VMEM((TQ, 1), F32),
            pltpu.VMEM((TQ, HEAD_PAD), F32),
        ],
        compiler_params=pltpu.CompilerParams(
            dimension_semantics=("parallel", "parallel", "arbitrary"), vmem_limit_bytes=VMEM_LIMIT),
        name="moba",
    )(c31, proj, proj, proj, bias_own, bias_prev)


def _nsa_body(c31_ref, q_ref, ks_ref, vs_ref, kw_ref, vw_ref, kc_ref, vc_ref, gates_ref,
              bown_ref, bprev_ref, bcmp_ref, ovt_ref, o_ref,
              ms_ref, accs_ref, mw_ref, accw_ref, ocmp_ref, *, ncp, nslc):
    g = pl.program_id(1)
    i = pl.program_id(2)
    t0 = i * TQ
    qs = [q_ref[0, :, r * HEAD_PAD:(r + 1) * HEAD_PAD] for r in range(NSA_REP)]
    c31 = [c31_ref[MOBA_HEADS + g * NSA_REP + r] for r in range(NSA_REP)]

    kc = kc_ref[0, 0, 0]
    vc = vc_ref[0, 0, 0]
    half_w = ncp // 2
    lane_c = lax.broadcasted_iota(jnp.int32, (TQ, ncp), 1)
    rel = lane_c - i * (TQ // CMP_STRIDE)
    shift = lax.rem(i * (TQ // CMP_STRIDE) + half_w, ncp)
    psum = jnp.zeros((TQ, ncp), F32)
    for r in range(NSA_REP):
        band = pltpu.roll(bcmp_ref[r], shift, 1)
        bias = jnp.where(rel < -half_w, c31[r], jnp.where(rel >= half_w, NEG, band))
        s = _dot_nt(qs[r], kc) + bias
        m = jnp.max(s, axis=1, keepdims=True)
        p = jnp.where(s > 0.5 * NEG, jnp.exp(s - m), 0.0)
        den = jnp.sum(p, axis=1, keepdims=True)
        pn = p / jnp.where(den > 0.0, den, 1.0)
        ocmp_ref[r] = _dot(pn.astype(BF16), vc)
        psum = psum + pn

    ph, plo = _split_bf16(psum)
    ovt = ovt_ref[...]
    imp_t = _dot_nt(ovt, ph) + _dot_nt(ovt, plo)
    blk = lax.broadcasted_iota(jnp.int32, imp_t.shape, 0)
    cur = (t0 + lax.broadcasted_iota(jnp.int32, imp_t.shape, 1)) // SLC_BLOCK
    forced = (blk == 0) | (blk == cur) | (blk == cur - 1)
    score = jnp.where(forced, 1e30, jnp.where(blk <= cur, imp_t, -1.0))
    score = jnp.where(blk < nslc, score, -2.0)
    cnt = jnp.zeros(score.shape, F32)
    for j in range(nslc):
        row = score[j:j + 1, :]
        cnt = cnt + jnp.where(row > score, 1.0, jnp.where((row == score) & (blk > j), 1.0, 0.0))
    selb_t = jnp.where((cnt < SLC_TOPN) & (blk < nslc), 0.0, NEG)
    selb = jnp.transpose(selb_t).astype(BF16)

    def slc_mask(j):
        eb = lax.broadcasted_iota(jnp.int32, (HEAD_PAD, TQ), 0)
        ec = lax.broadcasted_iota(jnp.int32, (HEAD_PAD, TQ), 1)
        e = jnp.where(eb == j * (TQ // SLC_BLOCK) + ec // SLC_BLOCK, 1.0, 0.0).astype(BF16)
        return _dot(selb, e)

    def tile(k_ref, v_ref, j):
        off = pl.multiple_of(j * TQ, TQ)
        return k_ref[0, pl.ds(off, TQ), :], v_ref[0, pl.ds(off, TQ), :]

    k_i, v_i = tile(ks_ref, vs_ref, i)
    mask_i = slc_mask(i)
    for r in range(NSA_REP):
        _flash_init(_dot_nt(qs[r], k_i) + mask_i + bown_ref[r], v_i, ms_ref, accs_ref, r)

    @pl.when(i >= 1)
    def _():
        k_j, v_j = tile(ks_ref, vs_ref, i - 1)
        mask_j = slc_mask(i - 1)
        for r in range(NSA_REP):
            _flash_update(_dot_nt(qs[r], k_j) + mask_j + bprev_ref[r], v_j, ms_ref, accs_ref, r)

    def far(j, carry):
        k_j, v_j = tile(ks_ref, vs_ref, j)
        mask_j = slc_mask(j)
        for r in range(NSA_REP):
            _flash_update(_dot_nt(qs[r], k_j) + mask_j + c31[r], v_j, ms_ref, accs_ref, r)
        return carry

    lax.fori_loop(0, jnp.maximum(i - 1, 0), far, 0)

    k_i, v_i = tile(kw_ref, vw_ref, i)
    for r in range(NSA_REP):
        _flash_init(_dot_nt(qs[r], k_i) + bown_ref[r], v_i, mw_ref, accw_ref, r)

    @pl.when(i >= 1)
    def _():
        k_j, v_j = tile(kw_ref, vw_ref, i - 1)
        for r in range(NSA_REP):
            _flash_update(_dot_nt(qs[r], k_j) + bprev_ref[r], v_j, mw_ref, accw_ref, r)

    @pl.when(i >= 2)
    def _():
        k_j, v_j = tile(kw_ref, vw_ref, i - 2)
        rr = lax.broadcasted_iota(jnp.int32, (TQ, TQ), 0)
        cc = lax.broadcasted_iota(jnp.int32, (TQ, TQ), 1)
        for r in range(NSA_REP):
            s = _dot_nt(qs[r], k_j) + jnp.where(rr < cc, c31[r], NEG)
            _flash_update(s, v_j, mw_ref, accw_ref, r)

    gt = gates_ref[0]
    for r in range(NSA_REP):
        g_cmp = gt[:, N_GATES * r + 0:N_GATES * r + 1]
        g_slc = gt[:, N_GATES * r + 1:N_GATES * r + 2]
        g_win = gt[:, N_GATES * r + 2:N_GATES * r + 3]
        out = g_cmp * ocmp_ref[r] + g_slc * _normalize(accs_ref[r]) + g_win * _normalize(accw_ref[r])
        o_ref[0, :, r * HEAD_PAD:(r + 1) * HEAD_PAD] = out.astype(BF16)


def _nsa(proj, cmp_kv, gates, bias_own, bias_prev, bias_cmp, ovt, c31):
    b, s, _ = proj.shape
    nq = s // TQ
    ncp = s // CMP_STRIDE
    nslc = s // SLC_BLOCK
    qw = NSA_REP * HEAD_PAD
    kv_spec = lambda slot: pl.BlockSpec((1, s, HEAD_PAD), lambda bi, g, i: (bi, 0, slot + g))
    head_blk = MOBA_HEADS // NSA_REP
    return pl.pallas_call(
        functools.partial(_nsa_body, ncp=ncp, nslc=nslc),
        grid=(b, NSA_GROUPS, nq),
        in_specs=[
            pl.BlockSpec(memory_space=pltpu.SMEM),
            pl.BlockSpec((1, TQ, qw), lambda bi, g, i: (bi, i, SLOT_QB // NSA_REP + g)),
            kv_spec(SLOT_KS), kv_spec(SLOT_VS), kv_spec(SLOT_KW), kv_spec(SLOT_VW),
            pl.BlockSpec((1, 1, 1, ncp, HEAD_PAD), lambda bi, g, i: (bi, 0, g, 0, 0)),
            pl.BlockSpec((1, 1, 1, ncp, HEAD_PAD), lambda bi, g, i: (bi, 1, g, 0, 0)),
            pl.BlockSpec((1, TQ, HEAD_PAD), lambda bi, g, i: (bi, i, g)),
            pl.BlockSpec((NSA_REP, TQ, TQ), lambda bi, g, i: (head_blk + g, 0, 0)),
            pl.BlockSpec((NSA_REP, TQ, TQ), lambda bi, g, i: (head_blk + g, 0, 0)),
            pl.BlockSpec((NSA_REP, TQ, ncp), lambda bi, g, i: (head_blk + g, 0, 0)),
            pl.BlockSpec((HEAD_PAD, ncp), lambda bi, g, i: (0, 0)),
        ],
        out_specs=pl.BlockSpec((1, TQ, qw), lambda bi, g, i: (bi, i, g)),
        out_shape=jax.ShapeDtypeStruct((b, s, NSA_HEADS * HEAD_PAD), BF16),
        scratch_shapes=[
            pltpu.VMEM((NSA_REP, TQ, 1), F32),
            pltpu.VMEM((NSA_REP, TQ, HEAD_PAD), F32),
            pltpu.VMEM((NSA_REP, TQ, 1), F32),
            pltpu.VMEM((NSA_REP, TQ, HEAD_PAD), F32),
            pltpu.VMEM((NSA_REP, TQ, HEAD_PAD), F32),
        ],
        compiler_params=pltpu.CompilerParams(
            dimension_semantics=("parallel", "parallel", "arbitrary"), vmem_limit_bytes=VMEM_LIMIT),
        name="nsa",
    )(c31, proj, proj, proj, proj, proj, cmp_kv, cmp_kv, gates, bias_own, bias_prev, bias_cmp, ovt)


def _outproj_body(x_ref, oa_ref, ob_ref, wa_ref, wb_ref, o_ref):
    o_ref[...] = x_ref[...] + _dot(oa_ref[...], wa_ref[...]) + _dot(ob_ref[...], wb_ref[...])


def _outproj(x2d, oa, ob, wa, wb, *, tm=512):
    n = x2d.shape[0]
    ka = oa.shape[1]
    kb = ob.shape[1]
    return pl.pallas_call(
        _outproj_body,
        grid=(n // tm,),
        in_specs=[
            pl.BlockSpec((tm, D_MODEL), lambda i: (i, 0)),
            pl.BlockSpec((tm, ka), lambda i: (i, 0)),
            pl.BlockSpec((tm, kb), lambda i: (i, 0)),
            pl.BlockSpec((ka, D_MODEL), lambda i: (0, 0)),
            pl.BlockSpec((kb, D_MODEL), lambda i: (0, 0)),
        ],
        out_specs=pl.BlockSpec((tm, D_MODEL), lambda i: (i, 0)),
        out_shape=jax.ShapeDtypeStruct((n, D_MODEL), F32),
        compiler_params=pltpu.CompilerParams(
            dimension_semantics=("parallel",), vmem_limit_bytes=VMEM_LIMIT),
        name="outproj",
    )(x2d, oa, ob, wa, wb)


def _rel_bucket(dist):
    n = jnp.maximum(dist, 0)
    nf = jnp.maximum(n, 1).astype(F32)
    large = REL_MAX_EXACT + (jnp.log(nf / REL_MAX_EXACT) / math.log(REL_MAX_DIST / REL_MAX_EXACT)
                             * (REL_BUCKETS - REL_MAX_EXACT)).astype(jnp.int32)
    return jnp.where(n < REL_MAX_EXACT, n, jnp.minimum(large, REL_BUCKETS - 1))


def _bias_body(tab_ref, bk_own_ref, bk_prev_ref, bk_cmp_ref, own_ref, prev_ref, cmp_ref):
    h = pl.program_id(0)
    for bk_ref, out_ref in ((bk_own_ref, own_ref), (bk_prev_ref, prev_ref), (bk_cmp_ref, cmp_ref)):
        bk = bk_ref[...]
        acc = jnp.full(bk.shape, NEG, F32)
        for bucket in range(REL_BUCKETS):
            acc = jnp.where(bk == bucket, tab_ref[h, bucket], acc)
        out_ref[0] = acc


def _bias_tables(rel_bias, ncp):
    tab = rel_bias.T.astype(F32)
    n_heads = tab.shape[0]
    r = jnp.arange(TQ)[:, None]
    c = jnp.arange(TQ)[None, :]
    d_own = r - c
    bk_own = jnp.where(d_own >= 0, _rel_bucket(d_own), -1)
    bk_prev = _rel_bucket(d_own + TQ)
    u = jnp.arange(ncp)[None, :]
    d_cmp = r - CMP_STRIDE * (u - ncp // 2) - (CMP_LEN - 1)
    bk_cmp = jnp.where(d_cmp >= 0, _rel_bucket(d_cmp), -1)
    full = lambda shape: pl.BlockSpec(shape, lambda h: (0,) * len(shape))
    own, prev, cmp_band = pl.pallas_call(
        _bias_body,
        grid=(n_heads,),
        in_specs=[pl.BlockSpec(memory_space=pltpu.SMEM), full((TQ, TQ)), full((TQ, TQ)), full((TQ, ncp))],
        out_specs=[
            pl.BlockSpec((1, TQ, TQ), lambda h: (h, 0, 0)),
            pl.BlockSpec((1, TQ, TQ), lambda h: (h, 0, 0)),
            pl.BlockSpec((1, TQ, ncp), lambda h: (h, 0, 0)),
        ],
        out_shape=[
            jax.ShapeDtypeStruct((n_heads, TQ, TQ), F32),
            jax.ShapeDtypeStruct((n_heads, TQ, TQ), F32),
            jax.ShapeDtypeStruct((n_heads, TQ, ncp), F32),
        ],
        compiler_params=pltpu.CompilerParams(dimension_semantics=("parallel",)),
        name="bias_tiles",
    )(tab, bk_own.astype(jnp.int32), bk_prev.astype(jnp.int32), bk_cmp.astype(jnp.int32))
    return own, prev, cmp_band, tab[:, REL_BUCKETS - 1]


def _pad_heads(w, n_heads):
    rows = w.shape[0]
    w = w.reshape(rows, n_heads, HEAD_DIM)
    w = jnp.pad(w, ((0, 0), (0, 0), (0, HEAD_PAD - HEAD_DIM)))
    return w.reshape(rows, n_heads * HEAD_PAD)


def _slc_overlap_t(ncp, nslc):
    ratio = SLC_BLOCK // CMP_STRIDE
    mc = CMP_LEN // CMP_STRIDE
    w = np.zeros((HEAD_PAD, nslc * ratio + mc), np.float32)
    for j in range(nslc):
        for a in range(ratio):
            for m in range(mc):
                w[j, ratio * j + a + m] += 1.0
    n_cmp = ncp - 1
    out = np.zeros((HEAD_PAD, ncp), np.float32)
    out[:, :n_cmp] = w[:, :n_cmp]
    return out


def kernel(x, norm_ffn1, w_ffn1_gate, w_ffn1_up, w_ffn1_down, norm_mix, w_in, cmp_pos_k, cmp_w1_k, cmp_w2_k,
           cmp_pos_v, cmp_w1_v, cmp_w2_v, w_out, norm_ffn2, w_ffn2_gate, w_ffn2_up, w_ffn2_down, rel_bias,
           norm_final):
    b, s, d = x.shape
    assert d == D_MODEL and s % (2 * TQ) == 0 and s // SLC_BLOCK <= HEAD_PAD and norm_ffn1.shape[0] == 1
    n = b * s
    ncp = s // CMP_STRIDE
    nslc = s // SLC_BLOCK
    scale = HEAD_DIM ** -0.5

    sizes = (512, 512, 512, 512) + (128,) * 6 + (24,)
    splits = np.cumsum(sizes)[:-1].tolist()
    wqa, wka, wva, wqb, wkc, wvc, wks, wvs, wkw, wvw, wgb = jnp.split(w_in[0], splits, axis=1)
    w_pad = jnp.concatenate([
        _pad_heads(wqa * scale, 8), _pad_heads(wka, 8), _pad_heads(wva, 8), _pad_heads(wqb * scale, 8),
        _pad_heads(wkc, 2), _pad_heads(wvc, 2), _pad_heads(wks, 2), _pad_heads(wvs, 2),
        _pad_heads(wkw, 2), _pad_heads(wvw, 2)], axis=1).astype(BF16)
    ones_np = np.zeros((1, N_SLOTS * HEAD_PAD), np.float32)
    for slot in list(range(SLOT_VA, SLOT_VA + 8)) + [SLOT_VS, SLOT_VS + 1, SLOT_VW, SLOT_VW + 1]:
        ones_np[0, slot * HEAD_PAD + ONES_LANE] = 1.0
    ones_row = jnp.asarray(ones_np)
    wgate = wgb.reshape(D_MODEL, NSA_GROUPS, NSA_REP * N_GATES)
    wgate = jnp.pad(wgate, ((0, 0), (0, 0), (0, HEAD_PAD - NSA_REP * N_GATES)))
    wgate = wgate.reshape(D_MODEL, NSA_GROUPS * HEAD_PAD).astype(BF16)

    def cmp_params(pos, w1, w2):
        pos_p = jnp.pad(pos[0], ((0, 0), (0, HEAD_PAD - HEAD_DIM)))
        w1_p = jnp.pad(w1[0].reshape(CMP_LEN, HEAD_DIM, CMP_HIDDEN), ((0, 0), (0, HEAD_PAD - HEAD_DIM), (0, 0)))
        w2_p = jnp.pad(w2[0], ((0, 0), (0, HEAD_PAD - HEAD_DIM)))
        return pos_p, w1_p.astype(BF16), w2_p.astype(BF16)

    pk, w1k, w2k = cmp_params(cmp_pos_k, cmp_w1_k, cmp_w2_k)
    pv, w1v, w2v = cmp_params(cmp_pos_v, cmp_w1_v, cmp_w2_v)
    cmp_pos = jnp.stack([pk, pv])
    cmp_w1 = jnp.stack([w1k, w1v])
    cmp_w2 = jnp.stack([w2k, w2v])

    wo = w_out[0]
    wo_a = _pad_heads(wo[:MOBA_HEADS * HEAD_DIM].T, MOBA_HEADS).T.astype(BF16)
    wo_b = _pad_heads(wo[MOBA_HEADS * HEAD_DIM:].T, NSA_HEADS).T.astype(BF16)

    bias_own, bias_prev, bias_cmp, c31 = _bias_tables(rel_bias, ncp)
    ovt = jnp.asarray(_slc_overlap_t(ncp, nslc)).astype(BF16)

    x2d = x.reshape(n, D_MODEL)
    nf = norm_final.reshape(1, D_MODEL)
    x1 = _ffn(x2d, norm_ffn1, w_ffn1_gate[0].astype(BF16), w_ffn1_up[0].astype(BF16),
              w_ffn1_down[0].astype(BF16), nf, final_norm=False)
    proj, gates = _inproj(x1, norm_mix, w_pad, ones_row, wgate)
    proj = proj.reshape(b, s, N_SLOTS * HEAD_PAD)
    gates = gates.reshape(b, s, NSA_GROUPS * HEAD_PAD)
    cmp_kv = _compress(proj, cmp_pos, cmp_w1, cmp_w2)
    o_a = _moba(proj, bias_own, bias_prev, c31)
    o_b = _nsa(proj, cmp_kv, gates, bias_own, bias_prev, bias_cmp, ovt, c31)
    x2 = _outproj(x1, o_a.reshape(n, -1), o_b.reshape(n, -1), wo_a, wo_b)
    y = _ffn(x2, norm_ffn2, w_ffn2_gate[0].astype(BF16), w_ffn2_up[0].astype(BF16),
             w_ffn2_down[0].astype(BF16), nf, final_norm=True)
    return y.reshape(b, s, D_MODEL)
```

```python
import functools
import math

import numpy as np
import jax
import jax.numpy as jnp
from jax import lax
from jax.experimental import pallas as pl
from jax.experimental.pallas import tpu as pltpu

D_MODEL = 1024
D_FF = 2816
HEAD_DIM = 64
HEAD_PAD = 128
ONES_ROW = HEAD_DIM
EXT0 = HEAD_DIM
MOBA_HEADS = 8
NSA_HEADS = 8
NSA_GROUPS = 2
NSA_REP = NSA_HEADS // NSA_GROUPS
HG = 4
MOBA_BLOCK = 256
MOBA_TOPK = 3
CMP_LEN = 32
CMP_STRIDE = 16
CMP_HIDDEN = 256
SLC_BLOCK = 64
SLC_TOPN = 16
WINDOW = 512
N_GATES = 3
GATE_ROWS = 16
REL_BUCKETS = 32
REL_MAX_EXACT = REL_BUCKETS // 2
REL_MAX_DIST = 128
RMS_EPS = 1e-6
TQ = 256
WIDE = HG * TQ
NEG = -1e30
VMEM_LIMIT = 56 * 1024 * 1024

KSLOT_KA, KSLOT_KC, KSLOT_VC, KSLOT_KS, KSLOT_KW = 0, 8, 10, 12, 14
N_KSLOTS = 16
TSLOT_QA, TSLOT_VA, TSLOT_QB, TSLOT_VS, TSLOT_VW = 0, 8, 16, 24, 26
N_TSLOTS = 28

F32 = jnp.float32
BF16 = jnp.bfloat16


def _dot(a, b):
    return jnp.dot(a, b, preferred_element_type=F32)


def _dot_nt(a, b):
    return lax.dot_general(a, b, (((1,), (1,)), ((), ())), preferred_element_type=F32)


def _split_bf16(x):
    hi = x.astype(BF16)
    lo = (x - hi.astype(F32)).astype(BF16)
    return hi, lo


def _rmsnorm(x, g):
    ms = jnp.mean(x * x, axis=-1, keepdims=True)
    return x * lax.rsqrt(ms + RMS_EPS) * g


def _ffn_body(x_ref, g_ref, wg_ref, wu_ref, wd_ref, gf_ref, o_ref, h_ref, acc_ref, *, final_norm):
    j = pl.program_id(1)

    @pl.when(j == 0)
    def _():
        h_ref[...] = _rmsnorm(x_ref[...], g_ref[...]).astype(BF16)
        acc_ref[...] = jnp.zeros_like(acc_ref)

    h = h_ref[...]
    a = _dot(h, wg_ref[...])
    u = _dot(h, wu_ref[...])
    z = (a * jax.nn.sigmoid(a)) * u
    acc_ref[...] += _dot(z.astype(BF16), wd_ref[...])

    @pl.when(j == pl.num_programs(1) - 1)
    def _():
        y = x_ref[...] + 0.5 * acc_ref[...]
        if final_norm:
            y = _rmsnorm(y, gf_ref[...])
        o_ref[...] = y


def _ffn(x2d, g, wg, wu, wd, gf, *, final_norm, tm=512, tf=1408):
    n = x2d.shape[0]
    grid = (n // tm, D_FF // tf)
    return pl.pallas_call(
        functools.partial(_ffn_body, final_norm=final_norm),
        grid=grid,
        in_specs=[
            pl.BlockSpec((tm, D_MODEL), lambda i, j: (i, 0)),
            pl.BlockSpec((1, D_MODEL), lambda i, j: (0, 0)),
            pl.BlockSpec((D_MODEL, tf), lambda i, j: (0, j)),
            pl.BlockSpec((D_MODEL, tf), lambda i, j: (0, j)),
            pl.BlockSpec((tf, D_MODEL), lambda i, j: (j, 0)),
            pl.BlockSpec((1, D_MODEL), lambda i, j: (0, 0)),
        ],
        out_specs=pl.BlockSpec((tm, D_MODEL), lambda i, j: (i, 0)),
        out_shape=jax.ShapeDtypeStruct((n, D_MODEL), F32),
        scratch_shapes=[pltpu.VMEM((tm, D_MODEL), BF16), pltpu.VMEM((tm, D_MODEL), F32)],
        compiler_params=pltpu.CompilerParams(
            dimension_semantics=("parallel", "arbitrary"), vmem_limit_bytes=VMEM_LIMIT),
        name="ffn_final" if final_norm else "ffn",
    )(x2d, g, wg, wu, wd, gf)


def _inproj_k_body(x_ref, g_ref, w_ref, p_ref, h_ref):
    @pl.when(pl.program_id(1) == 0)
    def _():
        h_ref[...] = _rmsnorm(x_ref[...], g_ref[...]).astype(BF16)

    p_ref[...] = _dot(h_ref[...], w_ref[...]).astype(BF16)


def _inproj_k(x2d, g, w, *, tm=512, tn=512):
    n = x2d.shape[0]
    nc = w.shape[1]
    return pl.pallas_call(
        _inproj_k_body,
        grid=(n // tm, nc // tn),
        in_specs=[
            pl.BlockSpec((tm, D_MODEL), lambda i, j: (i, 0)),
            pl.BlockSpec((1, D_MODEL), lambda i, j: (0, 0)),
            pl.BlockSpec((D_MODEL, tn), lambda i, j: (0, j)),
        ],
        out_specs=pl.BlockSpec((tm, tn), lambda i, j: (i, j)),
        out_shape=jax.ShapeDtypeStruct((n, nc), BF16),
        scratch_shapes=[pltpu.VMEM((tm, D_MODEL), BF16)],
        compiler_params=pltpu.CompilerParams(
            dimension_semantics=("parallel", "arbitrary"), vmem_limit_bytes=VMEM_LIMIT),
        name="inproj_k",
    )(x2d, g, w)


def _inproj_t_body(x_ref, g_ref, wt_ref, ones_ref, wgt_ref, p_ref, gates_ref, h_ref):
    @pl.when(pl.program_id(1) == 0)
    def _():
        h = _rmsnorm(x_ref[...], g_ref[...]).astype(BF16)
        h_ref[...] = h
        gates_ref[0] = jax.nn.sigmoid(_dot_nt(wgt_ref[...], h))

    p_ref[0] = (_dot_nt(wt_ref[...], h_ref[...]) + ones_ref[...]).astype(BF16)


def _inproj_t(x2d, g, wt, ones_col, wgt, b, s, *, tm=512, tn=512):
    n = x2d.shape[0]
    nc = wt.shape[0]
    ng = wgt.shape[0]
    spt = s // tm
    return pl.pallas_call(
        _inproj_t_body,
        grid=(n // tm, nc // tn),
        in_specs=[
            pl.BlockSpec((tm, D_MODEL), lambda i, j: (i, 0)),
            pl.BlockSpec((1, D_MODEL), lambda i, j: (0, 0)),
            pl.BlockSpec((tn, D_MODEL), lambda i, j: (j, 0)),
            pl.BlockSpec((tn, 1), lambda i, j: (j, 0)),
            pl.BlockSpec((ng, D_MODEL), lambda i, j: (0, 0)),
        ],
        out_specs=[
            pl.BlockSpec((1, tn, tm), lambda i, j: (i // spt, j, i % spt)),
            pl.BlockSpec((1, ng, tm), lambda i, j: (i // spt, 0, i % spt)),
        ],
        out_shape=[jax.ShapeDtypeStruct((b, nc, s), BF16), jax.ShapeDtypeStruct((b, ng, s), F32)],
        scratch_shapes=[pltpu.VMEM((tm, D_MODEL), BF16)],
        compiler_params=pltpu.CompilerParams(
            dimension_semantics=("parallel", "arbitrary"), vmem_limit_bytes=VMEM_LIMIT),
        name="inproj_t",
    )(x2d, g, wt, ones_col, wgt)


def _compress_body(kraw_ref, vraw_ref, pos_ref, w1_ref, w2k_ref, w2vt_ref, kc_ref, vct_ref, xs_ref, *, ncp):
    half = CMP_LEN // 2
    for kv, raw_ref in enumerate((kraw_ref, vraw_ref)):
        xs_ref[...] = raw_ref[0].astype(F32)
        y0 = jnp.zeros((ncp, CMP_HIDDEN), F32)
        y1 = jnp.zeros((ncp, CMP_HIDDEN), F32)
        for p in range(half):
            xp = xs_ref[pl.ds(p, ncp, stride=CMP_STRIDE), :]
            y0 = y0 + _dot((xp + pos_ref[kv, p:p + 1, :]).astype(BF16), w1_ref[kv, p])
            y1 = y1 + _dot((xp + pos_ref[kv, half + p:half + p + 1, :]).astype(BF16), w1_ref[kv, half + p])
        h1 = y0 + pltpu.roll(y1, ncp - 1, 0)
        act = (h1 * jax.nn.sigmoid(h1)).astype(BF16)
        if kv == 0:
            kc_ref[0, 0] = _dot(act, w2k_ref[...]).astype(BF16)
        else:
            out_t = _dot_nt(w2vt_ref[...], act)
            row = lax.broadcasted_iota(jnp.int32, out_t.shape, 0)
            vct_ref[0, 0] = (out_t + jnp.where(row == ONES_ROW, 1.0, 0.0)).astype(BF16)


def _compress(pk, pos, w1, w2k, w2vt):
    b, s, _ = pk.shape
    ncp = s // CMP_STRIDE
    full = lambda shape: pl.BlockSpec(shape, lambda bi, g: (0,) * len(shape))
    return pl.pallas_call(
        functools.partial(_compress_body, ncp=ncp),
        grid=(b, NSA_GROUPS),
        in_specs=[
            pl.BlockSpec((1, s, HEAD_PAD), lambda bi, g: (bi, 0, KSLOT_KC + g)),
            pl.BlockSpec((1, s, HEAD_PAD), lambda bi, g: (bi, 0, KSLOT_VC + g)),
            full((2, CMP_LEN, HEAD_PAD)),
            full((2, CMP_LEN, HEAD_PAD, CMP_HIDDEN)),
            full((CMP_HIDDEN, HEAD_PAD)),
            full((HEAD_PAD, CMP_HIDDEN)),
        ],
        out_specs=[
            pl.BlockSpec((1, 1, ncp, HEAD_PAD), lambda bi, g: (bi, g, 0, 0)),
            pl.BlockSpec((1, 1, HEAD_PAD, ncp), lambda bi, g: (bi, g, 0, 0)),
        ],
        out_shape=[
            jax.ShapeDtypeStruct((b, NSA_GROUPS, ncp, HEAD_PAD), BF16),
            jax.ShapeDtypeStruct((b, NSA_GROUPS, HEAD_PAD, ncp), BF16),
        ],
        scratch_shapes=[pltpu.VMEM((s, HEAD_PAD), F32)],
        compiler_params=pltpu.CompilerParams(
            dimension_semantics=("parallel", "arbitrary"), vmem_limit_bytes=VMEM_LIMIT),
        name="compress",
    )(pk, pk, pos, w1, w2k, w2vt)


def _flash_init(s_t, pv_fn, m_ref, acc_ref):
    m = jnp.max(s_t, axis=0, keepdims=True)
    p = jnp.exp(s_t - m).astype(BF16)
    m_ref[...] = m
    acc_ref[...] = pv_fn(p)


def _flash_update(s_t, shift, pv_fn, m_ref, acc_ref):
    m_prev = m_ref[...]
    m_new = jnp.maximum(m_prev, jnp.max(s_t, axis=0, keepdims=True) + shift)
    alpha = jnp.exp(m_prev - m_new)
    p = jnp.exp(s_t - (m_new - shift)).astype(BF16)
    acc_ref[...] = alpha * acc_ref[...] + pv_fn(p)
    m_ref[...] = m_new


def _normalize(acc):
    return acc / acc[ONES_ROW:ONES_ROW + 1, :]


def _lane_block(x, r):
    return x[:, r * TQ:(r + 1) * TQ]


def _shift_row(c31_ref, head0):
    return jnp.concatenate([jnp.full((1, TQ), c31_ref[head0 + r], F32) for r in range(HG)], axis=1)


def _rank_select(score, n_rows, top, row_weight=None):
    blk = lax.broadcasted_iota(jnp.int32, score.shape, 0)
    cnt = jnp.zeros(score.shape, F32)
    for j in range(n_rows):
        row = score[j:j + 1, :]
        ahead = jnp.where(row > score, 1.0, jnp.where((row == score) & (blk > j), 1.0, 0.0))
        cnt = cnt + (ahead if row_weight is None else ahead * row_weight(j))
    return cnt < top


def _moba_body(c31_ref, q_ref, k_ref, vt_ref, pat_ref, bown_ref, bprev_ref, o_ref,
               kmean_ref, kx_ref, qx_ref, m_ref, acc_ref, *, nb):
    hg = pl.program_id(1)
    i = pl.program_id(2)
    nbp = -(-nb // 16) * 16

    @pl.when(i == 0)
    def _():
        kmean_ref[...] = jnp.zeros_like(kmean_ref)
        for r in range(HG):
            kr = k_ref[0, :, r * HEAD_PAD:(r + 1) * HEAD_PAD]
            kx_ref[:, r * HEAD_PAD:(r + 1) * HEAD_PAD] = kr + pat_ref[...]
            for n in range(nb):
                kb = kr[n * MOBA_BLOCK:(n + 1) * MOBA_BLOCK, :].astype(F32)
                kmean_ref[r, n:n + 1, :] = jnp.mean(kb, axis=0, keepdims=True)

    for r in range(HG):
        q_t = q_ref[0, r * HEAD_PAD:(r + 1) * HEAD_PAD, :]
        kh, kl = _split_bf16(kmean_ref[r])
        gate = (_dot(kh, q_t) + _dot(kl, q_t))[:nbp]
        blk = lax.broadcasted_iota(jnp.int32, gate.shape, 0)
        top = _rank_select(gate, nb, MOBA_TOPK, lambda n: jnp.where(n < i, 1.0, 0.0))
        sel = (blk < i) & top
        mask_rows = jnp.where(sel, 0.0, NEG).astype(BF16)
        pad_rows = jnp.zeros((HEAD_PAD - EXT0 - nbp, TQ), BF16)
        qx_ref[:, r * TQ:(r + 1) * TQ] = jnp.concatenate([q_t[:EXT0], mask_rows, pad_rows], axis=0)

    def scores(kref, j):
        off = pl.multiple_of(j * TQ, TQ)
        return jnp.concatenate(
            [_dot(kref[pl.ds(off, TQ), r * HEAD_PAD:(r + 1) * HEAD_PAD], qx_ref[:, r * TQ:(r + 1) * TQ])
             for r in range(HG)], axis=1)

    def pv_fn(j):
        off = pl.multiple_of(j * TQ, TQ)

        def fn(p):
            return jnp.concatenate(
                [_dot(vt_ref[0, r * HEAD_PAD:(r + 1) * HEAD_PAD, pl.ds(off, TQ)], _lane_block(p, r))
                 for r in range(HG)], axis=1)
        return fn

    k_plain = k_ref.at[0]
    zero = jnp.zeros((1, WIDE), F32)
    shift_far = _shift_row(c31_ref, hg * HG)

    _flash_init(scores(k_plain, i) + bown_ref[0], pv_fn(i), m_ref, acc_ref)

    @pl.when(i >= 1)
    def _():
        _flash_update(scores(kx_ref, i - 1) + bprev_ref[0], zero, pv_fn(i - 1), m_ref, acc_ref)

    def far(j, carry):
        _flash_update(scores(kx_ref, j), shift_far, pv_fn(j), m_ref, acc_ref)
        return carry

    lax.fori_loop(0, jnp.maximum(i - 1, 0), far, 0)

    out = _normalize(acc_ref[...])
    for r in range(HG):
        o_ref[0, :, r * HEAD_PAD:(r + 1) * HEAD_PAD] = jnp.transpose(_lane_block(out, r)).astype(BF16)


def _moba(pk, pt, pat, bias_own, bias_prev, c31):
    b, s, _ = pk.shape
    nb = s // MOBA_BLOCK
    hw = HG * HEAD_PAD
    return pl.pallas_call(
        functools.partial(_moba_body, nb=nb),
        grid=(b, MOBA_HEADS // HG, nb),
        in_specs=[
            pl.BlockSpec(memory_space=pltpu.SMEM),
            pl.BlockSpec((1, hw, TQ), lambda bi, hg, i: (bi, TSLOT_QA // HG + hg, i)),
            pl.BlockSpec((1, s, hw), lambda bi, hg, i: (bi, 0, KSLOT_KA // HG + hg)),
            pl.BlockSpec((1, hw, s), lambda bi, hg, i: (bi, TSLOT_VA // HG + hg, 0)),
            pl.BlockSpec((s, HEAD_PAD), lambda bi, hg, i: (0, 0)),
            pl.BlockSpec((1, TQ, WIDE), lambda bi, hg, i: (hg, 0, 0)),
            pl.BlockSpec((1, TQ, WIDE), lambda bi, hg, i: (hg, 0, 0)),
        ],
        out_specs=pl.BlockSpec((1, TQ, hw), lambda bi, hg, i: (bi, i, hg)),
        out_shape=jax.ShapeDtypeStruct((b, s, MOBA_HEADS * HEAD_PAD), BF16),
        scratch_shapes=[
            pltpu.VMEM((HG, HEAD_PAD, HEAD_PAD), F32),
            pltpu.VMEM((s, hw), BF16),
            pltpu.VMEM((HEAD_PAD, WIDE), BF16),
            pltpu.VMEM((1, WIDE), F32),
            pltpu.VMEM((HEAD_PAD, WIDE), F32),
        ],
        compiler_params=pltpu.CompilerParams(
            dimension_semantics=("parallel", "parallel", "arbitrary"), vmem_limit_bytes=VMEM_LIMIT),
        name="moba",
    )(c31, pt, pk, pt, pat, bias_own, bias_prev)


def _nsa_body(c31_ref, q_ref, ks_ref, kw_ref, vst_ref, vwt_ref, kc_ref, vct_ref, gates_ref, pat_ref,
              bown_ref, bprev_ref, bwin2_ref, bcmp_ref, ovt_ref, o_ref,
              ksx_ref, qx_ref, ms_ref, accs_ref, mw_ref, accw_ref, *, ncp, nslc, nq):
    g = pl.program_id(1)
    i = pl.program_id(2)

    @pl.when(i == 0)
    def _():
        ksx_ref[...] = ks_ref[0] + pat_ref[...]

    q4 = jnp.concatenate([q_ref[0, r * HEAD_PAD:(r + 1) * HEAD_PAD, :] for r in range(HG)], axis=1)

    band_off = pl.multiple_of((nq - 1 - i) * (TQ // CMP_STRIDE), TQ // CMP_STRIDE)
    s_t = _dot(kc_ref[0, 0], q4) + bcmp_ref[0, pl.ds(band_off, ncp), :]
    m = jnp.max(s_t, axis=0, keepdims=True)
    p = jnp.where(s_t > 0.5 * NEG, jnp.exp(s_t - m), 0.0)
    den = jnp.sum(p, axis=0, keepdims=True)
    pn = p / jnp.where(den > 0.0, den, 1.0)
    o_cmp = _dot(vct_ref[0, 0], pn.astype(BF16))
    psum = _lane_block(pn, 0)
    for r in range(1, HG):
        psum = psum + _lane_block(pn, r)

    ph, plo = _split_bf16(psum)
    imp = (_dot(ovt_ref[...], ph) + _dot(ovt_ref[...], plo))[:EXT0]
    blk = lax.broadcasted_iota(jnp.int32, imp.shape, 0)
    cur = (i * TQ + lax.broadcasted_iota(jnp.int32, imp.shape, 1)) // SLC_BLOCK
    forced = (blk == 0) | (blk == cur) | (blk == cur - 1)
    score = jnp.where(forced, 1e30, jnp.where(blk <= cur, imp, -1.0))
    score = jnp.where(blk < nslc, score, -2.0)
    sel = _rank_select(score, nslc, SLC_TOPN) & (blk < nslc)
    mask_rows = jnp.where(sel, 0.0, NEG).astype(BF16)
    qx_ref[...] = jnp.concatenate([q4[:EXT0], jnp.concatenate([mask_rows] * HG, axis=1)], axis=0)

    def key_tile(kref, j):
        return kref[pl.ds(pl.multiple_of(j * TQ, TQ), TQ), :]

    def pv_fn(vref, j):
        off = pl.multiple_of(j * TQ, TQ)
        return lambda p: _dot(vref[0, :, pl.ds(off, TQ)], p)

    zero = jnp.zeros((1, WIDE), F32)
    shift_far = _shift_row(c31_ref, MOBA_HEADS + g * HG)

    _flash_init(_dot(key_tile(ksx_ref, i), qx_ref[...]) + bown_ref[0], pv_fn(vst_ref, i), ms_ref, accs_ref)

    @pl.when(i >= 1)
    def _():
        s_prev = _dot(key_tile(ksx_ref, i - 1), qx_ref[...]) + bprev_ref[0]
        _flash_update(s_prev, zero, pv_fn(vst_ref, i - 1), ms_ref, accs_ref)

    def far(j, carry):
        _flash_update(_dot(key_tile(ksx_ref, j), qx_ref[...]), shift_far, pv_fn(vst_ref, j), ms_ref, accs_ref)
        return carry

    lax.fori_loop(0, jnp.maximum(i - 1, 0), far, 0)

    kw = kw_ref.at[0]
    _flash_init(_dot(key_tile(kw, i), q4) + bown_ref[0], pv_fn(vwt_ref, i), mw_ref, accw_ref)

    @pl.when(i >= 1)
    def _():
        _flash_update(_dot(key_tile(kw, i - 1), q4) + bprev_ref[0], zero, pv_fn(vwt_ref, i - 1), mw_ref, accw_ref)

    @pl.when(i >= 2)
    def _():
        _flash_update(_dot(key_tile(kw, i - 2), q4) + bwin2_ref[0], zero, pv_fn(vwt_ref, i - 2), mw_ref, accw_ref)

    o_slc = _normalize(accs_ref[...])
    o_win = _normalize(accw_ref[...])
    gt = gates_ref[0]
    for r in range(HG):
        row = N_GATES * r
        out = (gt[row:row + 1, :] * _lane_block(o_cmp, r) + gt[row + 1:row + 2, :] * _lane_block(o_slc, r)
               + gt[row + 2:row + 3, :] * _lane_block(o_win, r))
        o_ref[0, :, r * HEAD_PAD:(r + 1) * HEAD_PAD] = jnp.transpose(out).astype(BF16)


def _nsa(pk, pt, kc, vct, gates_t, pat, bias_own, bias_prev, bias_win2, bias_cmp, ovt, c31):
    b, s, _ = pk.shape
    nq = s // TQ
    ncp = s // CMP_STRIDE
    nslc = s // SLC_BLOCK
    hw = HG * HEAD_PAD
    band_rows = bias_cmp.shape[1]
    hgrp = MOBA_HEADS // HG
    k_spec = lambda slot: pl.BlockSpec((1, s, HEAD_PAD), lambda bi, g, i: (bi, 0, slot + g))
    vt_spec = lambda slot: pl.BlockSpec((1, HEAD_PAD, s), lambda bi, g, i: (bi, slot + g, 0))
    bias_spec = lambda rows: pl.BlockSpec((1, rows, WIDE), lambda bi, g, i: (hgrp + g, 0, 0))
    return pl.pallas_call(
        functools.partial(_nsa_body, ncp=ncp, nslc=nslc, nq=nq),
        grid=(b, NSA_GROUPS, nq),
        in_specs=[
            pl.BlockSpec(memory_space=pltpu.SMEM),
            pl.BlockSpec((1, hw, TQ), lambda bi, g, i: (bi, TSLOT_QB // HG + g, i)),
            k_spec(KSLOT_KS), k_spec(KSLOT_KW), vt_spec(TSLOT_VS), vt_spec(TSLOT_VW),
            pl.BlockSpec((1, 1, ncp, HEAD_PAD), lambda bi, g, i: (bi, g, 0, 0)),
            pl.BlockSpec((1, 1, HEAD_PAD, ncp), lambda bi, g, i: (bi, g, 0, 0)),
            pl.BlockSpec((1, GATE_ROWS, TQ), lambda bi, g, i: (bi, g, i)),
            pl.BlockSpec((s, HEAD_PAD), lambda bi, g, i: (0, 0)),
            bias_spec(TQ), bias_spec(TQ), bias_spec(TQ), bias_spec(band_rows),
            pl.BlockSpec((HEAD_PAD, ncp), lambda bi, g, i: (0, 0)),
        ],
        out_specs=pl.BlockSpec((1, TQ, hw), lambda bi, g, i: (bi, i, g)),
        out_shape=jax.ShapeDtypeStruct((b, s, NSA_HEADS * HEAD_PAD), BF16),
        scratch_shapes=[
            pltpu.VMEM((s, HEAD_PAD), BF16),
            pltpu.VMEM((HEAD_PAD, WIDE), BF16),
            pltpu.VMEM((1, WIDE), F32),
            pltpu.VMEM((HEAD_PAD, WIDE), F32),
            pltpu.VMEM((1, WIDE), F32),
            pltpu.VMEM((HEAD_PAD, WIDE), F32),
        ],
        compiler_params=pltpu.CompilerParams(
            dimension_semantics=("parallel", "parallel", "arbitrary"), vmem_limit_bytes=VMEM_LIMIT),
        name="nsa",
    )(c31, pt, pk, pk, pt, pt, kc, vct, gates_t, pat, bias_own, bias_prev, bias_win2, bias_cmp, ovt)


def _outproj_body(x_ref, oa_ref, ob_ref, wa_ref, wb_ref, o_ref):
    o_ref[...] = x_ref[...] + _dot(oa_ref[...], wa_ref[...]) + _dot(ob_ref[...], wb_ref[...])


def _outproj(x2d, oa, ob, wa, wb, *, tm=512):
    n = x2d.shape[0]
    ka = oa.shape[1]
    kb = ob.shape[1]
    return pl.pallas_call(
        _outproj_body,
        grid=(n // tm,),
        in_specs=[
            pl.BlockSpec((tm, D_MODEL), lambda i: (i, 0)),
            pl.BlockSpec((tm, ka), lambda i: (i, 0)),
            pl.BlockSpec((tm, kb), lambda i: (i, 0)),
            pl.BlockSpec((ka, D_MODEL), lambda i: (0, 0)),
            pl.BlockSpec((kb, D_MODEL), lambda i: (0, 0)),
        ],
        out_specs=pl.BlockSpec((tm, D_MODEL), lambda i: (i, 0)),
        out_shape=jax.ShapeDtypeStruct((n, D_MODEL), F32),
        compiler_params=pltpu.CompilerParams(
            dimension_semantics=("parallel",), vmem_limit_bytes=VMEM_LIMIT),
        name="outproj",
    )(x2d, oa, ob, wa, wb)


def _rel_bucket(dist):
    n = jnp.maximum(dist, 0)
    nf = jnp.maximum(n, 1).astype(F32)
    large = REL_MAX_EXACT + (jnp.log(nf / REL_MAX_EXACT) / math.log(REL_MAX_DIST / REL_MAX_EXACT)
                             * (REL_BUCKETS - REL_MAX_EXACT)).astype(jnp.int32)
    return jnp.where(n < REL_MAX_EXACT, n, jnp.minimum(large, REL_BUCKETS - 1))


def _bias_body(tab_ref, *refs):
    h = pl.program_id(0)
    n_tab = len(refs) // 2
    for bk_ref, out_ref in zip(refs[:n_tab], refs[n_tab:]):
        bk = bk_ref[...]
        acc = jnp.full(bk.shape, NEG, F32)
        for bucket in range(REL_BUCKETS):
            acc = jnp.where(bk == bucket, tab_ref[h, bucket], acc)
        out_ref[0] = acc


def _bias_tables(rel_bias, ncp, nq):
    tab = rel_bias.T.astype(F32)
    n_heads = tab.shape[0]
    c = jnp.arange(TQ)[:, None]
    r = jnp.arange(TQ)[None, :]
    d_own = r - c
    bk_own = jnp.where(d_own >= 0, _rel_bucket(d_own), -1)
    bk_prev = _rel_bucket(d_own + TQ)
    bk_win2 = jnp.where(d_own + 2 * TQ < WINDOW, _rel_bucket(d_own + 2 * TQ), -1)
    band_rows = ncp + (nq - 1) * (TQ // CMP_STRIDE)
    u = jnp.arange(band_rows)[:, None] - (nq - 1) * (TQ // CMP_STRIDE)
    d_cmp = r - CMP_STRIDE * u - (CMP_LEN - 1)
    bk_cmp = jnp.where(d_cmp >= 0, _rel_bucket(d_cmp), -1)
    tables = [t.astype(jnp.int32) for t in (bk_own, bk_prev, bk_win2, bk_cmp)]
    outs = pl.pallas_call(
        _bias_body,
        grid=(n_heads,),
        in_specs=[pl.BlockSpec(memory_space=pltpu.SMEM)]
        + [pl.BlockSpec(t.shape, lambda h: (0, 0)) for t in tables],
        out_specs=[pl.BlockSpec((1, t.shape[0], TQ), lambda h: (h // HG, 0, h % HG)) for t in tables],
        out_shape=[jax.ShapeDtypeStruct((n_heads // HG, t.shape[0], WIDE), F32) for t in tables],
        compiler_params=pltpu.CompilerParams(dimension_semantics=("parallel",), vmem_limit_bytes=VMEM_LIMIT),
        name="bias_tiles",
    )(tab, *tables)
    return outs, tab[:, REL_BUCKETS - 1]


def _pad_heads(w, n_heads):
    rows = w.shape[0]
    w = w.reshape(rows, n_heads, HEAD_DIM)
    w = jnp.pad(w, ((0, 0), (0, 0), (0, HEAD_PAD - HEAD_DIM)))
    return w.reshape(rows, n_heads * HEAD_PAD)


def _slc_overlap_t(ncp, nslc):
    ratio = SLC_BLOCK // CMP_STRIDE
    mc = CMP_LEN // CMP_STRIDE
    w = np.zeros((HEAD_PAD, nslc * ratio + mc), np.float32)
    for j in range(nslc):
        for a in range(ratio):
            for m in range(mc):
                w[j, ratio * j + a + m] += 1.0
    n_cmp = ncp - 1
    out = np.zeros((HEAD_PAD, ncp), np.float32)
    out[:, :n_cmp] = w[:, :n_cmp]
    return out


def _block_onehot(s, block):
    pat = np.zeros((s, HEAD_PAD), np.float32)
    pat[np.arange(s), EXT0 + np.arange(s) // block] = 1.0
    return pat


def kernel(x, norm_ffn1, w_ffn1_gate, w_ffn1_up, w_ffn1_down, norm_mix, w_in, cmp_pos_k, cmp_w1_k, cmp_w2_k,
           cmp_pos_v, cmp_w1_v, cmp_w2_v, w_out, norm_ffn2, w_ffn2_gate, w_ffn2_up, w_ffn2_down, rel_bias,
           norm_final):
    b, s, d = x.shape
    assert d == D_MODEL and s % (2 * TQ) == 0 and norm_ffn1.shape[0] == 1
    assert s // SLC_BLOCK <= HEAD_PAD - EXT0, "selection-mask rows must fit beside the 64 head dims"
    n = b * s
    nq = s // TQ
    ncp = s // CMP_STRIDE
    nslc = s // SLC_BLOCK
    scale = HEAD_DIM ** -0.5

    sizes = (512, 512, 512, 512) + (128,) * 6 + (24,)
    splits = np.cumsum(sizes)[:-1].tolist()
    wqa, wka, wva, wqb, wkc, wvc, wks, wvs, wkw, wvw, wgb = jnp.split(w_in[0], splits, axis=1)
    w_k = jnp.concatenate([_pad_heads(wka, 8), _pad_heads(wkc, 2), _pad_heads(wvc, 2), _pad_heads(wks, 2),
                           _pad_heads(wkw, 2)], axis=1).astype(BF16)
    w_t = jnp.concatenate([_pad_heads(wqa * scale, 8), _pad_heads(wva, 8), _pad_heads(wqb * scale, 8),
                           _pad_heads(wvs, 2), _pad_heads(wvw, 2)], axis=1).T.astype(BF16)
    ones_np = np.zeros((N_TSLOTS * HEAD_PAD, 1), np.float32)
    for slot in list(range(TSLOT_VA, TSLOT_VA + 8)) + [TSLOT_VS, TSLOT_VS + 1, TSLOT_VW, TSLOT_VW + 1]:
        ones_np[slot * HEAD_PAD + ONES_ROW, 0] = 1.0
    ones_col = jnp.asarray(ones_np)
    wgt = wgb.reshape(D_MODEL, NSA_GROUPS, NSA_REP * N_GATES)
    wgt = jnp.pad(wgt, ((0, 0), (0, 0), (0, GATE_ROWS - NSA_REP * N_GATES)))
    wgt = wgt.reshape(D_MODEL, NSA_GROUPS * GATE_ROWS).T.astype(BF16)

    def cmp_params(pos, w1, w2):
        pos_p = jnp.pad(pos[0], ((0, 0), (0, HEAD_PAD - HEAD_DIM)))
        w1_p = jnp.pad(w1[0].reshape(CMP_LEN, HEAD_DIM, CMP_HIDDEN), ((0, 0), (0, HEAD_PAD - HEAD_DIM), (0, 0)))
        w2_p = jnp.pad(w2[0], ((0, 0), (0, HEAD_PAD - HEAD_DIM)))
        return pos_p, w1_p.astype(BF16), w2_p.astype(BF16)

    pos_k, w1k, w2k = cmp_params(cmp_pos_k, cmp_w1_k, cmp_w2_k)
    pos_v, w1v, w2v = cmp_params(cmp_pos_v, cmp_w1_v, cmp_w2_v)
    cmp_pos = jnp.stack([pos_k, pos_v])
    cmp_w1 = jnp.stack([w1k, w1v])

    wo = w_out[0]
    wo_a = _pad_heads(wo[:MOBA_HEADS * HEAD_DIM].T, MOBA_HEADS).T.astype(BF16)
    wo_b = _pad_heads(wo[MOBA_HEADS * HEAD_DIM:].T, NSA_HEADS).T.astype(BF16)

    (bias_own, bias_prev, bias_win2, bias_cmp), c31 = _bias_tables(rel_bias, ncp, nq)
    ovt = jnp.asarray(_slc_overlap_t(ncp, nslc)).astype(BF16)
    pat_moba = jnp.asarray(_block_onehot(s, MOBA_BLOCK)).astype(BF16)
    pat_slc = jnp.asarray(_block_onehot(s, SLC_BLOCK)).astype(BF16)

    x2d = x.reshape(n, D_MODEL)
    nf = norm_final.reshape(1, D_MODEL)
    x1 = _ffn(x2d, norm_ffn1, w_ffn1_gate[0].astype(BF16), w_ffn1_up[0].astype(BF16),
              w_ffn1_down[0].astype(BF16), nf, final_norm=False)
    pk = _inproj_k(x1, norm_mix, w_k).reshape(b, s, N_KSLOTS * HEAD_PAD)
    pt, gates_t = _inproj_t(x1, norm_mix, w_t, ones_col, wgt, b, s)
    kc, vct = _compress(pk, cmp_pos, cmp_w1, w2k, w2v.T)
    o_a = _moba(pk, pt, pat_moba, bias_own, bias_prev, c31)
    o_b = _nsa(pk, pt, kc, vct, gates_t, pat_slc, bias_own, bias_prev, bias_win2, bias_cmp, ovt, c31)
    x2 = _outproj(x1, o_a.reshape(n, -1), o_b.reshape(n, -1), wo_a, wo_b)
    y = _ffn(x2, norm_ffn2, w_ffn2_gate[0].astype(BF16), w_ffn2_up[0].astype(BF16),
             w_ffn2_down[0].astype(BF16), nf, final_norm=True)
    return y.reshape(b, s, D_MODEL)
```

```python
import functools
import math

import numpy as np
import jax
import jax.numpy as jnp
from jax import lax
from jax.experimental import pallas as pl
from jax.experimental.pallas import tpu as pltpu

D_MODEL = 1024
D_FF = 2816
HEAD_DIM = 64
HEAD_PAD = 128
ONES_ROW = HEAD_DIM
EXT0 = HEAD_DIM
MOBA_HEADS = 8
NSA_HEADS = 8
NSA_GROUPS = 2
NSA_REP = NSA_HEADS // NSA_GROUPS
HG = 4
MOBA_BLOCK = 256
MOBA_TOPK = 3
CMP_LEN = 32
CMP_STRIDE = 16
CMP_HIDDEN = 256
SLC_BLOCK = 64
SLC_TOPN = 16
WINDOW = 512
N_GATES = 3
GATE_ROWS = 16
REL_BUCKETS = 32
REL_MAX_EXACT = REL_BUCKETS // 2
REL_MAX_DIST = 128
RMS_EPS = 1e-6
TQ = 256
WIDE = HG * TQ
NEG = -1e30
M_INIT = -3e38
ACC_ROWS = 80
VMEM_LIMIT = 56 * 1024 * 1024

KSLOT_KA, KSLOT_KC, KSLOT_VC, KSLOT_KS, KSLOT_KW = 0, 8, 10, 12, 14
N_KSLOTS = 16
TSLOT_QA, TSLOT_VA, TSLOT_QB, TSLOT_VS, TSLOT_VW = 0, 8, 16, 24, 26
N_TSLOTS = 28

F32 = jnp.float32
BF16 = jnp.bfloat16


def _dot(a, b):
    return jnp.dot(a, b, preferred_element_type=F32)


def _dot_nt(a, b):
    return lax.dot_general(a, b, (((1,), (1,)), ((), ())), preferred_element_type=F32)


def _split_bf16(x):
    hi = x.astype(BF16)
    lo = (x - hi.astype(F32)).astype(BF16)
    return hi, lo


def _rmsnorm(x, g):
    ms = jnp.mean(x * x, axis=-1, keepdims=True)
    return x * lax.rsqrt(ms + RMS_EPS) * g


def _ffn_body(x_ref, g_ref, wg_ref, wu_ref, wd_ref, gf_ref, o_ref, h_ref, acc_ref, *, final_norm):
    j = pl.program_id(1)

    @pl.when(j == 0)
    def _():
        h_ref[...] = _rmsnorm(x_ref[...], g_ref[...]).astype(BF16)
        acc_ref[...] = jnp.zeros_like(acc_ref)

    h = h_ref[...]
    a = _dot(h, wg_ref[...])
    u = _dot(h, wu_ref[...])
    z = (a * jax.nn.sigmoid(a)) * u
    acc_ref[...] += _dot(z.astype(BF16), wd_ref[...])

    @pl.when(j == pl.num_programs(1) - 1)
    def _():
        y = x_ref[...] + 0.5 * acc_ref[...]
        if final_norm:
            y = _rmsnorm(y, gf_ref[...])
        o_ref[...] = y


def _ffn(x2d, g, wg, wu, wd, gf, *, final_norm, tm=512, tf=1408):
    n = x2d.shape[0]
    grid = (n // tm, D_FF // tf)
    return pl.pallas_call(
        functools.partial(_ffn_body, final_norm=final_norm),
        grid=grid,
        in_specs=[
            pl.BlockSpec((tm, D_MODEL), lambda i, j: (i, 0)),
            pl.BlockSpec((1, D_MODEL), lambda i, j: (0, 0)),
            pl.BlockSpec((D_MODEL, tf), lambda i, j: (0, j)),
            pl.BlockSpec((D_MODEL, tf), lambda i, j: (0, j)),
            pl.BlockSpec((tf, D_MODEL), lambda i, j: (j, 0)),
            pl.BlockSpec((1, D_MODEL), lambda i, j: (0, 0)),
        ],
        out_specs=pl.BlockSpec((tm, D_MODEL), lambda i, j: (i, 0)),
        out_shape=jax.ShapeDtypeStruct((n, D_MODEL), F32),
        scratch_shapes=[pltpu.VMEM((tm, D_MODEL), BF16), pltpu.VMEM((tm, D_MODEL), F32)],
        compiler_params=pltpu.CompilerParams(
            dimension_semantics=("parallel", "arbitrary"), vmem_limit_bytes=VMEM_LIMIT),
        name="ffn_final" if final_norm else "ffn",
    )(x2d, g, wg, wu, wd, gf)


def _inproj_k_body(x_ref, g_ref, w_ref, p_ref, h_ref):
    @pl.when(pl.program_id(1) == 0)
    def _():
        h_ref[...] = _rmsnorm(x_ref[...], g_ref[...]).astype(BF16)

    p_ref[...] = _dot(h_ref[...], w_ref[...]).astype(BF16)


def _inproj_k(x2d, g, w, *, tm=512, tn=512):
    n = x2d.shape[0]
    nc = w.shape[1]
    return pl.pallas_call(
        _inproj_k_body,
        grid=(n // tm, nc // tn),
        in_specs=[
            pl.BlockSpec((tm, D_MODEL), lambda i, j: (i, 0)),
            pl.BlockSpec((1, D_MODEL), lambda i, j: (0, 0)),
            pl.BlockSpec((D_MODEL, tn), lambda i, j: (0, j)),
        ],
        out_specs=pl.BlockSpec((tm, tn), lambda i, j: (i, j)),
        out_shape=jax.ShapeDtypeStruct((n, nc), BF16),
        scratch_shapes=[pltpu.VMEM((tm, D_MODEL), BF16)],
        compiler_params=pltpu.CompilerParams(
            dimension_semantics=("parallel", "arbitrary"), vmem_limit_bytes=VMEM_LIMIT),
        name="inproj_k",
    )(x2d, g, w)


def _inproj_t_body(x_ref, g_ref, wt_ref, ones_ref, wgt_ref, p_ref, gates_ref, h_ref):
    @pl.when(pl.program_id(1) == 0)
    def _():
        h = _rmsnorm(x_ref[...], g_ref[...]).astype(BF16)
        h_ref[...] = h
        gates_ref[0] = jax.nn.sigmoid(_dot_nt(wgt_ref[...], h))

    p_ref[0] = (_dot_nt(wt_ref[...], h_ref[...]) + ones_ref[...]).astype(BF16)


def _inproj_t(x2d, g, wt, ones_col, wgt, b, s, *, tm=512, tn=512):
    n = x2d.shape[0]
    nc = wt.shape[0]
    ng = wgt.shape[0]
    spt = s // tm
    return pl.pallas_call(
        _inproj_t_body,
        grid=(n // tm, nc // tn),
        in_specs=[
            pl.BlockSpec((tm, D_MODEL), lambda i, j: (i, 0)),
            pl.BlockSpec((1, D_MODEL), lambda i, j: (0, 0)),
            pl.BlockSpec((tn, D_MODEL), lambda i, j: (j, 0)),
            pl.BlockSpec((tn, 1), lambda i, j: (j, 0)),
            pl.BlockSpec((ng, D_MODEL), lambda i, j: (0, 0)),
        ],
        out_specs=[
            pl.BlockSpec((1, tn, tm), lambda i, j: (i // spt, j, i % spt)),
            pl.BlockSpec((1, ng, tm), lambda i, j: (i // spt, 0, i % spt)),
        ],
        out_shape=[jax.ShapeDtypeStruct((b, nc, s), BF16), jax.ShapeDtypeStruct((b, ng, s), F32)],
        scratch_shapes=[pltpu.VMEM((tm, D_MODEL), BF16)],
        compiler_params=pltpu.CompilerParams(
            dimension_semantics=("parallel", "arbitrary"), vmem_limit_bytes=VMEM_LIMIT),
        name="inproj_t",
    )(x2d, g, wt, ones_col, wgt)


def _compress_body(kraw_ref, vraw_ref, pos_ref, w1_ref, w2k_ref, w2vt_ref, kc_ref, vct_ref, xs_ref, *, ncp):
    half = CMP_LEN // 2
    for kv, raw_ref in enumerate((kraw_ref, vraw_ref)):
        xs_ref[...] = raw_ref[0].astype(F32)
        y0 = jnp.zeros((ncp, CMP_HIDDEN), F32)
        y1 = jnp.zeros((ncp, CMP_HIDDEN), F32)
        for p in range(half):
            xp = xs_ref[pl.ds(p, ncp, stride=CMP_STRIDE), :]
            y0 = y0 + _dot((xp + pos_ref[kv, p:p + 1, :]).astype(BF16), w1_ref[kv, p])
            y1 = y1 + _dot((xp + pos_ref[kv, half + p:half + p + 1, :]).astype(BF16), w1_ref[kv, half + p])
        h1 = y0 + pltpu.roll(y1, ncp - 1, 0)
        act = (h1 * jax.nn.sigmoid(h1)).astype(BF16)
        if kv == 0:
            kc_ref[0, 0] = _dot(act, w2k_ref[...]).astype(BF16)
        else:
            out_t = _dot_nt(w2vt_ref[...], act)
            row = lax.broadcasted_iota(jnp.int32, out_t.shape, 0)
            vct_ref[0, 0] = (out_t + jnp.where(row == ONES_ROW, 1.0, 0.0)).astype(BF16)


def _compress(pk, pos, w1, w2k, w2vt):
    b, s, _ = pk.shape
    ncp = s // CMP_STRIDE
    full = lambda shape: pl.BlockSpec(shape, lambda bi, g: (0,) * len(shape))
    return pl.pallas_call(
        functools.partial(_compress_body, ncp=ncp),
        grid=(b, NSA_GROUPS),
        in_specs=[
            pl.BlockSpec((1, s, HEAD_PAD), lambda bi, g: (bi, 0, KSLOT_KC + g)),
            pl.BlockSpec((1, s, HEAD_PAD), lambda bi, g: (bi, 0, KSLOT_VC + g)),
            full((2, CMP_LEN, HEAD_PAD)),
            full((2, CMP_LEN, HEAD_PAD, CMP_HIDDEN)),
            full((CMP_HIDDEN, HEAD_PAD)),
            full((HEAD_PAD, CMP_HIDDEN)),
        ],
        out_specs=[
            pl.BlockSpec((1, 1, ncp, HEAD_PAD), lambda bi, g: (bi, g, 0, 0)),
            pl.BlockSpec((1, 1, HEAD_PAD, ncp), lambda bi, g: (bi, g, 0, 0)),
        ],
        out_shape=[
            jax.ShapeDtypeStruct((b, NSA_GROUPS, ncp, HEAD_PAD), BF16),
            jax.ShapeDtypeStruct((b, NSA_GROUPS, HEAD_PAD, ncp), BF16),
        ],
        scratch_shapes=[pltpu.VMEM((s, HEAD_PAD), F32)],
        compiler_params=pltpu.CompilerParams(
            dimension_semantics=("parallel", "arbitrary"), vmem_limit_bytes=VMEM_LIMIT),
        name="compress",
    )(pk, pk, pos, w1, w2k, w2vt)


def _flash_update(s_t, shift, pv_fn, m_ref, acc_ref):
    m_prev = m_ref[...]
    m_new = jnp.maximum(m_prev, jnp.max(s_t, axis=0, keepdims=True) + shift)
    alpha = jnp.exp(m_prev - m_new)
    p = jnp.exp(s_t - (m_new - shift)).astype(BF16)
    acc_ref[...] = alpha * acc_ref[...] + pv_fn(p)
    m_ref[...] = m_new


def _causal_stream(i, scores_fn, pv_fn, bias_ref, shift_far, sa_ref, sb_ref, m_ref, acc_ref):
    zero = jnp.zeros((1, WIDE), F32)
    bias_prev = lambda: bias_ref[0, TQ:2 * TQ, :]
    bias_own = lambda: bias_ref[0, 2 * TQ:3 * TQ, :]
    upd = lambda s_t, shift, j: _flash_update(s_t, shift, pv_fn(j), m_ref, acc_ref)
    n_far = jnp.maximum(i - 1, 0)
    m_ref[...] = jnp.full(m_ref.shape, M_INIT, F32)
    acc_ref[...] = jnp.zeros_like(acc_ref)
    sa_ref[...] = scores_fn(0)

    def pair(p, carry):
        j = 2 * p
        sb_ref[...] = scores_fn(j + 1)
        upd(sa_ref[...], shift_far, j)
        sa_ref[...] = scores_fn(j + 2)
        upd(sb_ref[...], shift_far, j + 1)
        return carry

    lax.fori_loop(0, n_far // 2, pair, 0)
    odd = lax.rem(n_far, 2) == 1

    @pl.when(i == 0)
    def _():
        upd(sa_ref[...] + bias_own(), zero, 0)

    @pl.when(jnp.logical_and(i >= 1, jnp.logical_not(odd)))
    def _():
        sb_ref[...] = scores_fn(i)
        upd(sa_ref[...] + bias_prev(), zero, i - 1)
        upd(sb_ref[...] + bias_own(), zero, i)

    @pl.when(odd)
    def _():
        sb_ref[...] = scores_fn(i - 1)
        upd(sa_ref[...], shift_far, i - 2)
        sa_ref[...] = scores_fn(i)
        upd(sb_ref[...] + bias_prev(), zero, i - 1)
        upd(sa_ref[...] + bias_own(), zero, i)


def _normalize_padded(acc):
    out = acc / acc[ONES_ROW:ONES_ROW + 1, :]
    return jnp.concatenate([out, jnp.zeros((HEAD_PAD - ACC_ROWS, WIDE), F32)], axis=0)


def _lane_block(x, r):
    return x[:, r * TQ:(r + 1) * TQ]


def _shift_row(c31_ref, head0):
    return jnp.concatenate([jnp.full((1, TQ), c31_ref[head0 + r], F32) for r in range(HG)], axis=1)


def _rank_select(score, n_rows, top, row_weight=None):
    blk = lax.broadcasted_iota(jnp.int32, score.shape, 0)
    cnt = jnp.zeros(score.shape, F32)
    for j in range(n_rows):
        row = score[j:j + 1, :]
        ahead = jnp.where(row > score, 1.0, jnp.where((row == score) & (blk > j), 1.0, 0.0))
        cnt = cnt + (ahead if row_weight is None else ahead * row_weight(j))
    return cnt < top


def _moba_body(c31_ref, q_ref, k_ref, vt_ref, pat_ref, bias_ref, o_ref,
               kmean_ref, kx_ref, qx_ref, sa_ref, sb_ref, m_ref, acc_ref, *, nb):
    hg = pl.program_id(1)
    i = pl.program_id(2)
    nbp = -(-nb // 16) * 16

    @pl.when(i == 0)
    def _():
        kmean_ref[...] = jnp.zeros_like(kmean_ref)
        for r in range(HG):
            kr = k_ref[0, :, r * HEAD_PAD:(r + 1) * HEAD_PAD]
            kx_ref[:, r * HEAD_PAD:(r + 1) * HEAD_PAD] = kr + pat_ref[...]
            for n in range(nb):
                kb = kr[n * MOBA_BLOCK:(n + 1) * MOBA_BLOCK, :].astype(F32)
                kmean_ref[r, n:n + 1, :] = jnp.mean(kb, axis=0, keepdims=True)

    for r in range(HG):
        q_t = q_ref[0, r * HEAD_PAD:(r + 1) * HEAD_PAD, :]
        kh, kl = _split_bf16(kmean_ref[r])
        gate = (_dot(kh, q_t) + _dot(kl, q_t))[:nbp]
        blk = lax.broadcasted_iota(jnp.int32, gate.shape, 0)
        top = _rank_select(gate, nb, MOBA_TOPK, lambda n: jnp.where(n < i, 1.0, 0.0))
        sel = ((blk < i) & top) | (blk == i)
        mask_rows = jnp.where(sel, 0.0, NEG).astype(BF16)
        pad_rows = jnp.zeros((HEAD_PAD - EXT0 - nbp, TQ), BF16)
        qx_ref[:, r * TQ:(r + 1) * TQ] = jnp.concatenate([q_t[:EXT0], mask_rows, pad_rows], axis=0)

    def scores(j):
        off = pl.multiple_of(j * TQ, TQ)
        return jnp.concatenate(
            [_dot(kx_ref[pl.ds(off, TQ), r * HEAD_PAD:(r + 1) * HEAD_PAD], qx_ref[:, r * TQ:(r + 1) * TQ])
             for r in range(HG)], axis=1)

    def pv_fn(j):
        off = pl.multiple_of(j * TQ, TQ)

        def fn(p):
            return jnp.concatenate(
                [_dot(vt_ref[0, r * HEAD_PAD:r * HEAD_PAD + ACC_ROWS, pl.ds(off, TQ)], _lane_block(p, r))
                 for r in range(HG)], axis=1)
        return fn

    _causal_stream(i, scores, pv_fn, bias_ref, _shift_row(c31_ref, hg * HG), sa_ref, sb_ref, m_ref, acc_ref)

    out = _normalize_padded(acc_ref[...])
    for r in range(HG):
        o_ref[0, :, r * HEAD_PAD:(r + 1) * HEAD_PAD] = jnp.transpose(_lane_block(out, r)).astype(BF16)


def _moba(pk, pt, pat, bias_win, c31):
    b, s, _ = pk.shape
    nb = s // MOBA_BLOCK
    hw = HG * HEAD_PAD
    return pl.pallas_call(
        functools.partial(_moba_body, nb=nb),
        grid=(b, MOBA_HEADS // HG, nb),
        in_specs=[
            pl.BlockSpec(memory_space=pltpu.SMEM),
            pl.BlockSpec((1, hw, TQ), lambda bi, hg, i: (bi, TSLOT_QA // HG + hg, i)),
            pl.BlockSpec((1, s, hw), lambda bi, hg, i: (bi, 0, KSLOT_KA // HG + hg)),
            pl.BlockSpec((1, hw, s), lambda bi, hg, i: (bi, TSLOT_VA // HG + hg, 0)),
            pl.BlockSpec((s, HEAD_PAD), lambda bi, hg, i: (0, 0)),
            pl.BlockSpec((1, 3 * TQ, WIDE), lambda bi, hg, i: (hg, 0, 0)),
        ],
        out_specs=pl.BlockSpec((1, TQ, hw), lambda bi, hg, i: (bi, i, hg)),
        out_shape=jax.ShapeDtypeStruct((b, s, MOBA_HEADS * HEAD_PAD), BF16),
        scratch_shapes=[
            pltpu.VMEM((HG, HEAD_PAD, HEAD_PAD), F32),
            pltpu.VMEM((s, hw), BF16),
            pltpu.VMEM((HEAD_PAD, WIDE), BF16),
            pltpu.VMEM((TQ, WIDE), F32),
            pltpu.VMEM((TQ, WIDE), F32),
            pltpu.VMEM((1, WIDE), F32),
            pltpu.VMEM((ACC_ROWS, WIDE), F32),
        ],
        compiler_params=pltpu.CompilerParams(
            dimension_semantics=("parallel", "parallel", "arbitrary"), vmem_limit_bytes=VMEM_LIMIT),
        name="moba",
    )(c31, pt, pk, pt, pat, bias_win)


def _nsa_body(c31_ref, q_ref, ks_ref, kw_ref, vst_ref, vwt_ref, kc_ref, vct_ref, gates_ref, pat_ref,
              bias_ref, bcmp_ref, ovt_ref, o_ref,
              ksx_ref, qx_ref, sa_ref, sb_ref, ms_ref, accs_ref, accw_ref, *, ncp, nslc, nq):
    g = pl.program_id(1)
    i = pl.program_id(2)

    @pl.when(i == 0)
    def _():
        ksx_ref[...] = ks_ref[0] + pat_ref[...]

    q4 = jnp.concatenate([q_ref[0, r * HEAD_PAD:(r + 1) * HEAD_PAD, :] for r in range(HG)], axis=1)

    band_off = pl.multiple_of((nq - 1 - i) * (TQ // CMP_STRIDE), TQ // CMP_STRIDE)
    s_t = _dot(kc_ref[0, 0], q4) + bcmp_ref[0, pl.ds(band_off, ncp), :]
    m = jnp.max(s_t, axis=0, keepdims=True)
    p = jnp.where(s_t > 0.5 * NEG, jnp.exp(s_t - m), 0.0)
    den = jnp.sum(p, axis=0, keepdims=True)
    pn = p / jnp.where(den > 0.0, den, 1.0)
    o_cmp = _dot(vct_ref[0, 0, :ACC_ROWS, :], pn.astype(BF16))
    psum = _lane_block(pn, 0)
    for r in range(1, HG):
        psum = psum + _lane_block(pn, r)

    ph, plo = _split_bf16(psum)
    imp = (_dot(ovt_ref[...], ph) + _dot(ovt_ref[...], plo))[:EXT0]
    blk = lax.broadcasted_iota(jnp.int32, imp.shape, 0)
    cur = (i * TQ + lax.broadcasted_iota(jnp.int32, imp.shape, 1)) // SLC_BLOCK
    forced = (blk == 0) | (blk == cur) | (blk == cur - 1)
    score = jnp.where(forced, 1e30, jnp.where(blk <= cur, imp, -1.0))
    score = jnp.where(blk < nslc, score, -2.0)
    sel = _rank_select(score, nslc, SLC_TOPN) & (blk < nslc)
    mask_rows = jnp.where(sel, 0.0, NEG).astype(BF16)
    qx_ref[...] = jnp.concatenate([q4[:EXT0], jnp.concatenate([mask_rows] * HG, axis=1)], axis=0)

    def scores(j):
        return _dot(ksx_ref[pl.ds(pl.multiple_of(j * TQ, TQ), TQ), :], qx_ref[...])

    def pv_fn(j):
        off = pl.multiple_of(j * TQ, TQ)
        return lambda p: _dot(vst_ref[0, :ACC_ROWS, pl.ds(off, TQ)], p)

    _causal_stream(i, scores, pv_fn, bias_ref, _shift_row(c31_ref, MOBA_HEADS + g * HG),
                   sa_ref, sb_ref, ms_ref, accs_ref)

    def win_pass(n_tiles):
        start = pl.multiple_of((i + 1 - n_tiles) * TQ, TQ)
        s_w = _dot(kw_ref[0, pl.ds(start, n_tiles * TQ), :], q4) + bias_ref[0, (3 - n_tiles) * TQ:, :]
        p_w = jnp.exp(s_w - jnp.max(s_w, axis=0, keepdims=True)).astype(BF16)
        accw_ref[...] = _dot(vwt_ref[0, :ACC_ROWS, pl.ds(start, n_tiles * TQ)], p_w)

    pl.when(i == 0)(functools.partial(win_pass, 1))
    pl.when(i == 1)(functools.partial(win_pass, 2))
    pl.when(i >= 2)(functools.partial(win_pass, 3))

    o_slc = accs_ref[...] / accs_ref[ONES_ROW:ONES_ROW + 1, :]
    o_win = accw_ref[...] / accw_ref[ONES_ROW:ONES_ROW + 1, :]
    gt = gates_ref[0]
    zpad = jnp.zeros((HEAD_PAD - ACC_ROWS, TQ), F32)
    for r in range(HG):
        row = N_GATES * r
        out = (gt[row:row + 1, :] * _lane_block(o_cmp, r) + gt[row + 1:row + 2, :] * _lane_block(o_slc, r)
               + gt[row + 2:row + 3, :] * _lane_block(o_win, r))
        o_ref[0, :, r * HEAD_PAD:(r + 1) * HEAD_PAD] = jnp.transpose(
            jnp.concatenate([out, zpad], axis=0)).astype(BF16)


def _nsa(pk, pt, kc, vct, gates_t, pat, bias_win, bias_cmp, ovt, c31):
    b, s, _ = pk.shape
    nq = s // TQ
    ncp = s // CMP_STRIDE
    nslc = s // SLC_BLOCK
    hw = HG * HEAD_PAD
    band_rows = bias_cmp.shape[1]
    hgrp = MOBA_HEADS // HG
    k_spec = lambda slot: pl.BlockSpec((1, s, HEAD_PAD), lambda bi, g, i: (bi, 0, slot + g))
    vt_spec = lambda slot: pl.BlockSpec((1, HEAD_PAD, s), lambda bi, g, i: (bi, slot + g, 0))
    bias_spec = lambda rows: pl.BlockSpec((1, rows, WIDE), lambda bi, g, i: (hgrp + g, 0, 0))
    return pl.pallas_call(
        functools.partial(_nsa_body, ncp=ncp, nslc=nslc, nq=nq),
        grid=(b, NSA_GROUPS, nq),
        in_specs=[
            pl.BlockSpec(memory_space=pltpu.SMEM),
            pl.BlockSpec((1, hw, TQ), lambda bi, g, i: (bi, TSLOT_QB // HG + g, i)),
            k_spec(KSLOT_KS), k_spec(KSLOT_KW), vt_spec(TSLOT_VS), vt_spec(TSLOT_VW),
            pl.BlockSpec((1, 1, ncp, HEAD_PAD), lambda bi, g, i: (bi, g, 0, 0)),
            pl.BlockSpec((1, 1, HEAD_PAD, ncp), lambda bi, g, i: (bi, g, 0, 0)),
            pl.BlockSpec((1, GATE_ROWS, TQ), lambda bi, g, i: (bi, g, i)),
            pl.BlockSpec((s, HEAD_PAD), lambda bi, g, i: (0, 0)),
            bias_spec(3 * TQ), bias_spec(band_rows),
            pl.BlockSpec((HEAD_PAD, ncp), lambda bi, g, i: (0, 0)),
        ],
        out_specs=pl.BlockSpec((1, TQ, hw), lambda bi, g, i: (bi, i, g)),
        out_shape=jax.ShapeDtypeStruct((b, s, NSA_HEADS * HEAD_PAD), BF16),
        scratch_shapes=[
            pltpu.VMEM((s, HEAD_PAD), BF16),
            pltpu.VMEM((HEAD_PAD, WIDE), BF16),
            pltpu.VMEM((TQ, WIDE), F32),
            pltpu.VMEM((TQ, WIDE), F32),
            pltpu.VMEM((1, WIDE), F32),
            pltpu.VMEM((ACC_ROWS, WIDE), F32),
            pltpu.VMEM((ACC_ROWS, WIDE), F32),
        ],
        compiler_params=pltpu.CompilerParams(
            dimension_semantics=("parallel", "parallel", "arbitrary"), vmem_limit_bytes=VMEM_LIMIT),
        name="nsa",
    )(c31, pt, pk, pk, pt, pt, kc, vct, gates_t, pat, bias_win, bias_cmp, ovt)


def _outproj_body(x_ref, oa_ref, ob_ref, wa_ref, wb_ref, o_ref):
    o_ref[...] = x_ref[...] + _dot(oa_ref[...], wa_ref[...]) + _dot(ob_ref[...], wb_ref[...])


def _outproj(x2d, oa, ob, wa, wb, *, tm=512):
    n = x2d.shape[0]
    ka = oa.shape[1]
    kb = ob.shape[1]
    return pl.pallas_call(
        _outproj_body,
        grid=(n // tm,),
        in_specs=[
            pl.BlockSpec((tm, D_MODEL), lambda i: (i, 0)),
            pl.BlockSpec((tm, ka), lambda i: (i, 0)),
            pl.BlockSpec((tm, kb), lambda i: (i, 0)),
            pl.BlockSpec((ka, D_MODEL), lambda i: (0, 0)),
            pl.BlockSpec((kb, D_MODEL), lambda i: (0, 0)),
        ],
        out_specs=pl.BlockSpec((tm, D_MODEL), lambda i: (i, 0)),
        out_shape=jax.ShapeDtypeStruct((n, D_MODEL), F32),
        compiler_params=pltpu.CompilerParams(
            dimension_semantics=("parallel",), vmem_limit_bytes=VMEM_LIMIT),
        name="outproj",
    )(x2d, oa, ob, wa, wb)


def _rel_bucket(dist):
    n = jnp.maximum(dist, 0)
    nf = jnp.maximum(n, 1).astype(F32)
    large = REL_MAX_EXACT + (jnp.log(nf / REL_MAX_EXACT) / math.log(REL_MAX_DIST / REL_MAX_EXACT)
                             * (REL_BUCKETS - REL_MAX_EXACT)).astype(jnp.int32)
    return jnp.where(n < REL_MAX_EXACT, n, jnp.minimum(large, REL_BUCKETS - 1))


def _bias_body(tab_ref, *refs):
    h = pl.program_id(0)
    n_tab = len(refs) // 2
    for bk_ref, out_ref in zip(refs[:n_tab], refs[n_tab:]):
        bk = bk_ref[...]
        acc = jnp.full(bk.shape, NEG, F32)
        for bucket in range(REL_BUCKETS):
            acc = jnp.where(bk == bucket, tab_ref[h, bucket], acc)
        out_ref[0] = acc


def _bias_tables(rel_bias, ncp, nq):
    tab = rel_bias.T.astype(F32)
    n_heads = tab.shape[0]
    c = jnp.arange(TQ)[:, None]
    r = jnp.arange(TQ)[None, :]
    d_own = r - c
    bk_own = jnp.where(d_own >= 0, _rel_bucket(d_own), -1)
    bk_prev = _rel_bucket(d_own + TQ)
    bk_win2 = jnp.where(d_own + 2 * TQ < WINDOW, _rel_bucket(d_own + 2 * TQ), -1)
    band_rows = ncp + (nq - 1) * (TQ // CMP_STRIDE)
    u = jnp.arange(band_rows)[:, None] - (nq - 1) * (TQ // CMP_STRIDE)
    d_cmp = r - CMP_STRIDE * u - (CMP_LEN - 1)
    bk_cmp = jnp.where(d_cmp >= 0, _rel_bucket(d_cmp), -1)
    tables = [t.astype(jnp.int32) for t in (jnp.concatenate([bk_win2, bk_prev, bk_own], axis=0), bk_cmp)]
    outs = pl.pallas_call(
        _bias_body,
        grid=(n_heads,),
        in_specs=[pl.BlockSpec(memory_space=pltpu.SMEM)]
        + [pl.BlockSpec(t.shape, lambda h: (0, 0)) for t in tables],
        out_specs=[pl.BlockSpec((1, t.shape[0], TQ), lambda h: (h // HG, 0, h % HG)) for t in tables],
        out_shape=[jax.ShapeDtypeStruct((n_heads // HG, t.shape[0], WIDE), F32) for t in tables],
        compiler_params=pltpu.CompilerParams(dimension_semantics=("parallel",), vmem_limit_bytes=VMEM_LIMIT),
        name="bias_tiles",
    )(tab, *tables)
    return outs, tab[:, REL_BUCKETS - 1]


def _pad_heads(w, n_heads):
    rows = w.shape[0]
    w = w.reshape(rows, n_heads, HEAD_DIM)
    w = jnp.pad(w, ((0, 0), (0, 0), (0, HEAD_PAD - HEAD_DIM)))
    return w.reshape(rows, n_heads * HEAD_PAD)


def _slc_overlap_t(ncp, nslc):
    ratio = SLC_BLOCK // CMP_STRIDE
    mc = CMP_LEN // CMP_STRIDE
    w = np.zeros((HEAD_PAD, nslc * ratio + mc), np.float32)
    for j in range(nslc):
        for a in range(ratio):
            for m in range(mc):
                w[j, ratio * j + a + m] += 1.0
    n_cmp = ncp - 1
    out = np.zeros((HEAD_PAD, ncp), np.float32)
    out[:, :n_cmp] = w[:, :n_cmp]
    return out


def _block_onehot(s, block):
    pat = np.zeros((s, HEAD_PAD), np.float32)
    pat[np.arange(s), EXT0 + np.arange(s) // block] = 1.0
    return pat


def kernel(x, norm_ffn1, w_ffn1_gate, w_ffn1_up, w_ffn1_down, norm_mix, w_in, cmp_pos_k, cmp_w1_k, cmp_w2_k,
           cmp_pos_v, cmp_w1_v, cmp_w2_v, w_out, norm_ffn2, w_ffn2_gate, w_ffn2_up, w_ffn2_down, rel_bias,
           norm_final):
    b, s, d = x.shape
    assert d == D_MODEL and s % (2 * TQ) == 0 and norm_ffn1.shape[0] == 1
    assert s // SLC_BLOCK <= HEAD_PAD - EXT0, "selection-mask rows must fit beside the 64 head dims"
    n = b * s
    nq = s // TQ
    ncp = s // CMP_STRIDE
    nslc = s // SLC_BLOCK
    scale = HEAD_DIM ** -0.5

    sizes = (512, 512, 512, 512) + (128,) * 6 + (24,)
    splits = np.cumsum(sizes)[:-1].tolist()
    wqa, wka, wva, wqb, wkc, wvc, wks, wvs, wkw, wvw, wgb = jnp.split(w_in[0], splits, axis=1)
    w_k = jnp.concatenate([_pad_heads(wka, 8), _pad_heads(wkc, 2), _pad_heads(wvc, 2), _pad_heads(wks, 2),
                           _pad_heads(wkw, 2)], axis=1).astype(BF16)
    w_t = jnp.concatenate([_pad_heads(wqa * scale, 8), _pad_heads(wva, 8), _pad_heads(wqb * scale, 8),
                           _pad_heads(wvs, 2), _pad_heads(wvw, 2)], axis=1).T.astype(BF16)
    ones_np = np.zeros((N_TSLOTS * HEAD_PAD, 1), np.float32)
    for slot in list(range(TSLOT_VA, TSLOT_VA + 8)) + [TSLOT_VS, TSLOT_VS + 1, TSLOT_VW, TSLOT_VW + 1]:
        ones_np[slot * HEAD_PAD + ONES_ROW, 0] = 1.0
    ones_col = jnp.asarray(ones_np)
    wgt = wgb.reshape(D_MODEL, NSA_GROUPS, NSA_REP * N_GATES)
    wgt = jnp.pad(wgt, ((0, 0), (0, 0), (0, GATE_ROWS - NSA_REP * N_GATES)))
    wgt = wgt.reshape(D_MODEL, NSA_GROUPS * GATE_ROWS).T.astype(BF16)

    def cmp_params(pos, w1, w2):
        pos_p = jnp.pad(pos[0], ((0, 0), (0, HEAD_PAD - HEAD_DIM)))
        w1_p = jnp.pad(w1[0].reshape(CMP_LEN, HEAD_DIM, CMP_HIDDEN), ((0, 0), (0, HEAD_PAD - HEAD_DIM), (0, 0)))
        w2_p = jnp.pad(w2[0], ((0, 0), (0, HEAD_PAD - HEAD_DIM)))
        return pos_p, w1_p.astype(BF16), w2_p.astype(BF16)

    pos_k, w1k, w2k = cmp_params(cmp_pos_k, cmp_w1_k, cmp_w2_k)
    pos_v, w1v, w2v = cmp_params(cmp_pos_v, cmp_w1_v, cmp_w2_v)
    cmp_pos = jnp.stack([pos_k, pos_v])
    cmp_w1 = jnp.stack([w1k, w1v])

    wo = w_out[0]
    wo_a = _pad_heads(wo[:MOBA_HEADS * HEAD_DIM].T, MOBA_HEADS).T.astype(BF16)
    wo_b = _pad_heads(wo[MOBA_HEADS * HEAD_DIM:].T, NSA_HEADS).T.astype(BF16)

    (bias_win, bias_cmp), c31 = _bias_tables(rel_bias, ncp, nq)
    ovt = jnp.asarray(_slc_overlap_t(ncp, nslc)).astype(BF16)
    pat_moba = jnp.asarray(_block_onehot(s, MOBA_BLOCK)).astype(BF16)
    pat_slc = jnp.asarray(_block_onehot(s, SLC_BLOCK)).astype(BF16)

    x2d = x.reshape(n, D_MODEL)
    nf = norm_final.reshape(1, D_MODEL)
    x1 = _ffn(x2d, norm_ffn1, w_ffn1_gate[0].astype(BF16), w_ffn1_up[0].astype(BF16),
              w_ffn1_down[0].astype(BF16), nf, final_norm=False)
    pk = _inproj_k(x1, norm_mix, w_k).reshape(b, s, N_KSLOTS * HEAD_PAD)
    pt, gates_t = _inproj_t(x1, norm_mix, w_t, ones_col, wgt, b, s)
    kc, vct = _compress(pk, cmp_pos, cmp_w1, w2k, w2v.T)
    o_a = _moba(pk, pt, pat_moba, bias_win, c31)
    o_b = _nsa(pk, pt, kc, vct, gates_t, pat_slc, bias_win, bias_cmp, ovt, c31)
    x2 = _outproj(x1, o_a.reshape(n, -1), o_b.reshape(n, -1), wo_a, wo_b)
    y = _ffn(x2, norm_ffn2, w_ffn2_gate[0].astype(BF16), w_ffn2_up[0].astype(BF16),
             w_ffn2_down[0].astype(BF16), nf, final_norm=True)
    return y.reshape(b, s, D_MODEL)
```

```python
import functools
import math

import numpy as np
import jax
import jax.numpy as jnp
from jax import lax
from jax.experimental import pallas as pl
from jax.experimental.pallas import tpu as pltpu

D_MODEL = 1024
D_FF = 2816
HEAD_DIM = 64
HEAD_PAD = 128
ONES_ROW = HEAD_DIM
EXT0 = HEAD_DIM
MOBA_HEADS = 8
NSA_HEADS = 8
NSA_GROUPS = 2
NSA_REP = NSA_HEADS // NSA_GROUPS
HG = 4
MOBA_BLOCK = 256
MOBA_TOPK = 3
CMP_LEN = 32
CMP_STRIDE = 16
CMP_HIDDEN = 256
SLC_BLOCK = 64
SLC_TOPN = 16
WINDOW = 512
N_GATES = 3
GATE_ROWS = 16
REL_BUCKETS = 32
REL_MAX_EXACT = REL_BUCKETS // 2
REL_MAX_DIST = 128
RMS_EPS = 1e-6
TQ = 256
WIDE = HG * TQ
NEG = -1e30
M_INIT = -3e38
LOG2E = 1.4426950408889634
ACC_ROWS = 80
VMEM_LIMIT = 56 * 1024 * 1024

KSLOT_KA, KSLOT_KC, KSLOT_VC, KSLOT_KS, KSLOT_KW = 0, 8, 10, 12, 14
N_KSLOTS = 16
TSLOT_QA, TSLOT_VA, TSLOT_QB, TSLOT_VS, TSLOT_VW = 0, 8, 16, 24, 26
N_TSLOTS = 28

F32 = jnp.float32
BF16 = jnp.bfloat16


def _dot(a, b):
    return jnp.dot(a, b, preferred_element_type=F32)


def _dot_nt(a, b):
    return lax.dot_general(a, b, (((1,), (1,)), ((), ())), preferred_element_type=F32)


def _split_bf16(x):
    hi = x.astype(BF16)
    lo = (x - hi.astype(F32)).astype(BF16)
    return hi, lo


def _rmsnorm(x, g):
    ms = jnp.mean(x * x, axis=-1, keepdims=True)
    return x * lax.rsqrt(ms + RMS_EPS) * g


def _ffn_body(x_ref, g_ref, wg_ref, wu_ref, wd_ref, gf_ref, o_ref, h_ref, acc_ref, *, final_norm):
    j = pl.program_id(1)

    @pl.when(j == 0)
    def _():
        h_ref[...] = _rmsnorm(x_ref[...], g_ref[...]).astype(BF16)
        acc_ref[...] = jnp.zeros_like(acc_ref)

    h = h_ref[...]
    a = _dot(h, wg_ref[...])
    u = _dot(h, wu_ref[...])
    z = (a * jax.nn.sigmoid(a)) * u
    acc_ref[...] += _dot(z.astype(BF16), wd_ref[...])

    @pl.when(j == pl.num_programs(1) - 1)
    def _():
        y = x_ref[...] + 0.5 * acc_ref[...]
        if final_norm:
            y = _rmsnorm(y, gf_ref[...])
        o_ref[...] = y


def _ffn(x2d, g, wg, wu, wd, gf, *, final_norm, tm=512, tf=1408):
    n = x2d.shape[0]
    grid = (n // tm, D_FF // tf)
    return pl.pallas_call(
        functools.partial(_ffn_body, final_norm=final_norm),
        grid=grid,
        in_specs=[
            pl.BlockSpec((tm, D_MODEL), lambda i, j: (i, 0)),
            pl.BlockSpec((1, D_MODEL), lambda i, j: (0, 0)),
            pl.BlockSpec((D_MODEL, tf), lambda i, j: (0, j)),
            pl.BlockSpec((D_MODEL, tf), lambda i, j: (0, j)),
            pl.BlockSpec((tf, D_MODEL), lambda i, j: (j, 0)),
            pl.BlockSpec((1, D_MODEL), lambda i, j: (0, 0)),
        ],
        out_specs=pl.BlockSpec((tm, D_MODEL), lambda i, j: (i, 0)),
        out_shape=jax.ShapeDtypeStruct((n, D_MODEL), F32),
        scratch_shapes=[pltpu.VMEM((tm, D_MODEL), BF16), pltpu.VMEM((tm, D_MODEL), F32)],
        compiler_params=pltpu.CompilerParams(
            dimension_semantics=("parallel", "arbitrary"), vmem_limit_bytes=VMEM_LIMIT),
        name="ffn_final" if final_norm else "ffn",
    )(x2d, g, wg, wu, wd, gf)


def _inproj_body(x_ref, g_ref, wk_ref, wt_ref, ones_ref, wgt_ref, pk_ref, pt_ref, gates_ref):
    h = _rmsnorm(x_ref[...], g_ref[...]).astype(BF16)
    pk_ref[...] = _dot(h, wk_ref[...]).astype(BF16)
    pt_ref[0] = (_dot_nt(wt_ref[...], h) + ones_ref[...]).astype(BF16)
    gates_ref[0] = jax.nn.sigmoid(_dot_nt(wgt_ref[...], h))


def _inproj(x2d, g, wk, wt, ones_col, wgt, b, s, *, tm=256):
    n = x2d.shape[0]
    nk = wk.shape[1]
    nt = wt.shape[0]
    ng = wgt.shape[0]
    spt = s // tm
    const = lambda shape: pl.BlockSpec(shape, lambda i: (0, 0))
    return pl.pallas_call(
        _inproj_body,
        grid=(n // tm,),
        in_specs=[
            pl.BlockSpec((tm, D_MODEL), lambda i: (i, 0)),
            const((1, D_MODEL)), const((D_MODEL, nk)), const((nt, D_MODEL)), const((nt, 1)), const((ng, D_MODEL)),
        ],
        out_specs=[
            pl.BlockSpec((tm, nk), lambda i: (i, 0)),
            pl.BlockSpec((1, nt, tm), lambda i: (i // spt, 0, i % spt)),
            pl.BlockSpec((1, ng, tm), lambda i: (i // spt, 0, i % spt)),
        ],
        out_shape=[jax.ShapeDtypeStruct((n, nk), BF16), jax.ShapeDtypeStruct((b, nt, s), BF16),
                   jax.ShapeDtypeStruct((b, ng, s), F32)],
        compiler_params=pltpu.CompilerParams(
            dimension_semantics=("parallel",), vmem_limit_bytes=VMEM_LIMIT),
        name="inproj",
    )(x2d, g, wk, wt, ones_col, wgt)


def _compress_body(kraw_ref, vraw_ref, pos_ref, w1_ref, w2k_ref, w2vt_ref, kc_ref, vct_ref, xs_ref, *, ncp):
    half = CMP_LEN // 2
    for kv, raw_ref in enumerate((kraw_ref, vraw_ref)):
        xs_ref[...] = raw_ref[0].astype(F32)
        y0 = jnp.zeros((ncp, CMP_HIDDEN), F32)
        y1 = jnp.zeros((ncp, CMP_HIDDEN), F32)
        for p in range(half):
            xp = xs_ref[pl.ds(p, ncp, stride=CMP_STRIDE), :]
            y0 = y0 + _dot((xp + pos_ref[kv, p:p + 1, :]).astype(BF16), w1_ref[kv, p])
            y1 = y1 + _dot((xp + pos_ref[kv, half + p:half + p + 1, :]).astype(BF16), w1_ref[kv, half + p])
        h1 = y0 + pltpu.roll(y1, ncp - 1, 0)
        act = (h1 * jax.nn.sigmoid(h1)).astype(BF16)
        if kv == 0:
            kc_ref[0, 0] = _dot(act, w2k_ref[...]).astype(BF16)
        else:
            out_t = _dot_nt(w2vt_ref[...], act)
            row = lax.broadcasted_iota(jnp.int32, out_t.shape, 0)
            vct_ref[0, 0] = (out_t + jnp.where(row == ONES_ROW, 1.0, 0.0)).astype(BF16)


def _compress(pk, pos, w1, w2k, w2vt):
    b, s, _ = pk.shape
    ncp = s // CMP_STRIDE
    full = lambda shape: pl.BlockSpec(shape, lambda bi, g: (0,) * len(shape))
    return pl.pallas_call(
        functools.partial(_compress_body, ncp=ncp),
        grid=(b, NSA_GROUPS),
        in_specs=[
            pl.BlockSpec((1, s, HEAD_PAD), lambda bi, g: (bi, 0, KSLOT_KC + g)),
            pl.BlockSpec((1, s, HEAD_PAD), lambda bi, g: (bi, 0, KSLOT_VC + g)),
            full((2, CMP_LEN, HEAD_PAD)),
            full((2, CMP_LEN, HEAD_PAD, CMP_HIDDEN)),
            full((CMP_HIDDEN, HEAD_PAD)),
            full((HEAD_PAD, CMP_HIDDEN)),
        ],
        out_specs=[
            pl.BlockSpec((1, 1, ncp, HEAD_PAD), lambda bi, g: (bi, g, 0, 0)),
            pl.BlockSpec((1, 1, HEAD_PAD, ncp), lambda bi, g: (bi, g, 0, 0)),
        ],
        out_shape=[
            jax.ShapeDtypeStruct((b, NSA_GROUPS, ncp, HEAD_PAD), BF16),
            jax.ShapeDtypeStruct((b, NSA_GROUPS, HEAD_PAD, ncp), BF16),
        ],
        scratch_shapes=[pltpu.VMEM((s, HEAD_PAD), F32)],
        compiler_params=pltpu.CompilerParams(
            dimension_semantics=("parallel", "arbitrary"), vmem_limit_bytes=VMEM_LIMIT),
        name="compress",
    )(pk, pk, pos, w1, w2k, w2vt)


def _flash_update(s_t, shift, pv_fn, m_ref, acc_ref):
    m_prev = m_ref[...]
    m_new = jnp.maximum(m_prev, jnp.max(s_t, axis=0, keepdims=True) + shift)
    alpha = jnp.exp2(m_prev - m_new)
    p = jnp.exp2(s_t - (m_new - shift)).astype(BF16)
    acc_ref[...] = alpha * acc_ref[...] + pv_fn(p)
    m_ref[...] = m_new


def _causal_stream(i, scores_fn, pv_fn, bias_ref, shift_far, sa_ref, sb_ref, m_ref, acc_ref):
    zero = jnp.zeros((1, WIDE), F32)
    bias_prev = lambda: bias_ref[0, TQ:2 * TQ, :]
    bias_own = lambda: bias_ref[0, 2 * TQ:3 * TQ, :]
    upd = lambda s_t, shift, j: _flash_update(s_t, shift, pv_fn(j), m_ref, acc_ref)
    n_far = jnp.maximum(i - 1, 0)
    m_ref[...] = jnp.full(m_ref.shape, M_INIT, F32)
    acc_ref[...] = jnp.zeros_like(acc_ref)
    sa_ref[...] = scores_fn(0)

    def pair(p, carry):
        j = 2 * p
        sb_ref[...] = scores_fn(j + 1)
        upd(sa_ref[...], shift_far, j)
        sa_ref[...] = scores_fn(j + 2)
        upd(sb_ref[...], shift_far, j + 1)
        return carry

    lax.fori_loop(0, n_far // 2, pair, 0)
    odd = lax.rem(n_far, 2) == 1

    @pl.when(i == 0)
    def _():
        upd(sa_ref[...] + bias_own(), zero, 0)

    @pl.when(jnp.logical_and(i >= 1, jnp.logical_not(odd)))
    def _():
        sb_ref[...] = scores_fn(i)
        upd(sa_ref[...] + bias_prev(), zero, i - 1)
        upd(sb_ref[...] + bias_own(), zero, i)

    @pl.when(odd)
    def _():
        sb_ref[...] = scores_fn(i - 1)
        upd(sa_ref[...], shift_far, i - 2)
        sa_ref[...] = scores_fn(i)
        upd(sb_ref[...] + bias_prev(), zero, i - 1)
        upd(sa_ref[...] + bias_own(), zero, i)


def _normalize_padded(acc):
    out = acc / acc[ONES_ROW:ONES_ROW + 1, :]
    return jnp.concatenate([out, jnp.zeros((HEAD_PAD - ACC_ROWS, WIDE), F32)], axis=0)


def _lane_block(x, r):
    return x[:, r * TQ:(r + 1) * TQ]


def _shift_row(c31_ref, head0):
    return jnp.concatenate([jnp.full((1, TQ), c31_ref[head0 + r], F32) for r in range(HG)], axis=1)


def _rank_select(score, n_rows, top, row_weight=None):
    blk = lax.broadcasted_iota(jnp.int32, score.shape, 0)
    cnt = jnp.zeros(score.shape, F32)
    for j in range(n_rows):
        row = score[j:j + 1, :]
        ahead = jnp.where(row > score, 1.0, jnp.where((row == score) & (blk > j), 1.0, 0.0))
        cnt = cnt + (ahead if row_weight is None else ahead * row_weight(j))
    return cnt < top


def _moba_body(c31_ref, q_ref, k_ref, vt_ref, pat_ref, bias_ref, o_ref,
               kmean_ref, kx_ref, qx_ref, sa_ref, sb_ref, m_ref, acc_ref, *, nb):
    hg = pl.program_id(1)
    i = pl.program_id(2)
    nbp = -(-nb // 16) * 16

    @pl.when(i == 0)
    def _():
        kmean_ref[...] = jnp.zeros_like(kmean_ref)
        for r in range(HG):
            kr = k_ref[0, :, r * HEAD_PAD:(r + 1) * HEAD_PAD]
            kx_ref[:, r * HEAD_PAD:(r + 1) * HEAD_PAD] = kr + pat_ref[...]
            for n in range(nb):
                kb = kr[n * MOBA_BLOCK:(n + 1) * MOBA_BLOCK, :].astype(F32)
                kmean_ref[r, n:n + 1, :] = jnp.mean(kb, axis=0, keepdims=True)

    for r in range(HG):
        q_t = q_ref[0, r * HEAD_PAD:(r + 1) * HEAD_PAD, :]
        kh, kl = _split_bf16(kmean_ref[r])
        gate = (_dot(kh, q_t) + _dot(kl, q_t))[:nbp]
        blk = lax.broadcasted_iota(jnp.int32, gate.shape, 0)
        top = _rank_select(gate, nb, MOBA_TOPK, lambda n: jnp.where(n < i, 1.0, 0.0))
        sel = ((blk < i) & top) | (blk == i)
        mask_rows = jnp.where(sel, 0.0, NEG).astype(BF16)
        pad_rows = jnp.zeros((HEAD_PAD - EXT0 - nbp, TQ), BF16)
        qx_ref[:, r * TQ:(r + 1) * TQ] = jnp.concatenate([q_t[:EXT0], mask_rows, pad_rows], axis=0)

    def scores(j):
        off = pl.multiple_of(j * TQ, TQ)
        return jnp.concatenate(
            [_dot(kx_ref[pl.ds(off, TQ), r * HEAD_PAD:(r + 1) * HEAD_PAD], qx_ref[:, r * TQ:(r + 1) * TQ])
             for r in range(HG)], axis=1)

    def pv_fn(j):
        off = pl.multiple_of(j * TQ, TQ)

        def fn(p):
            return jnp.concatenate(
                [_dot(vt_ref[0, r * HEAD_PAD:r * HEAD_PAD + ACC_ROWS, pl.ds(off, TQ)], _lane_block(p, r))
                 for r in range(HG)], axis=1)
        return fn

    _causal_stream(i, scores, pv_fn, bias_ref, _shift_row(c31_ref, hg * HG), sa_ref, sb_ref, m_ref, acc_ref)

    out = _normalize_padded(acc_ref[...])
    for r in range(HG):
        o_ref[0, :, r * HEAD_PAD:(r + 1) * HEAD_PAD] = jnp.transpose(_lane_block(out, r)).astype(BF16)


def _moba(pk, pt, pat, bias_win, c31):
    b, s, _ = pk.shape
    nb = s // MOBA_BLOCK
    hw = HG * HEAD_PAD
    return pl.pallas_call(
        functools.partial(_moba_body, nb=nb),
        grid=(b, MOBA_HEADS // HG, nb),
        in_specs=[
            pl.BlockSpec(memory_space=pltpu.SMEM),
            pl.BlockSpec((1, hw, TQ), lambda bi, hg, i: (bi, TSLOT_QA // HG + hg, i)),
            pl.BlockSpec((1, s, hw), lambda bi, hg, i: (bi, 0, KSLOT_KA // HG + hg)),
            pl.BlockSpec((1, hw, s), lambda bi, hg, i: (bi, TSLOT_VA // HG + hg, 0)),
            pl.BlockSpec((s, HEAD_PAD), lambda bi, hg, i: (0, 0)),
            pl.BlockSpec((1, 3 * TQ, WIDE), lambda bi, hg, i: (hg, 0, 0)),
        ],
        out_specs=pl.BlockSpec((1, TQ, hw), lambda bi, hg, i: (bi, i, hg)),
        out_shape=jax.ShapeDtypeStruct((b, s, MOBA_HEADS * HEAD_PAD), BF16),
        scratch_shapes=[
            pltpu.VMEM((HG, HEAD_PAD, HEAD_PAD), F32),
            pltpu.VMEM((s, hw), BF16),
            pltpu.VMEM((HEAD_PAD, WIDE), BF16),
            pltpu.VMEM((TQ, WIDE), F32),
            pltpu.VMEM((TQ, WIDE), F32),
            pltpu.VMEM((1, WIDE), F32),
            pltpu.VMEM((ACC_ROWS, WIDE), F32),
        ],
        compiler_params=pltpu.CompilerParams(
            dimension_semantics=("parallel", "parallel", "arbitrary"), vmem_limit_bytes=VMEM_LIMIT),
        name="moba",
    )(c31, pt, pk, pt, pat, bias_win)


def _nsa_body(c31_ref, q_ref, ks_ref, kw_ref, vst_ref, vwt_ref, kc_ref, vct_ref, gates_ref, pat_ref,
              bias_ref, bcmp_ref, ovt_ref, o_ref,
              ksx_ref, qx_ref, sa_ref, sb_ref, ms_ref, accs_ref, accw_ref, *, ncp, nslc, nq):
    g = pl.program_id(1)
    i = pl.program_id(2)

    @pl.when(i == 0)
    def _():
        ksx_ref[...] = ks_ref[0] + pat_ref[...]

    q4 = jnp.concatenate([q_ref[0, r * HEAD_PAD:(r + 1) * HEAD_PAD, :] for r in range(HG)], axis=1)

    band_off = pl.multiple_of((nq - 1 - i) * (TQ // CMP_STRIDE), TQ // CMP_STRIDE)
    s_t = _dot(kc_ref[0, 0], q4) + bcmp_ref[0, pl.ds(band_off, ncp), :]
    m = jnp.max(s_t, axis=0, keepdims=True)
    p = jnp.where(s_t > 0.5 * NEG, jnp.exp2(s_t - m), 0.0)
    den = jnp.sum(p, axis=0, keepdims=True)
    pn = p / jnp.where(den > 0.0, den, 1.0)
    o_cmp = _dot(vct_ref[0, 0, :ACC_ROWS, :], pn.astype(BF16))
    psum = _lane_block(pn, 0)
    for r in range(1, HG):
        psum = psum + _lane_block(pn, r)

    ph, plo = _split_bf16(psum)
    imp = (_dot(ovt_ref[...], ph) + _dot(ovt_ref[...], plo))[:EXT0]
    blk = lax.broadcasted_iota(jnp.int32, imp.shape, 0)
    cur = (i * TQ + lax.broadcasted_iota(jnp.int32, imp.shape, 1)) // SLC_BLOCK
    forced = (blk == 0) | (blk == cur) | (blk == cur - 1)
    score = jnp.where(forced, 1e30, jnp.where(blk <= cur, imp, -1.0))
    score = jnp.where(blk < nslc, score, -2.0)
    sel = _rank_select(score, nslc, SLC_TOPN) & (blk < nslc)
    mask_rows = jnp.where(sel, 0.0, NEG).astype(BF16)
    qx_ref[...] = jnp.concatenate([q4[:EXT0], jnp.concatenate([mask_rows] * HG, axis=1)], axis=0)

    def scores(j):
        return _dot(ksx_ref[pl.ds(pl.multiple_of(j * TQ, TQ), TQ), :], qx_ref[...])

    def pv_fn(j):
        off = pl.multiple_of(j * TQ, TQ)
        return lambda p: _dot(vst_ref[0, :ACC_ROWS, pl.ds(off, TQ)], p)

    _causal_stream(i, scores, pv_fn, bias_ref, _shift_row(c31_ref, MOBA_HEADS + g * HG),
                   sa_ref, sb_ref, ms_ref, accs_ref)

    def win_pass(n_tiles):
        start = pl.multiple_of((i + 1 - n_tiles) * TQ, TQ)
        s_w = _dot(kw_ref[0, pl.ds(start, n_tiles * TQ), :], q4) + bias_ref[0, (3 - n_tiles) * TQ:, :]
        p_w = jnp.exp2((s_w - jnp.max(s_w, axis=0, keepdims=True)).astype(BF16))
        accw_ref[...] = _dot(vwt_ref[0, :ACC_ROWS, pl.ds(start, n_tiles * TQ)], p_w)

    pl.when(i == 0)(functools.partial(win_pass, 1))
    pl.when(i == 1)(functools.partial(win_pass, 2))
    pl.when(i >= 2)(functools.partial(win_pass, 3))

    o_slc = accs_ref[...] / accs_ref[ONES_ROW:ONES_ROW + 1, :]
    o_win = accw_ref[...] / accw_ref[ONES_ROW:ONES_ROW + 1, :]
    gt = gates_ref[0]
    zpad = jnp.zeros((HEAD_PAD - ACC_ROWS, TQ), F32)
    for r in range(HG):
        row = N_GATES * r
        out = (gt[row:row + 1, :] * _lane_block(o_cmp, r) + gt[row + 1:row + 2, :] * _lane_block(o_slc, r)
               + gt[row + 2:row + 3, :] * _lane_block(o_win, r))
        o_ref[0, :, r * HEAD_PAD:(r + 1) * HEAD_PAD] = jnp.transpose(
            jnp.concatenate([out, zpad], axis=0)).astype(BF16)


def _nsa(pk, pt, kc, vct, gates_t, pat, bias_win, bias_cmp, ovt, c31):
    b, s, _ = pk.shape
    nq = s // TQ
    ncp = s // CMP_STRIDE
    nslc = s // SLC_BLOCK
    hw = HG * HEAD_PAD
    band_rows = bias_cmp.shape[1]
    hgrp = MOBA_HEADS // HG
    k_spec = lambda slot: pl.BlockSpec((1, s, HEAD_PAD), lambda bi, g, i: (bi, 0, slot + g))
    vt_spec = lambda slot: pl.BlockSpec((1, HEAD_PAD, s), lambda bi, g, i: (bi, slot + g, 0))
    bias_spec = lambda rows: pl.BlockSpec((1, rows, WIDE), lambda bi, g, i: (hgrp + g, 0, 0))
    return pl.pallas_call(
        functools.partial(_nsa_body, ncp=ncp, nslc=nslc, nq=nq),
        grid=(b, NSA_GROUPS, nq),
        in_specs=[
            pl.BlockSpec(memory_space=pltpu.SMEM),
            pl.BlockSpec((1, hw, TQ), lambda bi, g, i: (bi, TSLOT_QB // HG + g, i)),
            k_spec(KSLOT_KS), k_spec(KSLOT_KW), vt_spec(TSLOT_VS), vt_spec(TSLOT_VW),
            pl.BlockSpec((1, 1, ncp, HEAD_PAD), lambda bi, g, i: (bi, g, 0, 0)),
            pl.BlockSpec((1, 1, HEAD_PAD, ncp), lambda bi, g, i: (bi, g, 0, 0)),
            pl.BlockSpec((1, GATE_ROWS, TQ), lambda bi, g, i: (bi, g, i)),
            pl.BlockSpec((s, HEAD_PAD), lambda bi, g, i: (0, 0)),
            bias_spec(3 * TQ), bias_spec(band_rows),
            pl.BlockSpec((HEAD_PAD, ncp), lambda bi, g, i: (0, 0)),
        ],
        out_specs=pl.BlockSpec((1, TQ, hw), lambda bi, g, i: (bi, i, g)),
        out_shape=jax.ShapeDtypeStruct((b, s, NSA_HEADS * HEAD_PAD), BF16),
        scratch_shapes=[
            pltpu.VMEM((s, HEAD_PAD), BF16),
            pltpu.VMEM((HEAD_PAD, WIDE), BF16),
            pltpu.VMEM((TQ, WIDE), F32),
            pltpu.VMEM((TQ, WIDE), F32),
            pltpu.VMEM((1, WIDE), F32),
            pltpu.VMEM((ACC_ROWS, WIDE), F32),
            pltpu.VMEM((ACC_ROWS, WIDE), F32),
        ],
        compiler_params=pltpu.CompilerParams(
            dimension_semantics=("parallel", "parallel", "arbitrary"), vmem_limit_bytes=VMEM_LIMIT),
        name="nsa",
    )(c31, pt, pk, pk, pt, pt, kc, vct, gates_t, pat, bias_win, bias_cmp, ovt)


def _outproj_body(x_ref, oa_ref, ob_ref, wa_ref, wb_ref, o_ref):
    o_ref[...] = x_ref[...] + _dot(oa_ref[...], wa_ref[...]) + _dot(ob_ref[...], wb_ref[...])


def _outproj(x2d, oa, ob, wa, wb, *, tm=512):
    n = x2d.shape[0]
    ka = oa.shape[1]
    kb = ob.shape[1]
    return pl.pallas_call(
        _outproj_body,
        grid=(n // tm,),
        in_specs=[
            pl.BlockSpec((tm, D_MODEL), lambda i: (i, 0)),
            pl.BlockSpec((tm, ka), lambda i: (i, 0)),
            pl.BlockSpec((tm, kb), lambda i: (i, 0)),
            pl.BlockSpec((ka, D_MODEL), lambda i: (0, 0)),
            pl.BlockSpec((kb, D_MODEL), lambda i: (0, 0)),
        ],
        out_specs=pl.BlockSpec((tm, D_MODEL), lambda i: (i, 0)),
        out_shape=jax.ShapeDtypeStruct((n, D_MODEL), F32),
        compiler_params=pltpu.CompilerParams(
            dimension_semantics=("parallel",), vmem_limit_bytes=VMEM_LIMIT),
        name="outproj",
    )(x2d, oa, ob, wa, wb)


def _rel_bucket(dist):
    n = jnp.maximum(dist, 0)
    nf = jnp.maximum(n, 1).astype(F32)
    large = REL_MAX_EXACT + (jnp.log(nf / REL_MAX_EXACT) / math.log(REL_MAX_DIST / REL_MAX_EXACT)
                             * (REL_BUCKETS - REL_MAX_EXACT)).astype(jnp.int32)
    return jnp.where(n < REL_MAX_EXACT, n, jnp.minimum(large, REL_BUCKETS - 1))


def _bias_body(tab_ref, *refs):
    h = pl.program_id(0)
    n_tab = len(refs) // 2
    for bk_ref, out_ref in zip(refs[:n_tab], refs[n_tab:]):
        bk = bk_ref[...]
        acc = jnp.full(bk.shape, NEG, F32)
        for bucket in range(REL_BUCKETS):
            acc = jnp.where(bk == bucket, tab_ref[h, bucket], acc)
        out_ref[0] = acc


def _bias_tables(rel_bias, ncp, nq):
    tab = rel_bias.T.astype(F32) * LOG2E
    n_heads = tab.shape[0]
    c = jnp.arange(TQ)[:, None]
    r = jnp.arange(TQ)[None, :]
    d_own = r - c
    bk_own = jnp.where(d_own >= 0, _rel_bucket(d_own), -1)
    bk_prev = _rel_bucket(d_own + TQ)
    bk_win2 = jnp.where(d_own + 2 * TQ < WINDOW, _rel_bucket(d_own + 2 * TQ), -1)
    band_rows = ncp + (nq - 1) * (TQ // CMP_STRIDE)
    u = jnp.arange(band_rows)[:, None] - (nq - 1) * (TQ // CMP_STRIDE)
    d_cmp = r - CMP_STRIDE * u - (CMP_LEN - 1)
    bk_cmp = jnp.where(d_cmp >= 0, _rel_bucket(d_cmp), -1)
    tables = [t.astype(jnp.int32) for t in (jnp.concatenate([bk_win2, bk_prev, bk_own], axis=0), bk_cmp)]
    outs = pl.pallas_call(
        _bias_body,
        grid=(n_heads,),
        in_specs=[pl.BlockSpec(memory_space=pltpu.SMEM)]
        + [pl.BlockSpec(t.shape, lambda h: (0, 0)) for t in tables],
        out_specs=[pl.BlockSpec((1, t.shape[0], TQ), lambda h: (h // HG, 0, h % HG)) for t in tables],
        out_shape=[jax.ShapeDtypeStruct((n_heads // HG, t.shape[0], WIDE), F32) for t in tables],
        compiler_params=pltpu.CompilerParams(dimension_semantics=("parallel",), vmem_limit_bytes=VMEM_LIMIT),
        name="bias_tiles",
    )(tab, *tables)
    return outs, tab[:, REL_BUCKETS - 1]


def _pad_heads(w, n_heads):
    rows = w.shape[0]
    w = w.reshape(rows, n_heads, HEAD_DIM)
    w = jnp.pad(w, ((0, 0), (0, 0), (0, HEAD_PAD - HEAD_DIM)))
    return w.reshape(rows, n_heads * HEAD_PAD)


def _slc_overlap_t(ncp, nslc):
    ratio = SLC_BLOCK // CMP_STRIDE
    mc = CMP_LEN // CMP_STRIDE
    w = np.zeros((HEAD_PAD, nslc * ratio + mc), np.float32)
    for j in range(nslc):
        for a in range(ratio):
            for m in range(mc):
                w[j, ratio * j + a + m] += 1.0
    n_cmp = ncp - 1
    out = np.zeros((HEAD_PAD, ncp), np.float32)
    out[:, :n_cmp] = w[:, :n_cmp]
    return out


def _block_onehot(s, block):
    pat = np.zeros((s, HEAD_PAD), np.float32)
    pat[np.arange(s), EXT0 + np.arange(s) // block] = 1.0
    return pat


def kernel(x, norm_ffn1, w_ffn1_gate, w_ffn1_up, w_ffn1_down, norm_mix, w_in, cmp_pos_k, cmp_w1_k, cmp_w2_k,
           cmp_pos_v, cmp_w1_v, cmp_w2_v, w_out, norm_ffn2, w_ffn2_gate, w_ffn2_up, w_ffn2_down, rel_bias,
           norm_final):
    b, s, d = x.shape
    assert d == D_MODEL and s % (2 * TQ) == 0 and norm_ffn1.shape[0] == 1
    assert s // SLC_BLOCK <= HEAD_PAD - EXT0, "selection-mask rows must fit beside the 64 head dims"
    n = b * s
    nq = s // TQ
    ncp = s // CMP_STRIDE
    nslc = s // SLC_BLOCK
    scale = HEAD_DIM ** -0.5 * LOG2E

    sizes = (512, 512, 512, 512) + (128,) * 6 + (24,)
    splits = np.cumsum(sizes)[:-1].tolist()
    wqa, wka, wva, wqb, wkc, wvc, wks, wvs, wkw, wvw, wgb = jnp.split(w_in[0], splits, axis=1)
    w_k = jnp.concatenate([_pad_heads(wka, 8), _pad_heads(wkc, 2), _pad_heads(wvc, 2), _pad_heads(wks, 2),
                           _pad_heads(wkw, 2)], axis=1).astype(BF16)
    w_t = jnp.concatenate([_pad_heads(wqa * scale, 8), _pad_heads(wva, 8), _pad_heads(wqb * scale, 8),
                           _pad_heads(wvs, 2), _pad_heads(wvw, 2)], axis=1).T.astype(BF16)
    ones_np = np.zeros((N_TSLOTS * HEAD_PAD, 1), np.float32)
    for slot in list(range(TSLOT_VA, TSLOT_VA + 8)) + [TSLOT_VS, TSLOT_VS + 1, TSLOT_VW, TSLOT_VW + 1]:
        ones_np[slot * HEAD_PAD + ONES_ROW, 0] = 1.0
    ones_col = jnp.asarray(ones_np)
    wgt = wgb.reshape(D_MODEL, NSA_GROUPS, NSA_REP * N_GATES)
    wgt = jnp.pad(wgt, ((0, 0), (0, 0), (0, GATE_ROWS - NSA_REP * N_GATES)))
    wgt = wgt.reshape(D_MODEL, NSA_GROUPS * GATE_ROWS).T.astype(BF16)

    def cmp_params(pos, w1, w2):
        pos_p = jnp.pad(pos[0], ((0, 0), (0, HEAD_PAD - HEAD_DIM)))
        w1_p = jnp.pad(w1[0].reshape(CMP_LEN, HEAD_DIM, CMP_HIDDEN), ((0, 0), (0, HEAD_PAD - HEAD_DIM), (0, 0)))
        w2_p = jnp.pad(w2[0], ((0, 0), (0, HEAD_PAD - HEAD_DIM)))
        return pos_p, w1_p.astype(BF16), w2_p.astype(BF16)

    pos_k, w1k, w2k = cmp_params(cmp_pos_k, cmp_w1_k, cmp_w2_k)
    pos_v, w1v, w2v = cmp_params(cmp_pos_v, cmp_w1_v, cmp_w2_v)
    cmp_pos = jnp.stack([pos_k, pos_v])
    cmp_w1 = jnp.stack([w1k, w1v])

    wo = w_out[0]
    wo_a = _pad_heads(wo[:MOBA_HEADS * HEAD_DIM].T, MOBA_HEADS).T.astype(BF16)
    wo_b = _pad_heads(wo[MOBA_HEADS * HEAD_DIM:].T, NSA_HEADS).T.astype(BF16)

    (bias_win, bias_cmp), c31 = _bias_tables(rel_bias, ncp, nq)
    ovt = jnp.asarray(_slc_overlap_t(ncp, nslc)).astype(BF16)
    pat_moba = jnp.asarray(_block_onehot(s, MOBA_BLOCK)).astype(BF16)
    pat_slc = jnp.asarray(_block_onehot(s, SLC_BLOCK)).astype(BF16)

    x2d = x.reshape(n, D_MODEL)
    nf = norm_final.reshape(1, D_MODEL)
    x1 = _ffn(x2d, norm_ffn1, w_ffn1_gate[0].astype(BF16), w_ffn1_up[0].astype(BF16),
              w_ffn1_down[0].astype(BF16), nf, final_norm=False)
    pk, pt, gates_t = _inproj(x1, norm_mix, w_k, w_t, ones_col, wgt, b, s)
    pk = pk.reshape(b, s, N_KSLOTS * HEAD_PAD)
    kc, vct = _compress(pk, cmp_pos, cmp_w1, w2k, w2v.T)
    o_a = _moba(pk, pt, pat_moba, bias_win, c31)
    o_b = _nsa(pk, pt, kc, vct, gates_t, pat_slc, bias_win, bias_cmp, ovt, c31)
    x2 = _outproj(x1, o_a.reshape(n, -1), o_b.reshape(n, -1), wo_a, wo_b)
    y = _ffn(x2, norm_ffn2, w_ffn2_gate[0].astype(BF16), w_ffn2_up[0].astype(BF16),
             w_ffn2_down[0].astype(BF16), nf, final_norm=True)
    return y.reshape(b, s, D_MODEL)
```

```python
import functools
import math

import numpy as np
import jax
import jax.numpy as jnp
from jax import lax
from jax.experimental import pallas as pl
from jax.experimental.pallas import tpu as pltpu

D_MODEL = 1024
D_FF = 2816
HEAD_DIM = 64
HEAD_PAD = 128
ONES_ROW = HEAD_DIM
EXT0 = HEAD_DIM
MOBA_HEADS = 8
NSA_HEADS = 8
NSA_GROUPS = 2
NSA_REP = NSA_HEADS // NSA_GROUPS
HG = 4
MOBA_BLOCK = 256
MOBA_TOPK = 3
CMP_LEN = 32
CMP_STRIDE = 16
CMP_HIDDEN = 256
SLC_BLOCK = 64
SLC_TOPN = 16
WINDOW = 512
N_GATES = 3
GATE_ROWS = 16
REL_BUCKETS = 32
REL_MAX_EXACT = REL_BUCKETS // 2
REL_MAX_DIST = 128
RMS_EPS = 1e-6
TQ = 256
WIDE = HG * TQ
NEG = -1e30
M_INIT = -3e38
LOG2E = 1.4426950408889634
ACC_ROWS = 80
VMEM_LIMIT = 56 * 1024 * 1024

KSLOT_KA, KSLOT_KC, KSLOT_VC, KSLOT_KS, KSLOT_KW = 0, 8, 10, 12, 14
N_KSLOTS = 16
TSLOT_QA, TSLOT_VA, TSLOT_QB, TSLOT_VS, TSLOT_VW = 0, 8, 16, 24, 26
N_TSLOTS = 28

F32 = jnp.float32
BF16 = jnp.bfloat16


def _dot(a, b):
    return jnp.dot(a, b, preferred_element_type=F32)


def _dot_nt(a, b):
    return lax.dot_general(a, b, (((1,), (1,)), ((), ())), preferred_element_type=F32)


def _split_bf16(x):
    hi = x.astype(BF16)
    lo = (x - hi.astype(F32)).astype(BF16)
    return hi, lo


def _rmsnorm(x, g):
    ms = jnp.mean(x * x, axis=-1, keepdims=True)
    return x * lax.rsqrt(ms + RMS_EPS) * g


def _ffn_body(x_ref, g_ref, wg_ref, wu_ref, wd_ref, gf_ref, o_ref, h_ref, acc_ref, *, final_norm):
    j = pl.program_id(1)

    @pl.when(j == 0)
    def _():
        h_ref[...] = _rmsnorm(x_ref[...], g_ref[...]).astype(BF16)
        acc_ref[...] = jnp.zeros_like(acc_ref)

    h = h_ref[...]
    a = _dot(h, wg_ref[...])
    u = _dot(h, wu_ref[...])
    z = (a * jax.nn.sigmoid(a)) * u
    acc_ref[...] += _dot(z.astype(BF16), wd_ref[...])

    @pl.when(j == pl.num_programs(1) - 1)
    def _():
        y = x_ref[...] + 0.5 * acc_ref[...]
        if final_norm:
            y = _rmsnorm(y, gf_ref[...])
        o_ref[...] = y


def _ffn(x2d, g, wg, wu, wd, gf, *, final_norm, tm=512, tf=1408):
    n = x2d.shape[0]
    grid = (n // tm, D_FF // tf)
    return pl.pallas_call(
        functools.partial(_ffn_body, final_norm=final_norm),
        grid=grid,
        in_specs=[
            pl.BlockSpec((tm, D_MODEL), lambda i, j: (i, 0)),
            pl.BlockSpec((1, D_MODEL), lambda i, j: (0, 0)),
            pl.BlockSpec((D_MODEL, tf), lambda i, j: (0, j)),
            pl.BlockSpec((D_MODEL, tf), lambda i, j: (0, j)),
            pl.BlockSpec((tf, D_MODEL), lambda i, j: (j, 0)),
            pl.BlockSpec((1, D_MODEL), lambda i, j: (0, 0)),
        ],
        out_specs=pl.BlockSpec((tm, D_MODEL), lambda i, j: (i, 0)),
        out_shape=jax.ShapeDtypeStruct((n, D_MODEL), F32),
        scratch_shapes=[pltpu.VMEM((tm, D_MODEL), BF16), pltpu.VMEM((tm, D_MODEL), F32)],
        compiler_params=pltpu.CompilerParams(
            dimension_semantics=("parallel", "arbitrary"), vmem_limit_bytes=VMEM_LIMIT),
        name="ffn_final" if final_norm else "ffn",
    )(x2d, g, wg, wu, wd, gf)


def _inproj_body(x_ref, g_ref, wk_ref, wt_ref, ones_ref, wgt_ref, pat_ref, pk_ref, pt_ref, gates_ref):
    h = _rmsnorm(x_ref[...], g_ref[...]).astype(BF16)
    pk = _dot(h, wk_ref[...])
    for slot in range(N_KSLOTS):
        piece = pk[:, slot * HEAD_PAD:(slot + 1) * HEAD_PAD]
        if KSLOT_KA <= slot < KSLOT_KA + MOBA_HEADS:
            piece = piece + pat_ref[:, :HEAD_PAD]
        elif KSLOT_KS <= slot < KSLOT_KS + NSA_GROUPS:
            piece = piece + pat_ref[:, HEAD_PAD:]
        pk_ref[:, slot * HEAD_PAD:(slot + 1) * HEAD_PAD] = piece.astype(BF16)
    pt_ref[0] = (_dot_nt(wt_ref[...], h) + ones_ref[...]).astype(BF16)
    gates_ref[0] = jax.nn.sigmoid(_dot_nt(wgt_ref[...], h))


def _inproj(x2d, g, wk, wt, ones_col, wgt, pat, b, s, *, tm=256):
    n = x2d.shape[0]
    nk = wk.shape[1]
    nt = wt.shape[0]
    ng = wgt.shape[0]
    spt = s // tm
    const = lambda shape: pl.BlockSpec(shape, lambda i: (0, 0))
    return pl.pallas_call(
        _inproj_body,
        grid=(n // tm,),
        in_specs=[
            pl.BlockSpec((tm, D_MODEL), lambda i: (i, 0)),
            const((1, D_MODEL)), const((D_MODEL, nk)), const((nt, D_MODEL)), const((nt, 1)), const((ng, D_MODEL)),
            pl.BlockSpec((tm, 2 * HEAD_PAD), lambda i: (i % spt, 0)),
        ],
        out_specs=[
            pl.BlockSpec((tm, nk), lambda i: (i, 0)),
            pl.BlockSpec((1, nt, tm), lambda i: (i // spt, 0, i % spt)),
            pl.BlockSpec((1, ng, tm), lambda i: (i // spt, 0, i % spt)),
        ],
        out_shape=[jax.ShapeDtypeStruct((n, nk), BF16), jax.ShapeDtypeStruct((b, nt, s), BF16),
                   jax.ShapeDtypeStruct((b, ng, s), F32)],
        compiler_params=pltpu.CompilerParams(
            dimension_semantics=("parallel",), vmem_limit_bytes=VMEM_LIMIT),
        name="inproj",
    )(x2d, g, wk, wt, ones_col, wgt, pat)


def _compress_body(kraw_ref, vraw_ref, pos_ref, w1_ref, w2k_ref, w2vt_ref, kc_ref, vct_ref, xs_ref, *, ncp):
    half = CMP_LEN // 2
    for kv, raw_ref in enumerate((kraw_ref, vraw_ref)):
        xs_ref[...] = raw_ref[0].astype(F32)
        y0 = jnp.zeros((ncp, CMP_HIDDEN), F32)
        y1 = jnp.zeros((ncp, CMP_HIDDEN), F32)
        for p in range(half):
            xp = xs_ref[pl.ds(p, ncp, stride=CMP_STRIDE), :]
            y0 = y0 + _dot((xp + pos_ref[kv, p:p + 1, :]).astype(BF16), w1_ref[kv, p])
            y1 = y1 + _dot((xp + pos_ref[kv, half + p:half + p + 1, :]).astype(BF16), w1_ref[kv, half + p])
        h1 = y0 + pltpu.roll(y1, ncp - 1, 0)
        act = (h1 * jax.nn.sigmoid(h1)).astype(BF16)
        if kv == 0:
            kc_ref[0, 0] = _dot(act, w2k_ref[...]).astype(BF16)
        else:
            out_t = _dot_nt(w2vt_ref[...], act)
            row = lax.broadcasted_iota(jnp.int32, out_t.shape, 0)
            vct_ref[0, 0] = (out_t + jnp.where(row == ONES_ROW, 1.0, 0.0)).astype(BF16)


def _compress(pk, pos, w1, w2k, w2vt):
    b, s, _ = pk.shape
    ncp = s // CMP_STRIDE
    full = lambda shape: pl.BlockSpec(shape, lambda bi, g: (0,) * len(shape))
    return pl.pallas_call(
        functools.partial(_compress_body, ncp=ncp),
        grid=(b, NSA_GROUPS),
        in_specs=[
            pl.BlockSpec((1, s, HEAD_PAD), lambda bi, g: (bi, 0, KSLOT_KC + g)),
            pl.BlockSpec((1, s, HEAD_PAD), lambda bi, g: (bi, 0, KSLOT_VC + g)),
            full((2, CMP_LEN, HEAD_PAD)),
            full((2, CMP_LEN, HEAD_PAD, CMP_HIDDEN)),
            full((CMP_HIDDEN, HEAD_PAD)),
            full((HEAD_PAD, CMP_HIDDEN)),
        ],
        out_specs=[
            pl.BlockSpec((1, 1, ncp, HEAD_PAD), lambda bi, g: (bi, g, 0, 0)),
            pl.BlockSpec((1, 1, HEAD_PAD, ncp), lambda bi, g: (bi, g, 0, 0)),
        ],
        out_shape=[
            jax.ShapeDtypeStruct((b, NSA_GROUPS, ncp, HEAD_PAD), BF16),
            jax.ShapeDtypeStruct((b, NSA_GROUPS, HEAD_PAD, ncp), BF16),
        ],
        scratch_shapes=[pltpu.VMEM((s, HEAD_PAD), F32)],
        compiler_params=pltpu.CompilerParams(
            dimension_semantics=("parallel", "arbitrary"), vmem_limit_bytes=VMEM_LIMIT),
        name="compress",
    )(pk, pk, pos, w1, w2k, w2vt)


def _flash_update(s_t, shift, pv_fn, m_ref, acc_ref):
    m_prev = m_ref[...]
    m_new = jnp.maximum(m_prev, jnp.max(s_t, axis=0, keepdims=True) + shift)
    alpha = jnp.exp2(m_prev - m_new)
    p = jnp.exp2(s_t - (m_new - shift)).astype(BF16)
    acc_ref[...] = alpha * acc_ref[...] + pv_fn(p)
    m_ref[...] = m_new


def _causal_stream(i, scores_fn, pv_fn, bias_ref, shift_far, sa_ref, sb_ref, m_ref, acc_ref):
    zero = jnp.zeros((1, WIDE), F32)
    bias_prev = lambda: bias_ref[0, TQ:2 * TQ, :]
    bias_own = lambda: bias_ref[0, 2 * TQ:3 * TQ, :]
    upd = lambda s_t, shift, j: _flash_update(s_t, shift, pv_fn(j), m_ref, acc_ref)
    n_far = jnp.maximum(i - 1, 0)
    m_ref[...] = jnp.full(m_ref.shape, M_INIT, F32)
    acc_ref[...] = jnp.zeros_like(acc_ref)
    sa_ref[...] = scores_fn(0)

    def pair(p, carry):
        j = 2 * p
        sb_ref[...] = scores_fn(j + 1)
        upd(sa_ref[...], shift_far, j)
        sa_ref[...] = scores_fn(j + 2)
        upd(sb_ref[...], shift_far, j + 1)
        return carry

    lax.fori_loop(0, n_far // 2, pair, 0)
    odd = lax.rem(n_far, 2) == 1

    @pl.when(i == 0)
    def _():
        upd(sa_ref[...] + bias_own(), zero, 0)

    @pl.when(jnp.logical_and(i >= 1, jnp.logical_not(odd)))
    def _():
        sb_ref[...] = scores_fn(i)
        upd(sa_ref[...] + bias_prev(), zero, i - 1)
        upd(sb_ref[...] + bias_own(), zero, i)

    @pl.when(odd)
    def _():
        sb_ref[...] = scores_fn(i - 1)
        upd(sa_ref[...], shift_far, i - 2)
        sa_ref[...] = scores_fn(i)
        upd(sb_ref[...] + bias_prev(), zero, i - 1)
        upd(sa_ref[...] + bias_own(), zero, i)


def _state_lanes(i):
    return pl.ds(pl.multiple_of(i * WIDE, WIDE), WIDE)


def _tile_phase(count, tile_fn, scores_fn, step_fn, sa_ref, sb_ref):
    if count == 0:
        return
    tile = lambda n: tile_fn(jnp.minimum(jnp.asarray(n, jnp.int32), count - 1))
    sa_ref[...] = scores_fn(*tile(0))

    def pair(p, carry):
        n = 2 * p
        sb_ref[...] = scores_fn(*tile(n + 1))
        step_fn(sa_ref, *tile(n))
        sa_ref[...] = scores_fn(*tile(n + 2))
        step_fn(sb_ref, *tile(n + 1))
        return carry

    lax.fori_loop(0, count // 2, pair, 0)
    if count % 2:
        step_fn(sa_ref, *tile(count - 1))


def _flat_stream(nq, n_far, far_i_ref, far_j_ref, scores_fn, pv_fn, bias_ref, shift_far,
                 sa_ref, sb_ref, m_ref, acc_ref):
    zero = jnp.zeros((1, WIDE), F32)

    def own_step(s_ref, i, j):
        s_t = s_ref[...] + bias_ref[0, 2 * TQ:3 * TQ, :]
        m = jnp.max(s_t, axis=0, keepdims=True)
        m_ref[:, _state_lanes(i)] = m
        acc_ref[:, _state_lanes(i)] = pv_fn(j)(jnp.exp2(s_t - m).astype(BF16))

    def prev_step(s_ref, i, j):
        _flash_update(s_ref[...] + bias_ref[0, TQ:2 * TQ, :], zero, pv_fn(j),
                      m_ref.at[:, _state_lanes(i)], acc_ref.at[:, _state_lanes(i)])

    def far_step(s_ref, i, j):
        _flash_update(s_ref[...], shift_far, pv_fn(j), m_ref.at[:, _state_lanes(i)], acc_ref.at[:, _state_lanes(i)])

    _tile_phase(nq, lambda n: (n, n), scores_fn, own_step, sa_ref, sb_ref)
    _tile_phase(nq - 1, lambda n: (n + 1, n), scores_fn, prev_step, sa_ref, sb_ref)
    _tile_phase(n_far, lambda n: (far_i_ref[n], far_j_ref[n]), scores_fn, far_step, sa_ref, sb_ref)


def _normalize_padded(acc):
    out = acc / acc[ONES_ROW:ONES_ROW + 1, :]
    return jnp.concatenate([out, jnp.zeros((HEAD_PAD - ACC_ROWS, WIDE), F32)], axis=0)


def _lane_block(x, r):
    return x[:, r * TQ:(r + 1) * TQ]


def _shift_row(c31_ref, head0):
    return jnp.concatenate([jnp.full((1, TQ), c31_ref[head0 + r], F32) for r in range(HG)], axis=1)


def _rank_select(score, n_rows, top, row_weight=None):
    blk = lax.broadcasted_iota(jnp.int32, score.shape, 0)
    cnt = jnp.zeros(score.shape, F32)
    for j in range(n_rows):
        row = score[j:j + 1, :]
        ahead = jnp.where(row > score, 1.0, jnp.where((row == score) & (blk > j), 1.0, 0.0))
        cnt = cnt + (ahead if row_weight is None else ahead * row_weight(j))
    return cnt < top


def _moba_body(c31_ref, far_i_ref, far_j_ref, q_ref, k_ref, vt_ref, bias_ref, o_ref,
               kmean_ref, qx_ref, sa_ref, sb_ref, m_ref, acc_ref, *, nb, n_far):
    hg = pl.program_id(1)
    s = q_ref.shape[2]
    nbp = -(-nb // 16) * 16

    kmean_ref[...] = jnp.zeros_like(kmean_ref)
    blk = lax.broadcasted_iota(jnp.int32, (nbp, s), 0)
    cur = lax.broadcasted_iota(jnp.int32, (nbp, s), 1) // MOBA_BLOCK
    for r in range(HG):
        for n in range(nb):
            kb = k_ref[0, n * MOBA_BLOCK:(n + 1) * MOBA_BLOCK, r * HEAD_PAD:(r + 1) * HEAD_PAD].astype(F32)
            kmean_ref[r, n:n + 1, :] = jnp.mean(kb, axis=0, keepdims=True)
        q_t = q_ref[0, r * HEAD_PAD:(r + 1) * HEAD_PAD, :]
        kh, kl = _split_bf16(kmean_ref[r])
        gate = (_dot(kh, q_t) + _dot(kl, q_t))[:nbp]
        top = _rank_select(gate, nb, MOBA_TOPK, lambda n: jnp.where(n < cur, 1.0, 0.0))
        sel = ((blk < cur) & top) | (blk == cur)
        mask_rows = jnp.where(sel, 0.0, NEG).astype(BF16)
        pad_rows = jnp.zeros((HEAD_PAD - EXT0 - nbp, s), BF16)
        qx_ref[r] = jnp.concatenate([q_t[:EXT0], mask_rows, pad_rows], axis=0)

    def scores(i, j):
        qoff = pl.multiple_of(i * TQ, TQ)
        koff = pl.multiple_of(j * TQ, TQ)
        return jnp.concatenate(
            [_dot(k_ref[0, pl.ds(koff, TQ), r * HEAD_PAD:(r + 1) * HEAD_PAD], qx_ref[r, :, pl.ds(qoff, TQ)])
             for r in range(HG)], axis=1)

    def pv_fn(j):
        off = pl.multiple_of(j * TQ, TQ)

        def fn(p):
            return jnp.concatenate(
                [_dot(vt_ref[0, r * HEAD_PAD:r * HEAD_PAD + ACC_ROWS, pl.ds(off, TQ)], _lane_block(p, r))
                 for r in range(HG)], axis=1)
        return fn

    _flat_stream(nb, n_far, far_i_ref, far_j_ref, scores, pv_fn, bias_ref, _shift_row(c31_ref, hg * HG),
                 sa_ref, sb_ref, m_ref, acc_ref)

    def finish(i, carry):
        out = _normalize_padded(acc_ref[:, _state_lanes(i)])
        for r in range(HG):
            o_ref[0, pl.ds(pl.multiple_of(i * TQ, TQ), TQ), r * HEAD_PAD:(r + 1) * HEAD_PAD] = (
                jnp.transpose(_lane_block(out, r)).astype(BF16))
        return carry

    lax.fori_loop(0, nb, finish, 0)


def _moba(pk, pt, bias_win, c31, far_i, far_j):
    b, s, _ = pk.shape
    nb = s // MOBA_BLOCK
    hw = HG * HEAD_PAD
    smem = pl.BlockSpec(memory_space=pltpu.SMEM)
    return pl.pallas_call(
        functools.partial(_moba_body, nb=nb, n_far=far_i.shape[0]),
        grid=(b, MOBA_HEADS // HG),
        in_specs=[
            smem, smem, smem,
            pl.BlockSpec((1, hw, s), lambda bi, hg: (bi, TSLOT_QA // HG + hg, 0)),
            pl.BlockSpec((1, s, hw), lambda bi, hg: (bi, 0, KSLOT_KA // HG + hg)),
            pl.BlockSpec((1, hw, s), lambda bi, hg: (bi, TSLOT_VA // HG + hg, 0)),
            pl.BlockSpec((1, 3 * TQ, WIDE), lambda bi, hg: (hg, 0, 0)),
        ],
        out_specs=pl.BlockSpec((1, s, hw), lambda bi, hg: (bi, 0, hg)),
        out_shape=jax.ShapeDtypeStruct((b, s, MOBA_HEADS * HEAD_PAD), BF16),
        scratch_shapes=[
            pltpu.VMEM((HG, HEAD_PAD, HEAD_PAD), F32),
            pltpu.VMEM((HG, HEAD_PAD, s), BF16),
            pltpu.VMEM((TQ, WIDE), F32),
            pltpu.VMEM((TQ, WIDE), F32),
            pltpu.VMEM((1, nb * WIDE), F32),
            pltpu.VMEM((ACC_ROWS, nb * WIDE), F32),
        ],
        compiler_params=pltpu.CompilerParams(
            dimension_semantics=("parallel", "arbitrary"), vmem_limit_bytes=VMEM_LIMIT),
        name="moba",
    )(c31, far_i, far_j, pt, pk, pt, bias_win)


def _nsa_body(c31_ref, q_ref, ks_ref, kw_ref, vst_ref, vwt_ref, kc_ref, vct_ref, gates_ref,
              bias_ref, bcmp_ref, ovt_ref, o_ref,
              qx_ref, sa_ref, sb_ref, ms_ref, accs_ref, accw_ref, *, ncp, nslc, nq):
    g = pl.program_id(1)
    i = pl.program_id(2)
    q4 = jnp.concatenate([q_ref[0, r * HEAD_PAD:(r + 1) * HEAD_PAD, :] for r in range(HG)], axis=1)

    band_off = pl.multiple_of((nq - 1 - i) * (TQ // CMP_STRIDE), TQ // CMP_STRIDE)
    s_t = _dot(kc_ref[0, 0], q4) + bcmp_ref[0, pl.ds(band_off, ncp), :]
    m = jnp.max(s_t, axis=0, keepdims=True)
    p = jnp.where(s_t > 0.5 * NEG, jnp.exp2(s_t - m), 0.0)
    den = jnp.sum(p, axis=0, keepdims=True)
    pn = p / jnp.where(den > 0.0, den, 1.0)
    o_cmp = _dot(vct_ref[0, 0, :ACC_ROWS, :], pn.astype(BF16))
    psum = _lane_block(pn, 0)
    for r in range(1, HG):
        psum = psum + _lane_block(pn, r)

    ph, plo = _split_bf16(psum)
    imp = (_dot(ovt_ref[...], ph) + _dot(ovt_ref[...], plo))[:EXT0]
    blk = lax.broadcasted_iota(jnp.int32, imp.shape, 0)
    cur = (i * TQ + lax.broadcasted_iota(jnp.int32, imp.shape, 1)) // SLC_BLOCK
    forced = (blk == 0) | (blk == cur) | (blk == cur - 1)
    score = jnp.where(forced, 1e30, jnp.where(blk <= cur, imp, -1.0))
    score = jnp.where(blk < nslc, score, -2.0)
    sel = _rank_select(score, nslc, SLC_TOPN) & (blk < nslc)
    mask_rows = jnp.where(sel, 0.0, NEG).astype(BF16)
    qx_ref[...] = jnp.concatenate([q4[:EXT0], jnp.concatenate([mask_rows] * HG, axis=1)], axis=0)

    def scores(j):
        return _dot(ks_ref[0, pl.ds(pl.multiple_of(j * TQ, TQ), TQ), :], qx_ref[...])

    def pv_fn(j):
        off = pl.multiple_of(j * TQ, TQ)
        return lambda p: _dot(vst_ref[0, :ACC_ROWS, pl.ds(off, TQ)], p)

    _causal_stream(i, scores, pv_fn, bias_ref, _shift_row(c31_ref, MOBA_HEADS + g * HG),
                   sa_ref, sb_ref, ms_ref, accs_ref)

    def win_pass(n_tiles):
        start = pl.multiple_of((i + 1 - n_tiles) * TQ, TQ)
        s_w = _dot(kw_ref[0, pl.ds(start, n_tiles * TQ), :], q4) + bias_ref[0, (3 - n_tiles) * TQ:, :]
        p_w = jnp.exp2((s_w - jnp.max(s_w, axis=0, keepdims=True)).astype(BF16))
        accw_ref[...] = _dot(vwt_ref[0, :ACC_ROWS, pl.ds(start, n_tiles * TQ)], p_w)

    pl.when(i == 0)(functools.partial(win_pass, 1))
    pl.when(i == 1)(functools.partial(win_pass, 2))
    pl.when(i >= 2)(functools.partial(win_pass, 3))

    o_slc = accs_ref[...] / accs_ref[ONES_ROW:ONES_ROW + 1, :]
    o_win = accw_ref[...] / accw_ref[ONES_ROW:ONES_ROW + 1, :]
    gt = gates_ref[0]
    zpad = jnp.zeros((HEAD_PAD - ACC_ROWS, TQ), F32)
    for r in range(HG):
        row = N_GATES * r
        out = (gt[row:row + 1, :] * _lane_block(o_cmp, r) + gt[row + 1:row + 2, :] * _lane_block(o_slc, r)
               + gt[row + 2:row + 3, :] * _lane_block(o_win, r))
        o_ref[0, :, r * HEAD_PAD:(r + 1) * HEAD_PAD] = jnp.transpose(
            jnp.concatenate([out, zpad], axis=0)).astype(BF16)


def _nsa(pk, pt, kc, vct, gates_t, bias_win, bias_cmp, ovt, c31):
    b, s, _ = pk.shape
    nq = s // TQ
    ncp = s // CMP_STRIDE
    nslc = s // SLC_BLOCK
    hw = HG * HEAD_PAD
    band_rows = bias_cmp.shape[1]
    hgrp = MOBA_HEADS // HG
    k_spec = lambda slot: pl.BlockSpec((1, s, HEAD_PAD), lambda bi, g, i: (bi, 0, slot + g))
    vt_spec = lambda slot: pl.BlockSpec((1, HEAD_PAD, s), lambda bi, g, i: (bi, slot + g, 0))
    bias_spec = lambda rows: pl.BlockSpec((1, rows, WIDE), lambda bi, g, i: (hgrp + g, 0, 0))
    return pl.pallas_call(
        functools.partial(_nsa_body, ncp=ncp, nslc=nslc, nq=nq),
        grid=(b, NSA_GROUPS, nq),
        in_specs=[
            pl.BlockSpec(memory_space=pltpu.SMEM),
            pl.BlockSpec((1, hw, TQ), lambda bi, g, i: (bi, TSLOT_QB // HG + g, i)),
            k_spec(KSLOT_KS), k_spec(KSLOT_KW), vt_spec(TSLOT_VS), vt_spec(TSLOT_VW),
            pl.BlockSpec((1, 1, ncp, HEAD_PAD), lambda bi, g, i: (bi, g, 0, 0)),
            pl.BlockSpec((1, 1, HEAD_PAD, ncp), lambda bi, g, i: (bi, g, 0, 0)),
            pl.BlockSpec((1, GATE_ROWS, TQ), lambda bi, g, i: (bi, g, i)),
            bias_spec(3 * TQ), bias_spec(band_rows),
            pl.BlockSpec((HEAD_PAD, ncp), lambda bi, g, i: (0, 0)),
        ],
        out_specs=pl.BlockSpec((1, TQ, hw), lambda bi, g, i: (bi, i, g)),
        out_shape=jax.ShapeDtypeStruct((b, s, NSA_HEADS * HEAD_PAD), BF16),
        scratch_shapes=[
            pltpu.VMEM((HEAD_PAD, WIDE), BF16),
            pltpu.VMEM((TQ, WIDE), F32),
            pltpu.VMEM((TQ, WIDE), F32),
            pltpu.VMEM((1, WIDE), F32),
            pltpu.VMEM((ACC_ROWS, WIDE), F32),
            pltpu.VMEM((ACC_ROWS, WIDE), F32),
        ],
        compiler_params=pltpu.CompilerParams(
            dimension_semantics=("parallel", "parallel", "arbitrary"), vmem_limit_bytes=VMEM_LIMIT),
        name="nsa",
    )(c31, pt, pk, pk, pt, pt, kc, vct, gates_t, bias_win, bias_cmp, ovt)


def _outproj_body(x_ref, oa_ref, ob_ref, wa_ref, wb_ref, o_ref):
    o_ref[...] = x_ref[...] + _dot(oa_ref[...], wa_ref[...]) + _dot(ob_ref[...], wb_ref[...])


def _outproj(x2d, oa, ob, wa, wb, *, tm=512):
    n = x2d.shape[0]
    ka = oa.shape[1]
    kb = ob.shape[1]
    return pl.pallas_call(
        _outproj_body,
        grid=(n // tm,),
        in_specs=[
            pl.BlockSpec((tm, D_MODEL), lambda i: (i, 0)),
            pl.BlockSpec((tm, ka), lambda i: (i, 0)),
            pl.BlockSpec((tm, kb), lambda i: (i, 0)),
            pl.BlockSpec((ka, D_MODEL), lambda i: (0, 0)),
            pl.BlockSpec((kb, D_MODEL), lambda i: (0, 0)),
        ],
        out_specs=pl.BlockSpec((tm, D_MODEL), lambda i: (i, 0)),
        out_shape=jax.ShapeDtypeStruct((n, D_MODEL), F32),
        compiler_params=pltpu.CompilerParams(
            dimension_semantics=("parallel",), vmem_limit_bytes=VMEM_LIMIT),
        name="outproj",
    )(x2d, oa, ob, wa, wb)


def _rel_bucket(dist):
    n = jnp.maximum(dist, 0)
    nf = jnp.maximum(n, 1).astype(F32)
    large = REL_MAX_EXACT + (jnp.log(nf / REL_MAX_EXACT) / math.log(REL_MAX_DIST / REL_MAX_EXACT)
                             * (REL_BUCKETS - REL_MAX_EXACT)).astype(jnp.int32)
    return jnp.where(n < REL_MAX_EXACT, n, jnp.minimum(large, REL_BUCKETS - 1))


def _bias_body(tab_ref, *refs):
    h = pl.program_id(0)
    n_tab = len(refs) // 2
    for bk_ref, out_ref in zip(refs[:n_tab], refs[n_tab:]):
        bk = bk_ref[...]
        acc = jnp.full(bk.shape, NEG, F32)
        for bucket in range(REL_BUCKETS):
            acc = jnp.where(bk == bucket, tab_ref[h, bucket], acc)
        out_ref[0] = acc


def _bias_tables(rel_bias, ncp, nq):
    tab = rel_bias.T.astype(F32) * LOG2E
    n_heads = tab.shape[0]
    c = jnp.arange(TQ)[:, None]
    r = jnp.arange(TQ)[None, :]
    d_own = r - c
    bk_own = jnp.where(d_own >= 0, _rel_bucket(d_own), -1)
    bk_prev = _rel_bucket(d_own + TQ)
    bk_win2 = jnp.where(d_own + 2 * TQ < WINDOW, _rel_bucket(d_own + 2 * TQ), -1)
    band_rows = ncp + (nq - 1) * (TQ // CMP_STRIDE)
    u = jnp.arange(band_rows)[:, None] - (nq - 1) * (TQ // CMP_STRIDE)
    d_cmp = r - CMP_STRIDE * u - (CMP_LEN - 1)
    bk_cmp = jnp.where(d_cmp >= 0, _rel_bucket(d_cmp), -1)
    tables = [t.astype(jnp.int32) for t in (jnp.concatenate([bk_win2, bk_prev, bk_own], axis=0), bk_cmp)]
    outs = pl.pallas_call(
        _bias_body,
        grid=(n_heads,),
        in_specs=[pl.BlockSpec(memory_space=pltpu.SMEM)]
        + [pl.BlockSpec(t.shape, lambda h: (0, 0)) for t in tables],
        out_specs=[pl.BlockSpec((1, t.shape[0], TQ), lambda h: (h // HG, 0, h % HG)) for t in tables],
        out_shape=[jax.ShapeDtypeStruct((n_heads // HG, t.shape[0], WIDE), F32) for t in tables],
        compiler_params=pltpu.CompilerParams(dimension_semantics=("parallel",), vmem_limit_bytes=VMEM_LIMIT),
        name="bias_tiles",
    )(tab, *tables)
    return outs, tab[:, REL_BUCKETS - 1]


def _pad_heads(w, n_heads):
    rows = w.shape[0]
    w = w.reshape(rows, n_heads, HEAD_DIM)
    w = jnp.pad(w, ((0, 0), (0, 0), (0, HEAD_PAD - HEAD_DIM)))
    return w.reshape(rows, n_heads * HEAD_PAD)


def _slc_overlap_t(ncp, nslc):
    ratio = SLC_BLOCK // CMP_STRIDE
    mc = CMP_LEN // CMP_STRIDE
    w = np.zeros((HEAD_PAD, nslc * ratio + mc), np.float32)
    for j in range(nslc):
        for a in range(ratio):
            for m in range(mc):
                w[j, ratio * j + a + m] += 1.0
    n_cmp = ncp - 1
    out = np.zeros((HEAD_PAD, ncp), np.float32)
    out[:, :n_cmp] = w[:, :n_cmp]
    return out


def _block_onehot(s, block):
    pat = np.zeros((s, HEAD_PAD), np.float32)
    pat[np.arange(s), EXT0 + np.arange(s) // block] = 1.0
    return pat


def kernel(x, norm_ffn1, w_ffn1_gate, w_ffn1_up, w_ffn1_down, norm_mix, w_in, cmp_pos_k, cmp_w1_k, cmp_w2_k,
           cmp_pos_v, cmp_w1_v, cmp_w2_v, w_out, norm_ffn2, w_ffn2_gate, w_ffn2_up, w_ffn2_down, rel_bias,
           norm_final):
    b, s, d = x.shape
    assert d == D_MODEL and s % (2 * TQ) == 0 and norm_ffn1.shape[0] == 1
    assert s // SLC_BLOCK <= HEAD_PAD - EXT0, "selection-mask rows must fit beside the 64 head dims"
    n = b * s
    nq = s // TQ
    ncp = s // CMP_STRIDE
    nslc = s // SLC_BLOCK
    scale = HEAD_DIM ** -0.5 * LOG2E

    sizes = (512, 512, 512, 512) + (128,) * 6 + (24,)
    splits = np.cumsum(sizes)[:-1].tolist()
    wqa, wka, wva, wqb, wkc, wvc, wks, wvs, wkw, wvw, wgb = jnp.split(w_in[0], splits, axis=1)
    w_k = jnp.concatenate([_pad_heads(wka, 8), _pad_heads(wkc, 2), _pad_heads(wvc, 2), _pad_heads(wks, 2),
                           _pad_heads(wkw, 2)], axis=1).astype(BF16)
    w_t = jnp.concatenate([_pad_heads(wqa * scale, 8), _pad_heads(wva, 8), _pad_heads(wqb * scale, 8),
                           _pad_heads(wvs, 2), _pad_heads(wvw, 2)], axis=1).T.astype(BF16)
    ones_np = np.zeros((N_TSLOTS * HEAD_PAD, 1), np.float32)
    for slot in list(range(TSLOT_VA, TSLOT_VA + 8)) + [TSLOT_VS, TSLOT_VS + 1, TSLOT_VW, TSLOT_VW + 1]:
        ones_np[slot * HEAD_PAD + ONES_ROW, 0] = 1.0
    ones_col = jnp.asarray(ones_np)
    wgt = wgb.reshape(D_MODEL, NSA_GROUPS, NSA_REP * N_GATES)
    wgt = jnp.pad(wgt, ((0, 0), (0, 0), (0, GATE_ROWS - NSA_REP * N_GATES)))
    wgt = wgt.reshape(D_MODEL, NSA_GROUPS * GATE_ROWS).T.astype(BF16)

    def cmp_params(pos, w1, w2):
        pos_p = jnp.pad(pos[0], ((0, 0), (0, HEAD_PAD - HEAD_DIM)))
        w1_p = jnp.pad(w1[0].reshape(CMP_LEN, HEAD_DIM, CMP_HIDDEN), ((0, 0), (0, HEAD_PAD - HEAD_DIM), (0, 0)))
        w2_p = jnp.pad(w2[0], ((0, 0), (0, HEAD_PAD - HEAD_DIM)))
        return pos_p, w1_p.astype(BF16), w2_p.astype(BF16)

    pos_k, w1k, w2k = cmp_params(cmp_pos_k, cmp_w1_k, cmp_w2_k)
    pos_v, w1v, w2v = cmp_params(cmp_pos_v, cmp_w1_v, cmp_w2_v)
    cmp_pos = jnp.stack([pos_k, pos_v])
    cmp_w1 = jnp.stack([w1k, w1v])

    wo = w_out[0]
    wo_a = _pad_heads(wo[:MOBA_HEADS * HEAD_DIM].T, MOBA_HEADS).T.astype(BF16)
    wo_b = _pad_heads(wo[MOBA_HEADS * HEAD_DIM:].T, NSA_HEADS).T.astype(BF16)

    (bias_win, bias_cmp), c31 = _bias_tables(rel_bias, ncp, nq)
    ovt = jnp.asarray(_slc_overlap_t(ncp, nslc)).astype(BF16)
    pat = jnp.asarray(np.concatenate([_block_onehot(s, MOBA_BLOCK), _block_onehot(s, SLC_BLOCK)], axis=1))
    far = [(i, j) for j in range(nq) for i in range(j + 2, nq)]
    far_i = jnp.asarray(np.array([i for i, _ in far], np.int32))
    far_j = jnp.asarray(np.array([j for _, j in far], np.int32))

    x2d = x.reshape(n, D_MODEL)
    nf = norm_final.reshape(1, D_MODEL)
    x1 = _ffn(x2d, norm_ffn1, w_ffn1_gate[0].astype(BF16), w_ffn1_up[0].astype(BF16),
              w_ffn1_down[0].astype(BF16), nf, final_norm=False)
    pk, pt, gates_t = _inproj(x1, norm_mix, w_k, w_t, ones_col, wgt, pat, b, s)
    pk = pk.reshape(b, s, N_KSLOTS * HEAD_PAD)
    kc, vct = _compress(pk, cmp_pos, cmp_w1, w2k, w2v.T)
    o_a = _moba(pk, pt, bias_win, c31, far_i, far_j)
    o_b = _nsa(pk, pt, kc, vct, gates_t, bias_win, bias_cmp, ovt, c31)
    x2 = _outproj(x1, o_a.reshape(n, -1), o_b.reshape(n, -1), wo_a, wo_b)
    y = _ffn(x2, norm_ffn2, w_ffn2_gate[0].astype(BF16), w_ffn2_up[0].astype(BF16),
             w_ffn2_down[0].astype(BF16), nf, final_norm=True)
    return y.reshape(b, s, D_MODEL)
```

```python
import functools
import math

import numpy as np
import jax
import jax.numpy as jnp
from jax import lax
from jax.experimental import pallas as pl
from jax.experimental.pallas import tpu as pltpu

D_MODEL = 1024
D_FF = 2816
HEAD_DIM = 64
HEAD_PAD = 128
ONES_ROW = HEAD_DIM
EXT0 = HEAD_DIM
MOBA_HEADS = 8
NSA_HEADS = 8
NSA_GROUPS = 2
NSA_REP = NSA_HEADS // NSA_GROUPS
HG = 4
MOBA_BLOCK = 256
MOBA_TOPK = 3
CMP_LEN = 32
CMP_STRIDE = 16
CMP_HIDDEN = 256
SLC_BLOCK = 64
SLC_TOPN = 16
WINDOW = 512
N_GATES = 3
GATE_ROWS = 16
REL_BUCKETS = 32
REL_MAX_EXACT = REL_BUCKETS // 2
REL_MAX_DIST = 128
RMS_EPS = 1e-6
TQ = 256
WIDE = HG * TQ
NEG = -1e30
LOG2E = 1.4426950408889634
ACC_ROWS = 80
VMEM_LIMIT = 56 * 1024 * 1024

KSLOT_KA, KSLOT_KC, KSLOT_VC, KSLOT_KS, KSLOT_KW = 0, 8, 10, 12, 14
N_KSLOTS = 16
TSLOT_QA, TSLOT_VA, TSLOT_QB, TSLOT_VS, TSLOT_VW = 0, 8, 16, 24, 26
N_TSLOTS = 28

F32 = jnp.float32
BF16 = jnp.bfloat16


def _dot(a, b):
    return jnp.dot(a, b, preferred_element_type=F32)


def _dot_nt(a, b):
    return lax.dot_general(a, b, (((1,), (1,)), ((), ())), preferred_element_type=F32)


def _split_bf16(x):
    hi = x.astype(BF16)
    lo = (x - hi.astype(F32)).astype(BF16)
    return hi, lo


def _rmsnorm(x, g):
    ms = jnp.mean(x * x, axis=-1, keepdims=True)
    return x * lax.rsqrt(ms + RMS_EPS) * g


def _ffn_body(x_ref, g_ref, wg_ref, wu_ref, wd_ref, gf_ref, o_ref, h_ref, acc_ref, *, final_norm):
    j = pl.program_id(1)

    @pl.when(j == 0)
    def _():
        h_ref[...] = _rmsnorm(x_ref[...], g_ref[...]).astype(BF16)
        acc_ref[...] = jnp.zeros_like(acc_ref)

    h = h_ref[...]
    a = _dot(h, wg_ref[...])
    u = _dot(h, wu_ref[...])
    z = (a * jax.nn.sigmoid(a)) * u
    acc_ref[...] += _dot(z.astype(BF16), wd_ref[...])

    @pl.when(j == pl.num_programs(1) - 1)
    def _():
        y = x_ref[...] + 0.5 * acc_ref[...]
        if final_norm:
            y = _rmsnorm(y, gf_ref[...])
        o_ref[...] = y


def _ffn(x2d, g, wg, wu, wd, gf, *, final_norm, tm=512, tf=1408):
    n = x2d.shape[0]
    grid = (n // tm, D_FF // tf)
    return pl.pallas_call(
        functools.partial(_ffn_body, final_norm=final_norm),
        grid=grid,
        in_specs=[
            pl.BlockSpec((tm, D_MODEL), lambda i, j: (i, 0)),
            pl.BlockSpec((1, D_MODEL), lambda i, j: (0, 0)),
            pl.BlockSpec((D_MODEL, tf), lambda i, j: (0, j)),
            pl.BlockSpec((D_MODEL, tf), lambda i, j: (0, j)),
            pl.BlockSpec((tf, D_MODEL), lambda i, j: (j, 0)),
            pl.BlockSpec((1, D_MODEL), lambda i, j: (0, 0)),
        ],
        out_specs=pl.BlockSpec((tm, D_MODEL), lambda i, j: (i, 0)),
        out_shape=jax.ShapeDtypeStruct((n, D_MODEL), F32),
        scratch_shapes=[pltpu.VMEM((tm, D_MODEL), BF16), pltpu.VMEM((tm, D_MODEL), F32)],
        compiler_params=pltpu.CompilerParams(
            dimension_semantics=("parallel", "arbitrary"), vmem_limit_bytes=VMEM_LIMIT),
        name="ffn_final" if final_norm else "ffn",
    )(x2d, g, wg, wu, wd, gf)


def _inproj_body(x_ref, g_ref, wk_ref, wt_ref, ones_ref, wgt_ref, pat_ref, pk_ref, pt_ref, gates_ref):
    h = _rmsnorm(x_ref[...], g_ref[...]).astype(BF16)
    pk = _dot(h, wk_ref[...])
    for slot in range(N_KSLOTS):
        piece = pk[:, slot * HEAD_PAD:(slot + 1) * HEAD_PAD]
        if KSLOT_KA <= slot < KSLOT_KA + MOBA_HEADS:
            piece = piece + pat_ref[:, :HEAD_PAD]
        elif KSLOT_KS <= slot < KSLOT_KS + NSA_GROUPS:
            piece = piece + pat_ref[:, HEAD_PAD:]
        pk_ref[:, slot * HEAD_PAD:(slot + 1) * HEAD_PAD] = piece.astype(BF16)
    pt_ref[0] = (_dot_nt(wt_ref[...], h) + ones_ref[...]).astype(BF16)
    gates_ref[0] = jax.nn.sigmoid(_dot_nt(wgt_ref[...], h))


def _inproj(x2d, g, wk, wt, ones_col, wgt, pat, b, s, *, tm=256):
    n = x2d.shape[0]
    nk = wk.shape[1]
    nt = wt.shape[0]
    ng = wgt.shape[0]
    spt = s // tm
    const = lambda shape: pl.BlockSpec(shape, lambda i: (0, 0))
    return pl.pallas_call(
        _inproj_body,
        grid=(n // tm,),
        in_specs=[
            pl.BlockSpec((tm, D_MODEL), lambda i: (i, 0)),
            const((1, D_MODEL)), const((D_MODEL, nk)), const((nt, D_MODEL)), const((nt, 1)), const((ng, D_MODEL)),
            pl.BlockSpec((tm, 2 * HEAD_PAD), lambda i: (i % spt, 0)),
        ],
        out_specs=[
            pl.BlockSpec((tm, nk), lambda i: (i, 0)),
            pl.BlockSpec((1, nt, tm), lambda i: (i // spt, 0, i % spt)),
            pl.BlockSpec((1, ng, tm), lambda i: (i // spt, 0, i % spt)),
        ],
        out_shape=[jax.ShapeDtypeStruct((n, nk), BF16), jax.ShapeDtypeStruct((b, nt, s), BF16),
                   jax.ShapeDtypeStruct((b, ng, s), F32)],
        compiler_params=pltpu.CompilerParams(
            dimension_semantics=("parallel",), vmem_limit_bytes=VMEM_LIMIT),
        name="inproj",
    )(x2d, g, wk, wt, ones_col, wgt, pat)


def _compress_body(kraw_ref, vraw_ref, pos_ref, w1_ref, w2k_ref, w2vt_ref, kc_ref, vct_ref, xs_ref, *, ncp):
    half = CMP_LEN // 2
    for kv, raw_ref in enumerate((kraw_ref, vraw_ref)):
        xs_ref[...] = raw_ref[0].astype(F32)
        y0 = jnp.zeros((ncp, CMP_HIDDEN), F32)
        y1 = jnp.zeros((ncp, CMP_HIDDEN), F32)
        for p in range(half):
            xp = xs_ref[pl.ds(p, ncp, stride=CMP_STRIDE), :]
            y0 = y0 + _dot((xp + pos_ref[kv, p:p + 1, :]).astype(BF16), w1_ref[kv, p])
            y1 = y1 + _dot((xp + pos_ref[kv, half + p:half + p + 1, :]).astype(BF16), w1_ref[kv, half + p])
        h1 = y0 + pltpu.roll(y1, ncp - 1, 0)
        act = (h1 * jax.nn.sigmoid(h1)).astype(BF16)
        if kv == 0:
            kc_ref[0, 0] = _dot(act, w2k_ref[...]).astype(BF16)
        else:
            out_t = _dot_nt(w2vt_ref[...], act)
            row = lax.broadcasted_iota(jnp.int32, out_t.shape, 0)
            vct_ref[0, 0] = (out_t + jnp.where(row == ONES_ROW, 1.0, 0.0)).astype(BF16)


def _compress(pk, pos, w1, w2k, w2vt):
    b, s, _ = pk.shape
    ncp = s // CMP_STRIDE
    full = lambda shape: pl.BlockSpec(shape, lambda bi, g: (0,) * len(shape))
    return pl.pallas_call(
        functools.partial(_compress_body, ncp=ncp),
        grid=(b, NSA_GROUPS),
        in_specs=[
            pl.BlockSpec((1, s, HEAD_PAD), lambda bi, g: (bi, 0, KSLOT_KC + g)),
            pl.BlockSpec((1, s, HEAD_PAD), lambda bi, g: (bi, 0, KSLOT_VC + g)),
            full((2, CMP_LEN, HEAD_PAD)),
            full((2, CMP_LEN, HEAD_PAD, CMP_HIDDEN)),
            full((CMP_HIDDEN, HEAD_PAD)),
            full((HEAD_PAD, CMP_HIDDEN)),
        ],
        out_specs=[
            pl.BlockSpec((1, 1, ncp, HEAD_PAD), lambda bi, g: (bi, g, 0, 0)),
            pl.BlockSpec((1, 1, HEAD_PAD, ncp), lambda bi, g: (bi, g, 0, 0)),
        ],
        out_shape=[
            jax.ShapeDtypeStruct((b, NSA_GROUPS, ncp, HEAD_PAD), BF16),
            jax.ShapeDtypeStruct((b, NSA_GROUPS, HEAD_PAD, ncp), BF16),
        ],
        scratch_shapes=[pltpu.VMEM((s, HEAD_PAD), F32)],
        compiler_params=pltpu.CompilerParams(
            dimension_semantics=("parallel", "arbitrary"), vmem_limit_bytes=VMEM_LIMIT),
        name="compress",
    )(pk, pk, pos, w1, w2k, w2vt)


def _flash_update(s_t, shift, pv_fn, m_ref, acc_ref):
    m_prev = m_ref[...]
    m_new = jnp.maximum(m_prev, jnp.max(s_t, axis=0, keepdims=True) + shift)
    alpha = jnp.exp2(m_prev - m_new)
    p = jnp.exp2(s_t - (m_new - shift)).astype(BF16)
    acc_ref[...] = alpha * acc_ref[...] + pv_fn(p)
    m_ref[...] = m_new


def _state_lanes(i):
    return pl.ds(pl.multiple_of(i * WIDE, WIDE), WIDE)


def _tile_phase(count, tile_fn, scores_fn, step_fn, sa_ref, sb_ref):
    if count == 0:
        return
    tile = lambda n: tile_fn(jnp.minimum(jnp.asarray(n, jnp.int32), count - 1))
    sa_ref[...] = scores_fn(*tile(0))

    def pair(p, carry):
        n = 2 * p
        sb_ref[...] = scores_fn(*tile(n + 1))
        step_fn(sa_ref, *tile(n))
        sa_ref[...] = scores_fn(*tile(n + 2))
        step_fn(sb_ref, *tile(n + 1))
        return carry

    lax.fori_loop(0, count // 2, pair, 0)
    if count % 2:
        step_fn(sa_ref, *tile(count - 1))


def _flat_stream(nq, n_far, far_i_ref, far_j_ref, scores_fn, pv_fn, bias_ref, shift_far,
                 sa_ref, sb_ref, m_ref, acc_ref):
    zero = jnp.zeros((1, WIDE), F32)

    def own_step(s_ref, i, j):
        s_t = s_ref[...] + bias_ref[0, 2 * TQ:3 * TQ, :]
        m = jnp.max(s_t, axis=0, keepdims=True)
        m_ref[:, _state_lanes(i)] = m
        acc_ref[:, _state_lanes(i)] = pv_fn(j)(jnp.exp2(s_t - m).astype(BF16))

    def prev_step(s_ref, i, j):
        _flash_update(s_ref[...] + bias_ref[0, TQ:2 * TQ, :], zero, pv_fn(j),
                      m_ref.at[:, _state_lanes(i)], acc_ref.at[:, _state_lanes(i)])

    def far_step(s_ref, i, j):
        _flash_update(s_ref[...], shift_far, pv_fn(j), m_ref.at[:, _state_lanes(i)], acc_ref.at[:, _state_lanes(i)])

    _tile_phase(nq, lambda n: (n, n), scores_fn, own_step, sa_ref, sb_ref)
    _tile_phase(nq - 1, lambda n: (n + 1, n), scores_fn, prev_step, sa_ref, sb_ref)
    _tile_phase(n_far, lambda n: (far_i_ref[n], far_j_ref[n]), scores_fn, far_step, sa_ref, sb_ref)


def _normalize_padded(acc):
    out = acc / acc[ONES_ROW:ONES_ROW + 1, :]
    return jnp.concatenate([out, jnp.zeros((HEAD_PAD - ACC_ROWS, WIDE), F32)], axis=0)


def _lane_block(x, r):
    return x[:, r * TQ:(r + 1) * TQ]


def _shift_row(c31_ref, head0):
    return jnp.concatenate([jnp.full((1, TQ), c31_ref[head0 + r], F32) for r in range(HG)], axis=1)


def _rank_select(score, n_rows, top, row_weight=None):
    blk = lax.broadcasted_iota(jnp.int32, score.shape, 0)
    cnt = jnp.zeros(score.shape, F32)
    for j in range(n_rows):
        row = score[j:j + 1, :]
        ahead = jnp.where(row > score, 1.0, jnp.where((row == score) & (blk > j), 1.0, 0.0))
        cnt = cnt + (ahead if row_weight is None else ahead * row_weight(j))
    return cnt < top


def _moba_body(c31_ref, far_i_ref, far_j_ref, q_ref, k_ref, vt_ref, bias_ref, o_ref,
               kmean_ref, qx_ref, sa_ref, sb_ref, m_ref, acc_ref, *, nb, n_far):
    hg = pl.program_id(1)
    s = q_ref.shape[2]
    nbp = -(-nb // 16) * 16

    kmean_ref[...] = jnp.zeros_like(kmean_ref)
    blk = lax.broadcasted_iota(jnp.int32, (nbp, s), 0)
    cur = lax.broadcasted_iota(jnp.int32, (nbp, s), 1) // MOBA_BLOCK
    for r in range(HG):
        for n in range(nb):
            kb = k_ref[0, n * MOBA_BLOCK:(n + 1) * MOBA_BLOCK, r * HEAD_PAD:(r + 1) * HEAD_PAD].astype(F32)
            kmean_ref[r, n:n + 1, :] = jnp.mean(kb, axis=0, keepdims=True)
        q_t = q_ref[0, r * HEAD_PAD:(r + 1) * HEAD_PAD, :]
        kh, kl = _split_bf16(kmean_ref[r])
        gate = (_dot(kh, q_t) + _dot(kl, q_t))[:nbp]
        top = _rank_select(gate, nb, MOBA_TOPK, lambda n: jnp.where(n < cur, 1.0, 0.0))
        sel = ((blk < cur) & top) | (blk == cur)
        mask_rows = jnp.where(sel, 0.0, NEG).astype(BF16)
        pad_rows = jnp.zeros((HEAD_PAD - EXT0 - nbp, s), BF16)
        qx_ref[r] = jnp.concatenate([q_t[:EXT0], mask_rows, pad_rows], axis=0)

    def scores(i, j):
        qoff = pl.multiple_of(i * TQ, TQ)
        koff = pl.multiple_of(j * TQ, TQ)
        return jnp.concatenate(
            [_dot(k_ref[0, pl.ds(koff, TQ), r * HEAD_PAD:(r + 1) * HEAD_PAD], qx_ref[r, :, pl.ds(qoff, TQ)])
             for r in range(HG)], axis=1)

    def pv_fn(j):
        off = pl.multiple_of(j * TQ, TQ)

        def fn(p):
            return jnp.concatenate(
                [_dot(vt_ref[0, r * HEAD_PAD:r * HEAD_PAD + ACC_ROWS, pl.ds(off, TQ)], _lane_block(p, r))
                 for r in range(HG)], axis=1)
        return fn

    _flat_stream(nb, n_far, far_i_ref, far_j_ref, scores, pv_fn, bias_ref, _shift_row(c31_ref, hg * HG),
                 sa_ref, sb_ref, m_ref, acc_ref)

    def finish(i, carry):
        out = _normalize_padded(acc_ref[:, _state_lanes(i)])
        for r in range(HG):
            o_ref[0, pl.ds(pl.multiple_of(i * TQ, TQ), TQ), r * HEAD_PAD:(r + 1) * HEAD_PAD] = (
                jnp.transpose(_lane_block(out, r)).astype(BF16))
        return carry

    lax.fori_loop(0, nb, finish, 0)


def _moba(pk, pt, bias_win, c31, far_i, far_j):
    b, s, _ = pk.shape
    nb = s // MOBA_BLOCK
    hw = HG * HEAD_PAD
    smem = pl.BlockSpec(memory_space=pltpu.SMEM)
    return pl.pallas_call(
        functools.partial(_moba_body, nb=nb, n_far=far_i.shape[0]),
        grid=(b, MOBA_HEADS // HG),
        in_specs=[
            smem, smem, smem,
            pl.BlockSpec((1, hw, s), lambda bi, hg: (bi, TSLOT_QA // HG + hg, 0)),
            pl.BlockSpec((1, s, hw), lambda bi, hg: (bi, 0, KSLOT_KA // HG + hg)),
            pl.BlockSpec((1, hw, s), lambda bi, hg: (bi, TSLOT_VA // HG + hg, 0)),
            pl.BlockSpec((1, 3 * TQ, WIDE), lambda bi, hg: (hg, 0, 0)),
        ],
        out_specs=pl.BlockSpec((1, s, hw), lambda bi, hg: (bi, 0, hg)),
        out_shape=jax.ShapeDtypeStruct((b, s, MOBA_HEADS * HEAD_PAD), BF16),
        scratch_shapes=[
            pltpu.VMEM((HG, HEAD_PAD, HEAD_PAD), F32),
            pltpu.VMEM((HG, HEAD_PAD, s), BF16),
            pltpu.VMEM((TQ, WIDE), F32),
            pltpu.VMEM((TQ, WIDE), F32),
            pltpu.VMEM((1, nb * WIDE), F32),
            pltpu.VMEM((ACC_ROWS, nb * WIDE), F32),
        ],
        compiler_params=pltpu.CompilerParams(
            dimension_semantics=("parallel", "arbitrary"), vmem_limit_bytes=VMEM_LIMIT),
        name="moba",
    )(c31, far_i, far_j, pt, pk, pt, bias_win)


def _nsa_body(c31_ref, far_i_ref, far_j_ref, q_ref, ks_ref, kw_ref, vst_ref, vwt_ref, kc_ref, vct_ref,
              gates_ref, bias_ref, bcmp_ref, ovt_ref, o_ref,
              qx_ref, sa_ref, sb_ref, m_ref, accs_ref, part_ref, *, ncp, nslc, nq, n_far):
    g = pl.program_id(1)

    def front(i, n_win):
        qoff = pl.multiple_of(i * TQ, TQ)
        q4 = jnp.concatenate([q_ref[0, r * HEAD_PAD:(r + 1) * HEAD_PAD, pl.ds(qoff, TQ)] for r in range(HG)],
                             axis=1)

        band_off = pl.multiple_of((nq - 1 - i) * (TQ // CMP_STRIDE), TQ // CMP_STRIDE)
        s_t = _dot(kc_ref[0, 0], q4) + bcmp_ref[0, pl.ds(band_off, ncp), :]
        m = jnp.max(s_t, axis=0, keepdims=True)
        p = jnp.where(s_t > 0.5 * NEG, jnp.exp2(s_t - m), 0.0)
        den = jnp.sum(p, axis=0, keepdims=True)
        pn = p / jnp.where(den > 0.0, den, 1.0)
        o_cmp = _dot(vct_ref[0, 0, :ACC_ROWS, :], pn.astype(BF16))
        psum = _lane_block(pn, 0)
        for r in range(1, HG):
            psum = psum + _lane_block(pn, r)

        ph, plo = _split_bf16(psum)
        imp = (_dot(ovt_ref[...], ph) + _dot(ovt_ref[...], plo))[:EXT0]
        blk = lax.broadcasted_iota(jnp.int32, imp.shape, 0)
        cur = (i * TQ + lax.broadcasted_iota(jnp.int32, imp.shape, 1)) // SLC_BLOCK
        forced = (blk == 0) | (blk == cur) | (blk == cur - 1)
        score = jnp.where(forced, 1e30, jnp.where(blk <= cur, imp, -1.0))
        score = jnp.where(blk < nslc, score, -2.0)
        sel = _rank_select(score, nslc, SLC_TOPN) & (blk < nslc)
        mask_rows = jnp.where(sel, 0.0, NEG).astype(BF16)
        qx_ref[:, _state_lanes(i)] = jnp.concatenate(
            [q4[:EXT0], jnp.concatenate([mask_rows] * HG, axis=1)], axis=0)

        start = pl.multiple_of((i + 1 - n_win) * TQ, TQ)
        s_w = _dot(kw_ref[0, pl.ds(start, n_win * TQ), :], q4) + bias_ref[0, (3 - n_win) * TQ:, :]
        p_w = jnp.exp2(s_w - jnp.max(s_w, axis=0, keepdims=True)).astype(BF16)
        acc_w = _dot(vwt_ref[0, :ACC_ROWS, pl.ds(start, n_win * TQ)], p_w)
        o_win = acc_w / acc_w[ONES_ROW:ONES_ROW + 1, :]

        gt = gates_ref[0, :, pl.ds(qoff, TQ)]
        part_ref[:, _state_lanes(i)] = jnp.concatenate(
            [gt[N_GATES * r:N_GATES * r + 1, :] * _lane_block(o_cmp, r)
             + gt[N_GATES * r + 2:N_GATES * r + 3, :] * _lane_block(o_win, r) for r in range(HG)], axis=1)

    for i0 in range(min(2, nq)):
        front(jnp.asarray(i0, jnp.int32), i0 + 1)

    def front_loop(i, carry):
        front(i, 3)
        return carry

    lax.fori_loop(2, nq, front_loop, 0)

    def scores(i, j):
        return _dot(ks_ref[0, pl.ds(pl.multiple_of(j * TQ, TQ), TQ), :], qx_ref[:, _state_lanes(i)])

    def pv_fn(j):
        off = pl.multiple_of(j * TQ, TQ)
        return lambda p: _dot(vst_ref[0, :ACC_ROWS, pl.ds(off, TQ)], p)

    _flat_stream(nq, n_far, far_i_ref, far_j_ref, scores, pv_fn, bias_ref,
                 _shift_row(c31_ref, MOBA_HEADS + g * HG), sa_ref, sb_ref, m_ref, accs_ref)

    def finish(i, carry):
        qoff = pl.multiple_of(i * TQ, TQ)
        acc = accs_ref[:, _state_lanes(i)]
        o_slc = acc / acc[ONES_ROW:ONES_ROW + 1, :]
        part = part_ref[:, _state_lanes(i)]
        gt = gates_ref[0, :, pl.ds(qoff, TQ)]
        zpad = jnp.zeros((HEAD_PAD - ACC_ROWS, TQ), F32)
        for r in range(HG):
            out = _lane_block(part, r) + gt[N_GATES * r + 1:N_GATES * r + 2, :] * _lane_block(o_slc, r)
            o_ref[0, pl.ds(qoff, TQ), r * HEAD_PAD:(r + 1) * HEAD_PAD] = jnp.transpose(
                jnp.concatenate([out, zpad], axis=0)).astype(BF16)
        return carry

    lax.fori_loop(0, nq, finish, 0)


def _nsa(pk, pt, kc, vct, gates_t, bias_win, bias_cmp, ovt, c31, far_i, far_j):
    b, s, _ = pk.shape
    nq = s // TQ
    ncp = s // CMP_STRIDE
    nslc = s // SLC_BLOCK
    hw = HG * HEAD_PAD
    band_rows = bias_cmp.shape[1]
    hgrp = MOBA_HEADS // HG
    smem = pl.BlockSpec(memory_space=pltpu.SMEM)
    k_spec = lambda slot: pl.BlockSpec((1, s, HEAD_PAD), lambda bi, g: (bi, 0, slot + g))
    vt_spec = lambda slot: pl.BlockSpec((1, HEAD_PAD, s), lambda bi, g: (bi, slot + g, 0))
    bias_spec = lambda rows: pl.BlockSpec((1, rows, WIDE), lambda bi, g: (hgrp + g, 0, 0),
                                          pipeline_mode=pl.Buffered(1))
    return pl.pallas_call(
        functools.partial(_nsa_body, ncp=ncp, nslc=nslc, nq=nq, n_far=far_i.shape[0]),
        grid=(b, NSA_GROUPS),
        in_specs=[
            smem, smem, smem,
            pl.BlockSpec((1, hw, s), lambda bi, g: (bi, TSLOT_QB // HG + g, 0)),
            k_spec(KSLOT_KS), k_spec(KSLOT_KW), vt_spec(TSLOT_VS), vt_spec(TSLOT_VW),
            pl.BlockSpec((1, 1, ncp, HEAD_PAD), lambda bi, g: (bi, g, 0, 0)),
            pl.BlockSpec((1, 1, HEAD_PAD, ncp), lambda bi, g: (bi, g, 0, 0)),
            pl.BlockSpec((1, GATE_ROWS, s), lambda bi, g: (bi, g, 0)),
            bias_spec(3 * TQ), bias_spec(band_rows),
            pl.BlockSpec((HEAD_PAD, ncp), lambda bi, g: (0, 0)),
        ],
        out_specs=pl.BlockSpec((1, s, hw), lambda bi, g: (bi, 0, g)),
        out_shape=jax.ShapeDtypeStruct((b, s, NSA_HEADS * HEAD_PAD), BF16),
        scratch_shapes=[
            pltpu.VMEM((HEAD_PAD, nq * WIDE), BF16),
            pltpu.VMEM((TQ, WIDE), F32),
            pltpu.VMEM((TQ, WIDE), F32),
            pltpu.VMEM((1, nq * WIDE), F32),
            pltpu.VMEM((ACC_ROWS, nq * WIDE), F32),
            pltpu.VMEM((ACC_ROWS, nq * WIDE), F32),
        ],
        compiler_params=pltpu.CompilerParams(
            dimension_semantics=("parallel", "arbitrary"), vmem_limit_bytes=VMEM_LIMIT),
        name="nsa",
    )(c31, far_i, far_j, pt, pk, pk, pt, pt, kc, vct, gates_t, bias_win, bias_cmp, ovt)


def _outproj_body(x_ref, oa_ref, ob_ref, wa_ref, wb_ref, o_ref):
    o_ref[...] = x_ref[...] + _dot(oa_ref[...], wa_ref[...]) + _dot(ob_ref[...], wb_ref[...])


def _outproj(x2d, oa, ob, wa, wb, *, tm=512):
    n = x2d.shape[0]
    ka = oa.shape[1]
    kb = ob.shape[1]
    return pl.pallas_call(
        _outproj_body,
        grid=(n // tm,),
        in_specs=[
            pl.BlockSpec((tm, D_MODEL), lambda i: (i, 0)),
            pl.BlockSpec((tm, ka), lambda i: (i, 0)),
            pl.BlockSpec((tm, kb), lambda i: (i, 0)),
            pl.BlockSpec((ka, D_MODEL), lambda i: (0, 0)),
            pl.BlockSpec((kb, D_MODEL), lambda i: (0, 0)),
        ],
        out_specs=pl.BlockSpec((tm, D_MODEL), lambda i: (i, 0)),
        out_shape=jax.ShapeDtypeStruct((n, D_MODEL), F32),
        compiler_params=pltpu.CompilerParams(
            dimension_semantics=("parallel",), vmem_limit_bytes=VMEM_LIMIT),
        name="outproj",
    )(x2d, oa, ob, wa, wb)


def _rel_bucket(dist):
    n = jnp.maximum(dist, 0)
    nf = jnp.maximum(n, 1).astype(F32)
    large = REL_MAX_EXACT + (jnp.log(nf / REL_MAX_EXACT) / math.log(REL_MAX_DIST / REL_MAX_EXACT)
                             * (REL_BUCKETS - REL_MAX_EXACT)).astype(jnp.int32)
    return jnp.where(n < REL_MAX_EXACT, n, jnp.minimum(large, REL_BUCKETS - 1))


def _bias_body(tab_ref, *refs):
    h = pl.program_id(0)
    n_tab = len(refs) // 2
    for bk_ref, out_ref in zip(refs[:n_tab], refs[n_tab:]):
        bk = bk_ref[...]
        acc = jnp.full(bk.shape, NEG, F32)
        for bucket in range(REL_BUCKETS):
            acc = jnp.where(bk == bucket, tab_ref[h, bucket], acc)
        out_ref[0] = acc


def _bias_tables(rel_bias, ncp, nq):
    tab = rel_bias.T.astype(F32) * LOG2E
    n_heads = tab.shape[0]
    c = jnp.arange(TQ)[:, None]
    r = jnp.arange(TQ)[None, :]
    d_own = r - c
    bk_own = jnp.where(d_own >= 0, _rel_bucket(d_own), -1)
    bk_prev = _rel_bucket(d_own + TQ)
    bk_win2 = jnp.where(d_own + 2 * TQ < WINDOW, _rel_bucket(d_own + 2 * TQ), -1)
    band_rows = ncp + (nq - 1) * (TQ // CMP_STRIDE)
    u = jnp.arange(band_rows)[:, None] - (nq - 1) * (TQ // CMP_STRIDE)
    d_cmp = r - CMP_STRIDE * u - (CMP_LEN - 1)
    bk_cmp = jnp.where(d_cmp >= 0, _rel_bucket(d_cmp), -1)
    tables = [t.astype(jnp.int32) for t in (jnp.concatenate([bk_win2, bk_prev, bk_own], axis=0), bk_cmp)]
    outs = pl.pallas_call(
        _bias_body,
        grid=(n_heads,),
        in_specs=[pl.BlockSpec(memory_space=pltpu.SMEM)]
        + [pl.BlockSpec(t.shape, lambda h: (0, 0)) for t in tables],
        out_specs=[pl.BlockSpec((1, t.shape[0], TQ), lambda h: (h // HG, 0, h % HG)) for t in tables],
        out_shape=[jax.ShapeDtypeStruct((n_heads // HG, t.shape[0], WIDE), F32) for t in tables],
        compiler_params=pltpu.CompilerParams(dimension_semantics=("parallel",), vmem_limit_bytes=VMEM_LIMIT),
        name="bias_tiles",
    )(tab, *tables)
    return outs, tab[:, REL_BUCKETS - 1]


def _pad_heads(w, n_heads):
    rows = w.shape[0]
    w = w.reshape(rows, n_heads, HEAD_DIM)
    w = jnp.pad(w, ((0, 0), (0, 0), (0, HEAD_PAD - HEAD_DIM)))
    return w.reshape(rows, n_heads * HEAD_PAD)


def _slc_overlap_t(ncp, nslc):
    ratio = SLC_BLOCK // CMP_STRIDE
    mc = CMP_LEN // CMP_STRIDE
    w = np.zeros((HEAD_PAD, nslc * ratio + mc), np.float32)
    for j in range(nslc):
        for a in range(ratio):
            for m in range(mc):
                w[j, ratio * j + a + m] += 1.0
    n_cmp = ncp - 1
    out = np.zeros((HEAD_PAD, ncp), np.float32)
    out[:, :n_cmp] = w[:, :n_cmp]
    return out


def _block_onehot(s, block):
    pat = np.zeros((s, HEAD_PAD), np.float32)
    pat[np.arange(s), EXT0 + np.arange(s) // block] = 1.0
    return pat


def kernel(x, norm_ffn1, w_ffn1_gate, w_ffn1_up, w_ffn1_down, norm_mix, w_in, cmp_pos_k, cmp_w1_k, cmp_w2_k,
           cmp_pos_v, cmp_w1_v, cmp_w2_v, w_out, norm_ffn2, w_ffn2_gate, w_ffn2_up, w_ffn2_down, rel_bias,
           norm_final):
    b, s, d = x.shape
    assert d == D_MODEL and s % (2 * TQ) == 0 and norm_ffn1.shape[0] == 1
    assert s // SLC_BLOCK <= HEAD_PAD - EXT0, "selection-mask rows must fit beside the 64 head dims"
    n = b * s
    nq = s // TQ
    ncp = s // CMP_STRIDE
    nslc = s // SLC_BLOCK
    scale = HEAD_DIM ** -0.5 * LOG2E

    sizes = (512, 512, 512, 512) + (128,) * 6 + (24,)
    splits = np.cumsum(sizes)[:-1].tolist()
    wqa, wka, wva, wqb, wkc, wvc, wks, wvs, wkw, wvw, wgb = jnp.split(w_in[0], splits, axis=1)
    w_k = jnp.concatenate([_pad_heads(wka, 8), _pad_heads(wkc, 2), _pad_heads(wvc, 2), _pad_heads(wks, 2),
                           _pad_heads(wkw, 2)], axis=1).astype(BF16)
    w_t = jnp.concatenate([_pad_heads(wqa * scale, 8), _pad_heads(wva, 8), _pad_heads(wqb * scale, 8),
                           _pad_heads(wvs, 2), _pad_heads(wvw, 2)], axis=1).T.astype(BF16)
    ones_np = np.zeros((N_TSLOTS * HEAD_PAD, 1), np.float32)
    for slot in list(range(TSLOT_VA, TSLOT_VA + 8)) + [TSLOT_VS, TSLOT_VS + 1, TSLOT_VW, TSLOT_VW + 1]:
        ones_np[slot * HEAD_PAD + ONES_ROW, 0] = 1.0
    ones_col = jnp.asarray(ones_np)
    wgt = wgb.reshape(D_MODEL, NSA_GROUPS, NSA_REP * N_GATES)
    wgt = jnp.pad(wgt, ((0, 0), (0, 0), (0, GATE_ROWS - NSA_REP * N_GATES)))
    wgt = wgt.reshape(D_MODEL, NSA_GROUPS * GATE_ROWS).T.astype(BF16)

    def cmp_params(pos, w1, w2):
        pos_p = jnp.pad(pos[0], ((0, 0), (0, HEAD_PAD - HEAD_DIM)))
        w1_p = jnp.pad(w1[0].reshape(CMP_LEN, HEAD_DIM, CMP_HIDDEN), ((0, 0), (0, HEAD_PAD - HEAD_DIM), (0, 0)))
        w2_p = jnp.pad(w2[0], ((0, 0), (0, HEAD_PAD - HEAD_DIM)))
        return pos_p, w1_p.astype(BF16), w2_p.astype(BF16)

    pos_k, w1k, w2k = cmp_params(cmp_pos_k, cmp_w1_k, cmp_w2_k)
    pos_v, w1v, w2v = cmp_params(cmp_pos_v, cmp_w1_v, cmp_w2_v)
    cmp_pos = jnp.stack([pos_k, pos_v])
    cmp_w1 = jnp.stack([w1k, w1v])

    wo = w_out[0]
    wo_a = _pad_heads(wo[:MOBA_HEADS * HEAD_DIM].T, MOBA_HEADS).T.astype(BF16)
    wo_b = _pad_heads(wo[MOBA_HEADS * HEAD_DIM:].T, NSA_HEADS).T.astype(BF16)

    (bias_win, bias_cmp), c31 = _bias_tables(rel_bias, ncp, nq)
    ovt = jnp.asarray(_slc_overlap_t(ncp, nslc)).astype(BF16)
    pat = jnp.asarray(np.concatenate([_block_onehot(s, MOBA_BLOCK), _block_onehot(s, SLC_BLOCK)], axis=1))
    far = [(i, j) for j in range(nq) for i in range(j + 2, nq)]
    far_i = jnp.asarray(np.array([i for i, _ in far], np.int32))
    far_j = jnp.asarray(np.array([j for _, j in far], np.int32))

    x2d = x.reshape(n, D_MODEL)
    nf = norm_final.reshape(1, D_MODEL)
    x1 = _ffn(x2d, norm_ffn1, w_ffn1_gate[0].astype(BF16), w_ffn1_up[0].astype(BF16),
              w_ffn1_down[0].astype(BF16), nf, final_norm=False)
    pk, pt, gates_t = _inproj(x1, norm_mix, w_k, w_t, ones_col, wgt, pat, b, s)
    pk = pk.reshape(b, s, N_KSLOTS * HEAD_PAD)
    kc, vct = _compress(pk, cmp_pos, cmp_w1, w2k, w2v.T)
    o_a = _moba(pk, pt, bias_win, c31, far_i, far_j)
    o_b = _nsa(pk, pt, kc, vct, gates_t, bias_win, bias_cmp, ovt, c31, far_i, far_j)
    x2 = _outproj(x1, o_a.reshape(n, -1), o_b.reshape(n, -1), wo_a, wo_b)
    y = _ffn(x2, norm_ffn2, w_ffn2_gate[0].astype(BF16), w_ffn2_up[0].astype(BF16),
             w_ffn2_down[0].astype(BF16), nf, final_norm=True)
    return y.reshape(b, s, D_MODEL)
```

```python
import functools
import math

import numpy as np
import jax
import jax.numpy as jnp
from jax import lax
from jax.experimental import pallas as pl
from jax.experimental.pallas import tpu as pltpu

D_MODEL = 1024
D_FF = 2816
HEAD_DIM = 64
HEAD_PAD = 128
ONES_ROW = HEAD_DIM
EXT0 = HEAD_DIM
MOBA_HEADS = 8
NSA_HEADS = 8
NSA_GROUPS = 2
NSA_REP = NSA_HEADS // NSA_GROUPS
HG = 4
MOBA_BLOCK = 256
MOBA_TOPK = 3
CMP_LEN = 32
CMP_STRIDE = 16
CMP_HIDDEN = 256
SLC_BLOCK = 64
SLC_TOPN = 16
WINDOW = 512
N_GATES = 3
GATE_ROWS = 16
REL_BUCKETS = 32
REL_MAX_EXACT = REL_BUCKETS // 2
REL_MAX_DIST = 128
RMS_EPS = 1e-6
TQ = 256
WIDE = HG * TQ
NEG = -1e30
LOG2E = 1.4426950408889634
ACC_ROWS = 80
VMEM_LIMIT = 56 * 1024 * 1024

KSLOT_KA, KSLOT_KC, KSLOT_VC, KSLOT_KS, KSLOT_KW = 0, 8, 10, 12, 14
N_KSLOTS = 16
TSLOT_QA, TSLOT_VA, TSLOT_QB, TSLOT_VS, TSLOT_VW = 0, 8, 16, 24, 26
N_TSLOTS = 28

F32 = jnp.float32
BF16 = jnp.bfloat16


def _dot(a, b):
    return jnp.dot(a, b, preferred_element_type=F32)


def _dot_nt(a, b):
    return lax.dot_general(a, b, (((1,), (1,)), ((), ())), preferred_element_type=F32)


def _split_bf16(x):
    hi = x.astype(BF16)
    lo = (x - hi.astype(F32)).astype(BF16)
    return hi, lo


def _rmsnorm(x, g):
    ms = jnp.mean(x * x, axis=-1, keepdims=True)
    return x * lax.rsqrt(ms + RMS_EPS) * g


def _ffn_body(x_ref, g_ref, wg_ref, wu_ref, wd_ref, gf_ref, o_ref, *, final_norm, n_chunks):
    x = x_ref[...]
    h = _rmsnorm(x, g_ref[...]).astype(BF16)
    tf = D_FF // n_chunks
    acc = None
    for c in range(n_chunks):
        a = _dot(h, wg_ref[:, c * tf:(c + 1) * tf])
        u = _dot(h, wu_ref[:, c * tf:(c + 1) * tf])
        z = ((a * jax.nn.sigmoid(a)) * u).astype(BF16)
        part = _dot(z, wd_ref[c * tf:(c + 1) * tf, :])
        acc = part if acc is None else acc + part
    y = x + 0.5 * acc
    if final_norm:
        y = _rmsnorm(y, gf_ref[...])
    o_ref[...] = y


def _ffn(x2d, g, wg, wu, wd, gf, *, final_norm, tm=512, n_chunks=2):
    n = x2d.shape[0]
    resident = lambda shape: pl.BlockSpec(shape, lambda i: (0, 0), pipeline_mode=pl.Buffered(1))
    return pl.pallas_call(
        functools.partial(_ffn_body, final_norm=final_norm, n_chunks=n_chunks),
        grid=(n // tm,),
        in_specs=[
            pl.BlockSpec((tm, D_MODEL), lambda i: (i, 0)),
            resident((1, D_MODEL)),
            resident((D_MODEL, D_FF)), resident((D_MODEL, D_FF)), resident((D_FF, D_MODEL)),
            resident((1, D_MODEL)),
        ],
        out_specs=pl.BlockSpec((tm, D_MODEL), lambda i: (i, 0)),
        out_shape=jax.ShapeDtypeStruct((n, D_MODEL), F32),
        compiler_params=pltpu.CompilerParams(
            dimension_semantics=("parallel",), vmem_limit_bytes=VMEM_LIMIT),
        name="ffn_final" if final_norm else "ffn",
    )(x2d, g, wg, wu, wd, gf)


def _inproj_body(x_ref, g_ref, wk_ref, wt_ref, ones_ref, wgt_ref, pat_ref, pk_ref, pt_ref, gates_ref):
    h = _rmsnorm(x_ref[...], g_ref[...]).astype(BF16)
    pk = _dot(h, wk_ref[...])
    for slot in range(N_KSLOTS):
        piece = pk[:, slot * HEAD_PAD:(slot + 1) * HEAD_PAD]
        if KSLOT_KA <= slot < KSLOT_KA + MOBA_HEADS:
            piece = piece + pat_ref[:, :HEAD_PAD]
        elif KSLOT_KS <= slot < KSLOT_KS + NSA_GROUPS:
            piece = piece + pat_ref[:, HEAD_PAD:]
        pk_ref[:, slot * HEAD_PAD:(slot + 1) * HEAD_PAD] = piece.astype(BF16)
    pt_ref[0] = (_dot_nt(wt_ref[...], h) + ones_ref[...]).astype(BF16)
    gates_ref[0] = jax.nn.sigmoid(_dot_nt(wgt_ref[...], h))


def _inproj(x2d, g, wk, wt, ones_col, wgt, pat, b, s, *, tm=256):
    n = x2d.shape[0]
    nk = wk.shape[1]
    nt = wt.shape[0]
    ng = wgt.shape[0]
    spt = s // tm
    const = lambda shape: pl.BlockSpec(shape, lambda i: (0, 0))
    return pl.pallas_call(
        _inproj_body,
        grid=(n // tm,),
        in_specs=[
            pl.BlockSpec((tm, D_MODEL), lambda i: (i, 0)),
            const((1, D_MODEL)), const((D_MODEL, nk)), const((nt, D_MODEL)), const((nt, 1)), const((ng, D_MODEL)),
            pl.BlockSpec((tm, 2 * HEAD_PAD), lambda i: (i % spt, 0)),
        ],
        out_specs=[
            pl.BlockSpec((tm, nk), lambda i: (i, 0)),
            pl.BlockSpec((1, nt, tm), lambda i: (i // spt, 0, i % spt)),
            pl.BlockSpec((1, ng, tm), lambda i: (i // spt, 0, i % spt)),
        ],
        out_shape=[jax.ShapeDtypeStruct((n, nk), BF16), jax.ShapeDtypeStruct((b, nt, s), BF16),
                   jax.ShapeDtypeStruct((b, ng, s), F32)],
        compiler_params=pltpu.CompilerParams(
            dimension_semantics=("parallel",), vmem_limit_bytes=VMEM_LIMIT),
        name="inproj",
    )(x2d, g, wk, wt, ones_col, wgt, pat)


def _compress_body(kraw_ref, vraw_ref, pos_ref, w1_ref, w2k_ref, w2vt_ref, kc_ref, vct_ref, xs_ref, *, ncp):
    half = CMP_LEN // 2
    for kv, raw_ref in enumerate((kraw_ref, vraw_ref)):
        xs_ref[...] = raw_ref[0].astype(F32)
        y0 = jnp.zeros((ncp, CMP_HIDDEN), F32)
        y1 = jnp.zeros((ncp, CMP_HIDDEN), F32)
        for p in range(half):
            xp = xs_ref[pl.ds(p, ncp, stride=CMP_STRIDE), :]
            y0 = y0 + _dot((xp + pos_ref[kv, p:p + 1, :]).astype(BF16), w1_ref[kv, p])
            y1 = y1 + _dot((xp + pos_ref[kv, half + p:half + p + 1, :]).astype(BF16), w1_ref[kv, half + p])
        h1 = y0 + pltpu.roll(y1, ncp - 1, 0)
        act = (h1 * jax.nn.sigmoid(h1)).astype(BF16)
        if kv == 0:
            kc_ref[0, 0] = _dot(act, w2k_ref[...]).astype(BF16)
        else:
            out_t = _dot_nt(w2vt_ref[...], act)
            row = lax.broadcasted_iota(jnp.int32, out_t.shape, 0)
            vct_ref[0, 0] = (out_t + jnp.where(row == ONES_ROW, 1.0, 0.0)).astype(BF16)


def _compress(pk, pos, w1, w2k, w2vt):
    b, s, _ = pk.shape
    ncp = s // CMP_STRIDE
    full = lambda shape: pl.BlockSpec(shape, lambda bi, g: (0,) * len(shape))
    return pl.pallas_call(
        functools.partial(_compress_body, ncp=ncp),
        grid=(b, NSA_GROUPS),
        in_specs=[
            pl.BlockSpec((1, s, HEAD_PAD), lambda bi, g: (bi, 0, KSLOT_KC + g)),
            pl.BlockSpec((1, s, HEAD_PAD), lambda bi, g: (bi, 0, KSLOT_VC + g)),
            full((2, CMP_LEN, HEAD_PAD)),
            full((2, CMP_LEN, HEAD_PAD, CMP_HIDDEN)),
            full((CMP_HIDDEN, HEAD_PAD)),
            full((HEAD_PAD, CMP_HIDDEN)),
        ],
        out_specs=[
            pl.BlockSpec((1, 1, ncp, HEAD_PAD), lambda bi, g: (bi, g, 0, 0)),
            pl.BlockSpec((1, 1, HEAD_PAD, ncp), lambda bi, g: (bi, g, 0, 0)),
        ],
        out_shape=[
            jax.ShapeDtypeStruct((b, NSA_GROUPS, ncp, HEAD_PAD), BF16),
            jax.ShapeDtypeStruct((b, NSA_GROUPS, HEAD_PAD, ncp), BF16),
        ],
        scratch_shapes=[pltpu.VMEM((s, HEAD_PAD), F32)],
        compiler_params=pltpu.CompilerParams(
            dimension_semantics=("parallel", "arbitrary"), vmem_limit_bytes=VMEM_LIMIT),
        name="compress",
    )(pk, pk, pos, w1, w2k, w2vt)


def _flash_update(s_t, shift, pv_fn, m_ref, acc_ref):
    m_prev = m_ref[...]
    m_new = jnp.maximum(m_prev, jnp.max(s_t, axis=0, keepdims=True) + shift)
    alpha = jnp.exp2(m_prev - m_new)
    p = jnp.exp2(s_t - (m_new - shift)).astype(BF16)
    acc_ref[...] = alpha * acc_ref[...] + pv_fn(p)
    m_ref[...] = m_new


def _state_lanes(i):
    return pl.ds(pl.multiple_of(i * WIDE, WIDE), WIDE)


def _tile_phase(count, tile_fn, scores_fn, step_fn, sa_ref, sb_ref):
    if count == 0:
        return
    tile = lambda n: tile_fn(jnp.minimum(jnp.asarray(n, jnp.int32), count - 1))
    sa_ref[...] = scores_fn(*tile(0))

    def pair(p, carry):
        n = 2 * p
        sb_ref[...] = scores_fn(*tile(n + 1))
        step_fn(sa_ref, *tile(n))
        sa_ref[...] = scores_fn(*tile(n + 2))
        step_fn(sb_ref, *tile(n + 1))
        return carry

    lax.fori_loop(0, count // 2, pair, 0, unroll=4)
    if count % 2:
        step_fn(sa_ref, *tile(count - 1))


def _flat_stream(nq, n_far, far_i_ref, far_j_ref, scores_fn, pv_fn, bias_ref, shift_far,
                 sa_ref, sb_ref, m_ref, acc_ref):
    zero = jnp.zeros((1, WIDE), F32)

    def own_step(s_ref, i, j):
        s_t = s_ref[...] + bias_ref[0, 2 * TQ:3 * TQ, :]
        m = jnp.max(s_t, axis=0, keepdims=True)
        m_ref[:, _state_lanes(i)] = m
        acc_ref[:, _state_lanes(i)] = pv_fn(j)(jnp.exp2(s_t - m).astype(BF16))

    def prev_step(s_ref, i, j):
        _flash_update(s_ref[...] + bias_ref[0, TQ:2 * TQ, :], zero, pv_fn(j),
                      m_ref.at[:, _state_lanes(i)], acc_ref.at[:, _state_lanes(i)])

    def far_step(s_ref, i, j):
        _flash_update(s_ref[...], shift_far, pv_fn(j), m_ref.at[:, _state_lanes(i)], acc_ref.at[:, _state_lanes(i)])

    _tile_phase(nq, lambda n: (n, n), scores_fn, own_step, sa_ref, sb_ref)
    _tile_phase(nq - 1, lambda n: (n + 1, n), scores_fn, prev_step, sa_ref, sb_ref)
    _tile_phase(n_far, lambda n: (far_i_ref[n], far_j_ref[n]), scores_fn, far_step, sa_ref, sb_ref)


def _normalize_padded(acc):
    out = acc / acc[ONES_ROW:ONES_ROW + 1, :]
    return jnp.concatenate([out, jnp.zeros((HEAD_PAD - ACC_ROWS, WIDE), F32)], axis=0)


def _lane_block(x, r):
    return x[:, r * TQ:(r + 1) * TQ]


def _shift_row(c31_ref, head0):
    return jnp.concatenate([jnp.full((1, TQ), c31_ref[head0 + r], F32) for r in range(HG)], axis=1)


def _rank_select(score, n_rows, top, row_weight=None):
    if row_weight is not None:
        blk = lax.broadcasted_iota(jnp.int32, score.shape, 0)
        cnt = jnp.zeros(score.shape, F32)
        for j in range(n_rows):
            row = score[j:j + 1, :]
            ahead = jnp.where(row > score, 1.0, jnp.where((row == score) & (blk > j), 1.0, 0.0))
            cnt = cnt + ahead * row_weight(j)
        return cnt < top
    sub = 8
    n_groups = score.shape[0] // sub
    groups = [score[g * sub:(g + 1) * sub] for g in range(n_groups)]
    counts = [jnp.zeros(groups[0].shape, F32) for _ in range(n_groups)]
    local = lax.broadcasted_iota(jnp.int32, groups[0].shape, 0)
    for j in range(n_rows):
        row = score[j:j + 1, :]
        weight = None if row_weight is None else row_weight(j)
        for g in range(n_groups):
            one = 1.0 if weight is None else weight[g * sub:(g + 1) * sub]
            if j < g * sub:
                ahead = jnp.where(row >= groups[g], one, 0.0)
            elif j >= (g + 1) * sub:
                ahead = jnp.where(row > groups[g], one, 0.0)
            else:
                ahead = jnp.where(local > j - g * sub, jnp.where(row >= groups[g], one, 0.0),
                                  jnp.where(row > groups[g], one, 0.0))
            counts[g] = counts[g] + ahead
    return jnp.concatenate(counts, axis=0) < top


def _moba_body(c31_ref, far_i_ref, far_j_ref, q_ref, k_ref, vt_ref, bias_ref, o_ref,
               kmean_ref, qx_ref, sa_ref, sb_ref, m_ref, acc_ref, *, nb, n_far):
    hg = pl.program_id(1)
    s = q_ref.shape[2]
    nbp = -(-nb // 16) * 16

    kmean_ref[...] = jnp.zeros_like(kmean_ref)
    blk = lax.broadcasted_iota(jnp.int32, (nbp, s), 0)
    cur = lax.broadcasted_iota(jnp.int32, (nbp, s), 1) // MOBA_BLOCK
    for r in range(HG):
        for n in range(nb):
            kb = k_ref[0, n * MOBA_BLOCK:(n + 1) * MOBA_BLOCK, r * HEAD_PAD:(r + 1) * HEAD_PAD].astype(F32)
            kmean_ref[r, n:n + 1, :] = jnp.mean(kb, axis=0, keepdims=True)
        q_t = q_ref[0, r * HEAD_PAD:(r + 1) * HEAD_PAD, :]
        kh, kl = _split_bf16(kmean_ref[r])
        gate = (_dot(kh, q_t) + _dot(kl, q_t))[:nbp]
        top = _rank_select(gate, nb, MOBA_TOPK, lambda n: jnp.where(n < cur, 1.0, 0.0))
        sel = ((blk < cur) & top) | (blk == cur)
        mask_rows = jnp.where(sel, 0.0, NEG).astype(BF16)
        pad_rows = jnp.zeros((HEAD_PAD - EXT0 - nbp, s), BF16)
        qx_ref[r] = jnp.concatenate([q_t[:EXT0], mask_rows, pad_rows], axis=0)

    def scores(i, j):
        qoff = pl.multiple_of(i * TQ, TQ)
        koff = pl.multiple_of(j * TQ, TQ)
        return jnp.concatenate(
            [_dot(k_ref[0, pl.ds(koff, TQ), r * HEAD_PAD:(r + 1) * HEAD_PAD], qx_ref[r, :, pl.ds(qoff, TQ)])
             for r in range(HG)], axis=1)

    def pv_fn(j):
        off = pl.multiple_of(j * TQ, TQ)

        def fn(p):
            return jnp.concatenate(
                [_dot(vt_ref[0, r * HEAD_PAD:r * HEAD_PAD + ACC_ROWS, pl.ds(off, TQ)], _lane_block(p, r))
                 for r in range(HG)], axis=1)
        return fn

    _flat_stream(nb, n_far, far_i_ref, far_j_ref, scores, pv_fn, bias_ref, _shift_row(c31_ref, hg * HG),
                 sa_ref, sb_ref, m_ref, acc_ref)

    def finish(i, carry):
        out = _normalize_padded(acc_ref[:, _state_lanes(i)])
        for r in range(HG):
            o_ref[0, pl.ds(pl.multiple_of(i * TQ, TQ), TQ), r * HEAD_PAD:(r + 1) * HEAD_PAD] = (
                jnp.transpose(_lane_block(out, r)).astype(BF16))
        return carry

    lax.fori_loop(0, nb, finish, 0)


def _moba(pk, pt, bias_win, c31, far_i, far_j):
    b, s, _ = pk.shape
    nb = s // MOBA_BLOCK
    hw = HG * HEAD_PAD
    smem = pl.BlockSpec(memory_space=pltpu.SMEM)
    return pl.pallas_call(
        functools.partial(_moba_body, nb=nb, n_far=far_i.shape[0]),
        grid=(b, MOBA_HEADS // HG),
        in_specs=[
            smem, smem, smem,
            pl.BlockSpec((1, hw, s), lambda bi, hg: (bi, TSLOT_QA // HG + hg, 0)),
            pl.BlockSpec((1, s, hw), lambda bi, hg: (bi, 0, KSLOT_KA // HG + hg)),
            pl.BlockSpec((1, hw, s), lambda bi, hg: (bi, TSLOT_VA // HG + hg, 0)),
            pl.BlockSpec((1, 3 * TQ, WIDE), lambda bi, hg: (hg, 0, 0)),
        ],
        out_specs=pl.BlockSpec((1, s, hw), lambda bi, hg: (bi, 0, hg)),
        out_shape=jax.ShapeDtypeStruct((b, s, MOBA_HEADS * HEAD_PAD), BF16),
        scratch_shapes=[
            pltpu.VMEM((HG, HEAD_PAD, HEAD_PAD), F32),
            pltpu.VMEM((HG, HEAD_PAD, s), BF16),
            pltpu.VMEM((TQ, WIDE), F32),
            pltpu.VMEM((TQ, WIDE), F32),
            pltpu.VMEM((1, nb * WIDE), F32),
            pltpu.VMEM((ACC_ROWS, nb * WIDE), F32),
        ],
        compiler_params=pltpu.CompilerParams(
            dimension_semantics=("parallel", "arbitrary"), vmem_limit_bytes=VMEM_LIMIT),
        name="moba",
    )(c31, far_i, far_j, pt, pk, pt, bias_win)


def _nsa_body(c31_ref, far_i_ref, far_j_ref, q_ref, ks_ref, kw_ref, vst_ref, vwt_ref, kc_ref, vct_ref,
              gates_ref, bias_ref, bcmp_ref, ovt_ref, o_ref,
              qx_ref, sa_ref, sb_ref, m_ref, accs_ref, part_ref, *, ncp, nslc, nq, n_far):
    g = pl.program_id(1)

    def front(i, n_win):
        qoff = pl.multiple_of(i * TQ, TQ)
        q4 = jnp.concatenate([q_ref[0, r * HEAD_PAD:(r + 1) * HEAD_PAD, pl.ds(qoff, TQ)] for r in range(HG)],
                             axis=1)

        band_off = pl.multiple_of((nq - 1 - i) * (TQ // CMP_STRIDE), TQ // CMP_STRIDE)
        s_t = _dot(kc_ref[0, 0], q4) + bcmp_ref[0, pl.ds(band_off, ncp), :]
        m = jnp.max(s_t, axis=0, keepdims=True)
        p = jnp.where(s_t > 0.5 * NEG, jnp.exp2(s_t - m), 0.0)
        den = jnp.sum(p, axis=0, keepdims=True)
        pn = p / jnp.where(den > 0.0, den, 1.0)
        o_cmp = _dot(vct_ref[0, 0, :ACC_ROWS, :], pn.astype(BF16))
        psum = _lane_block(pn, 0)
        for r in range(1, HG):
            psum = psum + _lane_block(pn, r)

        ph, plo = _split_bf16(psum)
        imp = (_dot(ovt_ref[...], ph) + _dot(ovt_ref[...], plo))[:EXT0]
        blk = lax.broadcasted_iota(jnp.int32, imp.shape, 0)
        cur = (i * TQ + lax.broadcasted_iota(jnp.int32, imp.shape, 1)) // SLC_BLOCK
        forced = (blk == 0) | (blk == cur) | (blk == cur - 1)
        score = jnp.where(forced, 1e30, jnp.where(blk <= cur, imp, -1.0))
        score = jnp.where(blk < nslc, score, -2.0)
        sel = _rank_select(score, nslc, SLC_TOPN) & (blk < nslc)
        mask_rows = jnp.where(sel, 0.0, NEG).astype(BF16)
        qx_ref[:, _state_lanes(i)] = jnp.concatenate(
            [q4[:EXT0], jnp.concatenate([mask_rows] * HG, axis=1)], axis=0)

        start = pl.multiple_of((i + 1 - n_win) * TQ, TQ)
        s_w = _dot(kw_ref[0, pl.ds(start, n_win * TQ), :], q4) + bias_ref[0, (3 - n_win) * TQ:, :]
        p_w = jnp.exp2(s_w - jnp.max(s_w, axis=0, keepdims=True)).astype(BF16)
        acc_w = _dot(vwt_ref[0, :ACC_ROWS, pl.ds(start, n_win * TQ)], p_w)
        o_win = acc_w / acc_w[ONES_ROW:ONES_ROW + 1, :]

        gt = gates_ref[0, :, pl.ds(qoff, TQ)]
        part_ref[:, _state_lanes(i)] = jnp.concatenate(
            [gt[N_GATES * r:N_GATES * r + 1, :] * _lane_block(o_cmp, r)
             + gt[N_GATES * r + 2:N_GATES * r + 3, :] * _lane_block(o_win, r) for r in range(HG)], axis=1)

    for i0 in range(min(2, nq)):
        front(jnp.asarray(i0, jnp.int32), i0 + 1)

    def front_loop(i, carry):
        front(i, 3)
        return carry

    lax.fori_loop(2, nq, front_loop, 0)

    def scores(i, j):
        return _dot(ks_ref[0, pl.ds(pl.multiple_of(j * TQ, TQ), TQ), :], qx_ref[:, _state_lanes(i)])

    def pv_fn(j):
        off = pl.multiple_of(j * TQ, TQ)
        return lambda p: _dot(vst_ref[0, :ACC_ROWS, pl.ds(off, TQ)], p)

    _flat_stream(nq, n_far, far_i_ref, far_j_ref, scores, pv_fn, bias_ref,
                 _shift_row(c31_ref, MOBA_HEADS + g * HG), sa_ref, sb_ref, m_ref, accs_ref)

    def finish(i, carry):
        qoff = pl.multiple_of(i * TQ, TQ)
        acc = accs_ref[:, _state_lanes(i)]
        o_slc = acc / acc[ONES_ROW:ONES_ROW + 1, :]
        part = part_ref[:, _state_lanes(i)]
        gt = gates_ref[0, :, pl.ds(qoff, TQ)]
        zpad = jnp.zeros((HEAD_PAD - ACC_ROWS, TQ), F32)
        for r in range(HG):
            out = _lane_block(part, r) + gt[N_GATES * r + 1:N_GATES * r + 2, :] * _lane_block(o_slc, r)
            o_ref[0, pl.ds(qoff, TQ), r * HEAD_PAD:(r + 1) * HEAD_PAD] = jnp.transpose(
                jnp.concatenate([out, zpad], axis=0)).astype(BF16)
        return carry

    lax.fori_loop(0, nq, finish, 0)


def _nsa(pk, pt, kc, vct, gates_t, bias_win, bias_cmp, ovt, c31, far_i, far_j):
    b, s, _ = pk.shape
    nq = s // TQ
    ncp = s // CMP_STRIDE
    nslc = s // SLC_BLOCK
    hw = HG * HEAD_PAD
    band_rows = bias_cmp.shape[1]
    hgrp = MOBA_HEADS // HG
    smem = pl.BlockSpec(memory_space=pltpu.SMEM)
    k_spec = lambda slot: pl.BlockSpec((1, s, HEAD_PAD), lambda bi, g: (bi, 0, slot + g))
    vt_spec = lambda slot: pl.BlockSpec((1, HEAD_PAD, s), lambda bi, g: (bi, slot + g, 0))
    bias_spec = lambda rows: pl.BlockSpec((1, rows, WIDE), lambda bi, g: (hgrp + g, 0, 0),
                                          pipeline_mode=pl.Buffered(1))
    return pl.pallas_call(
        functools.partial(_nsa_body, ncp=ncp, nslc=nslc, nq=nq, n_far=far_i.shape[0]),
        grid=(b, NSA_GROUPS),
        in_specs=[
            smem, smem, smem,
            pl.BlockSpec((1, hw, s), lambda bi, g: (bi, TSLOT_QB // HG + g, 0)),
            k_spec(KSLOT_KS), k_spec(KSLOT_KW), vt_spec(TSLOT_VS), vt_spec(TSLOT_VW),
            pl.BlockSpec((1, 1, ncp, HEAD_PAD), lambda bi, g: (bi, g, 0, 0)),
            pl.BlockSpec((1, 1, HEAD_PAD, ncp), lambda bi, g: (bi, g, 0, 0)),
            pl.BlockSpec((1, GATE_ROWS, s), lambda bi, g: (bi, g, 0)),
            bias_spec(3 * TQ), bias_spec(band_rows),
            pl.BlockSpec((HEAD_PAD, ncp), lambda bi, g: (0, 0)),
        ],
        out_specs=pl.BlockSpec((1, s, hw), lambda bi, g: (bi, 0, g)),
        out_shape=jax.ShapeDtypeStruct((b, s, NSA_HEADS * HEAD_PAD), BF16),
        scratch_shapes=[
            pltpu.VMEM((HEAD_PAD, nq * WIDE), BF16),
            pltpu.VMEM((TQ, WIDE), F32),
            pltpu.VMEM((TQ, WIDE), F32),
            pltpu.VMEM((1, nq * WIDE), F32),
            pltpu.VMEM((ACC_ROWS, nq * WIDE), F32),
            pltpu.VMEM((ACC_ROWS, nq * WIDE), F32),
        ],
        compiler_params=pltpu.CompilerParams(
            dimension_semantics=("parallel", "arbitrary"), vmem_limit_bytes=VMEM_LIMIT),
        name="nsa",
    )(c31, far_i, far_j, pt, pk, pk, pt, pt, kc, vct, gates_t, bias_win, bias_cmp, ovt)


def _outproj_body(x_ref, oa_ref, ob_ref, wa_ref, wb_ref, o_ref):
    o_ref[...] = x_ref[...] + _dot(oa_ref[...], wa_ref[...]) + _dot(ob_ref[...], wb_ref[...])


def _outproj(x2d, oa, ob, wa, wb, *, tm=512):
    n = x2d.shape[0]
    ka = oa.shape[1]
    kb = ob.shape[1]
    return pl.pallas_call(
        _outproj_body,
        grid=(n // tm,),
        in_specs=[
            pl.BlockSpec((tm, D_MODEL), lambda i: (i, 0)),
            pl.BlockSpec((tm, ka), lambda i: (i, 0)),
            pl.BlockSpec((tm, kb), lambda i: (i, 0)),
            pl.BlockSpec((ka, D_MODEL), lambda i: (0, 0)),
            pl.BlockSpec((kb, D_MODEL), lambda i: (0, 0)),
        ],
        out_specs=pl.BlockSpec((tm, D_MODEL), lambda i: (i, 0)),
        out_shape=jax.ShapeDtypeStruct((n, D_MODEL), F32),
        compiler_params=pltpu.CompilerParams(
            dimension_semantics=("parallel",), vmem_limit_bytes=VMEM_LIMIT),
        name="outproj",
    )(x2d, oa, ob, wa, wb)


def _rel_bucket(dist):
    n = jnp.maximum(dist, 0)
    nf = jnp.maximum(n, 1).astype(F32)
    large = REL_MAX_EXACT + (jnp.log(nf / REL_MAX_EXACT) / math.log(REL_MAX_DIST / REL_MAX_EXACT)
                             * (REL_BUCKETS - REL_MAX_EXACT)).astype(jnp.int32)
    return jnp.where(n < REL_MAX_EXACT, n, jnp.minimum(large, REL_BUCKETS - 1))


def _bias_body(tab_ref, *refs):
    h = pl.program_id(0)
    n_tab = len(refs) // 2
    for bk_ref, out_ref in zip(refs[:n_tab], refs[n_tab:]):
        bk = bk_ref[...]
        acc = jnp.full(bk.shape, NEG, F32)
        for bucket in range(REL_BUCKETS):
            acc = jnp.where(bk == bucket, tab_ref[h, bucket], acc)
        out_ref[0] = acc


def _bias_tables(rel_bias, ncp, nq):
    tab = rel_bias.T.astype(F32) * LOG2E
    n_heads = tab.shape[0]
    c = jnp.arange(TQ)[:, None]
    r = jnp.arange(TQ)[None, :]
    d_own = r - c
    bk_own = jnp.where(d_own >= 0, _rel_bucket(d_own), -1)
    bk_prev = _rel_bucket(d_own + TQ)
    bk_win2 = jnp.where(d_own + 2 * TQ < WINDOW, _rel_bucket(d_own + 2 * TQ), -1)
    band_rows = ncp + (nq - 1) * (TQ // CMP_STRIDE)
    u = jnp.arange(band_rows)[:, None] - (nq - 1) * (TQ // CMP_STRIDE)
    d_cmp = r - CMP_STRIDE * u - (CMP_LEN - 1)
    bk_cmp = jnp.where(d_cmp >= 0, _rel_bucket(d_cmp), -1)
    tables = [t.astype(jnp.int32) for t in (jnp.concatenate([bk_win2, bk_prev, bk_own], axis=0), bk_cmp)]
    outs = pl.pallas_call(
        _bias_body,
        grid=(n_heads,),
        in_specs=[pl.BlockSpec(memory_space=pltpu.SMEM)]
        + [pl.BlockSpec(t.shape, lambda h: (0, 0)) for t in tables],
        out_specs=[pl.BlockSpec((1, t.shape[0], TQ), lambda h: (h // HG, 0, h % HG)) for t in tables],
        out_shape=[jax.ShapeDtypeStruct((n_heads // HG, t.shape[0], WIDE), F32) for t in tables],
        compiler_params=pltpu.CompilerParams(dimension_semantics=("parallel",), vmem_limit_bytes=VMEM_LIMIT),
        name="bias_tiles",
    )(tab, *tables)
    return outs, tab[:, REL_BUCKETS - 1]


def _pad_heads(w, n_heads):
    rows = w.shape[0]
    w = w.reshape(rows, n_heads, HEAD_DIM)
    w = jnp.pad(w, ((0, 0), (0, 0), (0, HEAD_PAD - HEAD_DIM)))
    return w.reshape(rows, n_heads * HEAD_PAD)


def _slc_overlap_t(ncp, nslc):
    ratio = SLC_BLOCK // CMP_STRIDE
    mc = CMP_LEN // CMP_STRIDE
    w = np.zeros((HEAD_PAD, nslc * ratio + mc), np.float32)
    for j in range(nslc):
        for a in range(ratio):
            for m in range(mc):
                w[j, ratio * j + a + m] += 1.0
    n_cmp = ncp - 1
    out = np.zeros((HEAD_PAD, ncp), np.float32)
    out[:, :n_cmp] = w[:, :n_cmp]
    return out


def _block_onehot(s, block):
    pat = np.zeros((s, HEAD_PAD), np.float32)
    pat[np.arange(s), EXT0 + np.arange(s) // block] = 1.0
    return pat


def kernel(x, norm_ffn1, w_ffn1_gate, w_ffn1_up, w_ffn1_down, norm_mix, w_in, cmp_pos_k, cmp_w1_k, cmp_w2_k,
           cmp_pos_v, cmp_w1_v, cmp_w2_v, w_out, norm_ffn2, w_ffn2_gate, w_ffn2_up, w_ffn2_down, rel_bias,
           norm_final):
    b, s, d = x.shape
    assert d == D_MODEL and s % (2 * TQ) == 0 and norm_ffn1.shape[0] == 1
    assert s // SLC_BLOCK <= HEAD_PAD - EXT0, "selection-mask rows must fit beside the 64 head dims"
    n = b * s
    nq = s // TQ
    ncp = s // CMP_STRIDE
    nslc = s // SLC_BLOCK
    scale = HEAD_DIM ** -0.5 * LOG2E

    sizes = (512, 512, 512, 512) + (128,) * 6 + (24,)
    splits = np.cumsum(sizes)[:-1].tolist()
    wqa, wka, wva, wqb, wkc, wvc, wks, wvs, wkw, wvw, wgb = jnp.split(w_in[0], splits, axis=1)
    w_k = jnp.concatenate([_pad_heads(wka, 8), _pad_heads(wkc, 2), _pad_heads(wvc, 2), _pad_heads(wks, 2),
                           _pad_heads(wkw, 2)], axis=1).astype(BF16)
    w_t = jnp.concatenate([_pad_heads(wqa * scale, 8), _pad_heads(wva, 8), _pad_heads(wqb * scale, 8),
                           _pad_heads(wvs, 2), _pad_heads(wvw, 2)], axis=1).T.astype(BF16)
    ones_np = np.zeros((N_TSLOTS * HEAD_PAD, 1), np.float32)
    for slot in list(range(TSLOT_VA, TSLOT_VA + 8)) + [TSLOT_VS, TSLOT_VS + 1, TSLOT_VW, TSLOT_VW + 1]:
        ones_np[slot * HEAD_PAD + ONES_ROW, 0] = 1.0
    ones_col = jnp.asarray(ones_np)
    wgt = wgb.reshape(D_MODEL, NSA_GROUPS, NSA_REP * N_GATES)
    wgt = jnp.pad(wgt, ((0, 0), (0, 0), (0, GATE_ROWS - NSA_REP * N_GATES)))
    wgt = wgt.reshape(D_MODEL, NSA_GROUPS * GATE_ROWS).T.astype(BF16)

    def cmp_params(pos, w1, w2):
        pos_p = jnp.pad(pos[0], ((0, 0), (0, HEAD_PAD - HEAD_DIM)))
        w1_p = jnp.pad(w1[0].reshape(CMP_LEN, HEAD_DIM, CMP_HIDDEN), ((0, 0), (0, HEAD_PAD - HEAD_DIM), (0, 0)))
        w2_p = jnp.pad(w2[0], ((0, 0), (0, HEAD_PAD - HEAD_DIM)))
        return pos_p, w1_p.astype(BF16), w2_p.astype(BF16)

    pos_k, w1k, w2k = cmp_params(cmp_pos_k, cmp_w1_k, cmp_w2_k)
    pos_v, w1v, w2v = cmp_params(cmp_pos_v, cmp_w1_v, cmp_w2_v)
    cmp_pos = jnp.stack([pos_k, pos_v])
    cmp_w1 = jnp.stack([w1k, w1v])

    wo = w_out[0]
    wo_a = _pad_heads(wo[:MOBA_HEADS * HEAD_DIM].T, MOBA_HEADS).T.astype(BF16)
    wo_b = _pad_heads(wo[MOBA_HEADS * HEAD_DIM:].T, NSA_HEADS).T.astype(BF16)

    (bias_win, bias_cmp), c31 = _bias_tables(rel_bias, ncp, nq)
    ovt = jnp.asarray(_slc_overlap_t(ncp, nslc)).astype(BF16)
    pat = jnp.asarray(np.concatenate([_block_onehot(s, MOBA_BLOCK), _block_onehot(s, SLC_BLOCK)], axis=1))
    far = [(i, j) for j in range(nq) for i in range(j + 2, nq)]
    far_i = jnp.asarray(np.array([i for i, _ in far], np.int32))
    far_j = jnp.asarray(np.array([j for _, j in far], np.int32))

    x2d = x.reshape(n, D_MODEL)
    nf = norm_final.reshape(1, D_MODEL)
    x1 = _ffn(x2d, norm_ffn1, w_ffn1_gate[0].astype(BF16), w_ffn1_up[0].astype(BF16),
              w_ffn1_down[0].astype(BF16), nf, final_norm=False)
    pk, pt, gates_t = _inproj(x1, norm_mix, w_k, w_t, ones_col, wgt, pat, b, s)
    pk = pk.reshape(b, s, N_KSLOTS * HEAD_PAD)
    kc, vct = _compress(pk, cmp_pos, cmp_w1, w2k, w2v.T)
    o_a = _moba(pk, pt, bias_win, c31, far_i, far_j)
    o_b = _nsa(pk, pt, kc, vct, gates_t, bias_win, bias_cmp, ovt, c31, far_i, far_j)
    x2 = _outproj(x1, o_a.reshape(n, -1), o_b.reshape(n, -1), wo_a, wo_b)
    y = _ffn(x2, norm_ffn2, w_ffn2_gate[0].astype(BF16), w_ffn2_up[0].astype(BF16),
             w_ffn2_down[0].astype(BF16), nf, final_norm=True)
    return y.reshape(b, s, D_MODEL)
```

```python
import functools
import math

import numpy as np
import jax
import jax.numpy as jnp
from jax import lax
from jax.experimental import pallas as pl
from jax.experimental.pallas import tpu as pltpu

D_MODEL = 1024
D_FF = 2816
HEAD_DIM = 64
HEAD_PAD = 128
ONES_ROW = HEAD_DIM
EXT0 = HEAD_DIM
MOBA_HEADS = 8
NSA_HEADS = 8
NSA_GROUPS = 2
NSA_REP = NSA_HEADS // NSA_GROUPS
HG = 4
MOBA_BLOCK = 256
MOBA_TOPK = 3
CMP_LEN = 32
CMP_STRIDE = 16
CMP_HIDDEN = 256
SLC_BLOCK = 64
SLC_TOPN = 16
WINDOW = 512
N_GATES = 3
GATE_ROWS = 16
REL_BUCKETS = 32
REL_MAX_EXACT = REL_BUCKETS // 2
REL_MAX_DIST = 128
RMS_EPS = 1e-6
TQ = 256
WIDE = HG * TQ
NEG = -1e30
LOG2E = 1.4426950408889634
ACC_ROWS = 80
VMEM_LIMIT = 56 * 1024 * 1024

KSLOT_KA, KSLOT_KC, KSLOT_VC, KSLOT_KS, KSLOT_KW = 0, 8, 10, 12, 14
N_KSLOTS = 16
TSLOT_QA, TSLOT_VA, TSLOT_QB, TSLOT_VS, TSLOT_VW = 0, 8, 16, 24, 26
N_TSLOTS = 28

F32 = jnp.float32
BF16 = jnp.bfloat16


def _dot(a, b):
    return jnp.dot(a, b, preferred_element_type=F32)


def _dot_nt(a, b):
    return lax.dot_general(a, b, (((1,), (1,)), ((), ())), preferred_element_type=F32)


def _split_bf16(x):
    hi = x.astype(BF16)
    lo = (x - hi.astype(F32)).astype(BF16)
    return hi, lo


def _rmsnorm(x, g):
    ms = jnp.mean(x * x, axis=-1, keepdims=True)
    return x * lax.rsqrt(ms + RMS_EPS) * g


def _dot_tn(a_t, b):
    return lax.dot_general(a_t, b, (((0,), (0,)), ((), ())), preferred_element_type=F32)


def _ffn_body(x_ref, g_ref, wg_ref, wu_ref, wd_ref, gf_ref, *rest, final_norm, n_chunks, with_mixer):
    x = x_ref[...]
    if with_mixer:
        oa_ref, ob_ref, woa_ref, wob_ref, o_ref = rest
        x = x + _dot_tn(oa_ref[0], woa_ref[...]) + _dot_tn(ob_ref[0], wob_ref[...])
    else:
        (o_ref,) = rest
    h = _rmsnorm(x, g_ref[...]).astype(BF16)
    tf = D_FF // n_chunks
    acc = None
    for c in range(n_chunks):
        a = _dot(h, wg_ref[:, c * tf:(c + 1) * tf])
        u = _dot(h, wu_ref[:, c * tf:(c + 1) * tf])
        z = ((a * jax.nn.sigmoid(a)) * u).astype(BF16)
        part = _dot(z, wd_ref[c * tf:(c + 1) * tf, :])
        acc = part if acc is None else acc + part
    y = x + 0.5 * acc
    if final_norm:
        y = _rmsnorm(y, gf_ref[...])
    o_ref[...] = y


def _ffn(x2d, g, wg, wu, wd, gf, *, final_norm, mixer=None, tm=512, n_chunks=2):
    n = x2d.shape[0]
    resident = lambda shape: pl.BlockSpec(shape, lambda i: (0, 0), pipeline_mode=pl.Buffered(1))
    in_specs = [
        pl.BlockSpec((tm, D_MODEL), lambda i: (i, 0)),
        resident((1, D_MODEL)),
        resident((D_MODEL, D_FF)), resident((D_MODEL, D_FF)), resident((D_FF, D_MODEL)),
        resident((1, D_MODEL)),
    ]
    args = [x2d, g, wg, wu, wd, gf]
    if mixer is not None:
        o_a, o_b, wo_a, wo_b = mixer
        spt = o_a.shape[2] // tm
        for o_t in (o_a, o_b):
            in_specs.append(pl.BlockSpec((1, o_t.shape[1], tm), lambda i: (i // spt, 0, i % spt)))
        in_specs += [resident(wo_a.shape), resident(wo_b.shape)]
        args += [o_a, o_b, wo_a, wo_b]
    return pl.pallas_call(
        functools.partial(_ffn_body, final_norm=final_norm, n_chunks=n_chunks, with_mixer=mixer is not None),
        grid=(n // tm,),
        in_specs=in_specs,
        out_specs=pl.BlockSpec((tm, D_MODEL), lambda i: (i, 0)),
        out_shape=jax.ShapeDtypeStruct((n, D_MODEL), F32),
        compiler_params=pltpu.CompilerParams(
            dimension_semantics=("parallel",), vmem_limit_bytes=VMEM_LIMIT),
        name="ffn_final" if final_norm else "ffn",
    )(*args)


def _inproj_body(x_ref, g_ref, wk_ref, wt_ref, ones_ref, wgt_ref, pat_ref, pk_ref, pt_ref, gates_ref):
    h = _rmsnorm(x_ref[...], g_ref[...]).astype(BF16)
    pk = _dot(h, wk_ref[...])
    for slot in range(N_KSLOTS):
        piece = pk[:, slot * HEAD_PAD:(slot + 1) * HEAD_PAD]
        if KSLOT_KA <= slot < KSLOT_KA + MOBA_HEADS:
            piece = piece + pat_ref[:, :HEAD_PAD]
        elif KSLOT_KS <= slot < KSLOT_KS + NSA_GROUPS:
            piece = piece + pat_ref[:, HEAD_PAD:]
        pk_ref[:, slot * HEAD_PAD:(slot + 1) * HEAD_PAD] = piece.astype(BF16)
    pt_ref[0] = (_dot_nt(wt_ref[...], h) + ones_ref[...]).astype(BF16)
    gates_ref[0] = jax.nn.sigmoid(_dot_nt(wgt_ref[...], h))


def _inproj(x2d, g, wk, wt, ones_col, wgt, pat, b, s, *, tm=256):
    n = x2d.shape[0]
    nk = wk.shape[1]
    nt = wt.shape[0]
    ng = wgt.shape[0]
    spt = s // tm
    const = lambda shape: pl.BlockSpec(shape, lambda i: (0, 0))
    return pl.pallas_call(
        _inproj_body,
        grid=(n // tm,),
        in_specs=[
            pl.BlockSpec((tm, D_MODEL), lambda i: (i, 0)),
            const((1, D_MODEL)), const((D_MODEL, nk)), const((nt, D_MODEL)), const((nt, 1)), const((ng, D_MODEL)),
            pl.BlockSpec((tm, 2 * HEAD_PAD), lambda i: (i % spt, 0)),
        ],
        out_specs=[
            pl.BlockSpec((tm, nk), lambda i: (i, 0)),
            pl.BlockSpec((1, nt, tm), lambda i: (i // spt, 0, i % spt)),
            pl.BlockSpec((1, ng, tm), lambda i: (i // spt, 0, i % spt)),
        ],
        out_shape=[jax.ShapeDtypeStruct((n, nk), BF16), jax.ShapeDtypeStruct((b, nt, s), BF16),
                   jax.ShapeDtypeStruct((b, ng, s), F32)],
        compiler_params=pltpu.CompilerParams(
            dimension_semantics=("parallel",), vmem_limit_bytes=VMEM_LIMIT),
        name="inproj",
    )(x2d, g, wk, wt, ones_col, wgt, pat)


def _compress_body(kraw_ref, vraw_ref, pos_ref, w1_ref, w2k_ref, w2vt_ref, kc_ref, vct_ref, xs_ref, *, ncp):
    half = CMP_LEN // 2
    for kv, raw_ref in enumerate((kraw_ref, vraw_ref)):
        xs_ref[...] = raw_ref[0].astype(F32)
        y0 = jnp.zeros((ncp, CMP_HIDDEN), F32)
        y1 = jnp.zeros((ncp, CMP_HIDDEN), F32)
        for p in range(half):
            xp = xs_ref[pl.ds(p, ncp, stride=CMP_STRIDE), :]
            y0 = y0 + _dot((xp + pos_ref[kv, p:p + 1, :]).astype(BF16), w1_ref[kv, p])
            y1 = y1 + _dot((xp + pos_ref[kv, half + p:half + p + 1, :]).astype(BF16), w1_ref[kv, half + p])
        h1 = y0 + pltpu.roll(y1, ncp - 1, 0)
        act = (h1 * jax.nn.sigmoid(h1)).astype(BF16)
        if kv == 0:
            kc_ref[0, 0] = _dot(act, w2k_ref[...]).astype(BF16)
        else:
            out_t = _dot_nt(w2vt_ref[...], act)
            row = lax.broadcasted_iota(jnp.int32, out_t.shape, 0)
            vct_ref[0, 0] = (out_t + jnp.where(row == ONES_ROW, 1.0, 0.0)).astype(BF16)


def _compress(pk, pos, w1, w2k, w2vt):
    b, s, _ = pk.shape
    ncp = s // CMP_STRIDE
    full = lambda shape: pl.BlockSpec(shape, lambda bi, g: (0,) * len(shape))
    return pl.pallas_call(
        functools.partial(_compress_body, ncp=ncp),
        grid=(b, NSA_GROUPS),
        in_specs=[
            pl.BlockSpec((1, s, HEAD_PAD), lambda bi, g: (bi, 0, KSLOT_KC + g)),
            pl.BlockSpec((1, s, HEAD_PAD), lambda bi, g: (bi, 0, KSLOT_VC + g)),
            full((2, CMP_LEN, HEAD_PAD)),
            full((2, CMP_LEN, HEAD_PAD, CMP_HIDDEN)),
            full((CMP_HIDDEN, HEAD_PAD)),
            full((HEAD_PAD, CMP_HIDDEN)),
        ],
        out_specs=[
            pl.BlockSpec((1, 1, ncp, HEAD_PAD), lambda bi, g: (bi, g, 0, 0)),
            pl.BlockSpec((1, 1, HEAD_PAD, ncp), lambda bi, g: (bi, g, 0, 0)),
        ],
        out_shape=[
            jax.ShapeDtypeStruct((b, NSA_GROUPS, ncp, HEAD_PAD), BF16),
            jax.ShapeDtypeStruct((b, NSA_GROUPS, HEAD_PAD, ncp), BF16),
        ],
        scratch_shapes=[pltpu.VMEM((s, HEAD_PAD), F32)],
        compiler_params=pltpu.CompilerParams(
            dimension_semantics=("parallel", "arbitrary"), vmem_limit_bytes=VMEM_LIMIT),
        name="compress",
    )(pk, pk, pos, w1, w2k, w2vt)


def _flash_update(s_t, shift, pv_fn, m_ref, acc_ref):
    m_prev = m_ref[...]
    m_new = jnp.maximum(m_prev, jnp.max(s_t, axis=0, keepdims=True) + shift)
    alpha = jnp.exp2(m_prev - m_new)
    p = jnp.exp2(s_t - (m_new - shift)).astype(BF16)
    acc_ref[...] = alpha * acc_ref[...] + pv_fn(p)
    m_ref[...] = m_new


def _state_lanes(i):
    return pl.ds(pl.multiple_of(i * WIDE, WIDE), WIDE)


def _tile_phase(count, tile_fn, scores_fn, step_fn, sa_ref, sb_ref):
    if count == 0:
        return
    tile = lambda n: tile_fn(jnp.minimum(jnp.asarray(n, jnp.int32), count - 1))
    sa_ref[...] = scores_fn(*tile(0))

    def pair(p, carry):
        n = 2 * p
        sb_ref[...] = scores_fn(*tile(n + 1))
        step_fn(sa_ref, *tile(n))
        sa_ref[...] = scores_fn(*tile(n + 2))
        step_fn(sb_ref, *tile(n + 1))
        return carry

    lax.fori_loop(0, count // 2, pair, 0, unroll=4)
    if count % 2:
        step_fn(sa_ref, *tile(count - 1))


def _flat_stream(nq, n_far, far_i_ref, far_j_ref, scores_fn, pv_fn, bias_ref, shift_far,
                 sa_ref, sb_ref, m_ref, acc_ref):
    zero = jnp.zeros((1, WIDE), F32)

    def own_step(s_ref, i, j):
        s_t = s_ref[...] + bias_ref[0, 2 * TQ:3 * TQ, :]
        m = jnp.max(s_t, axis=0, keepdims=True)
        m_ref[:, _state_lanes(i)] = m
        acc_ref[:, _state_lanes(i)] = pv_fn(j)(jnp.exp2(s_t - m).astype(BF16))

    def prev_step(s_ref, i, j):
        _flash_update(s_ref[...] + bias_ref[0, TQ:2 * TQ, :], zero, pv_fn(j),
                      m_ref.at[:, _state_lanes(i)], acc_ref.at[:, _state_lanes(i)])

    def far_step(s_ref, i, j):
        _flash_update(s_ref[...], shift_far, pv_fn(j), m_ref.at[:, _state_lanes(i)], acc_ref.at[:, _state_lanes(i)])

    _tile_phase(nq, lambda n: (n, n), scores_fn, own_step, sa_ref, sb_ref)
    _tile_phase(nq - 1, lambda n: (n + 1, n), scores_fn, prev_step, sa_ref, sb_ref)
    _tile_phase(n_far, lambda n: (far_i_ref[n], far_j_ref[n]), scores_fn, far_step, sa_ref, sb_ref)


def _lane_block(x, r):
    return x[:, r * TQ:(r + 1) * TQ]


def _shift_row(c31_ref, head0):
    return jnp.concatenate([jnp.full((1, TQ), c31_ref[head0 + r], F32) for r in range(HG)], axis=1)


def _rank_select(score, n_rows, top, row_weight=None):
    if row_weight is not None:
        blk = lax.broadcasted_iota(jnp.int32, score.shape, 0)
        cnt = jnp.zeros(score.shape, F32)
        for j in range(n_rows):
            row = score[j:j + 1, :]
            ahead = jnp.where(row > score, 1.0, jnp.where((row == score) & (blk > j), 1.0, 0.0))
            cnt = cnt + ahead * row_weight(j)
        return cnt < top
    sub = 8
    n_groups = score.shape[0] // sub
    groups = [score[g * sub:(g + 1) * sub] for g in range(n_groups)]
    counts = [jnp.zeros(groups[0].shape, F32) for _ in range(n_groups)]
    local = lax.broadcasted_iota(jnp.int32, groups[0].shape, 0)
    for j in range(n_rows):
        row = score[j:j + 1, :]
        weight = None if row_weight is None else row_weight(j)
        for g in range(n_groups):
            one = 1.0 if weight is None else weight[g * sub:(g + 1) * sub]
            if j < g * sub:
                ahead = jnp.where(row >= groups[g], one, 0.0)
            elif j >= (g + 1) * sub:
                ahead = jnp.where(row > groups[g], one, 0.0)
            else:
                ahead = jnp.where(local > j - g * sub, jnp.where(row >= groups[g], one, 0.0),
                                  jnp.where(row > groups[g], one, 0.0))
            counts[g] = counts[g] + ahead
    return jnp.concatenate(counts, axis=0) < top


def _moba_body(c31_ref, far_i_ref, far_j_ref, q_ref, k_ref, vt_ref, bias_ref, o_ref,
               kmean_ref, qx_ref, sa_ref, sb_ref, m_ref, acc_ref, *, nb, n_far):
    hg = pl.program_id(1)
    s = q_ref.shape[2]
    nbp = -(-nb // 16) * 16

    kmean_ref[...] = jnp.zeros_like(kmean_ref)
    blk = lax.broadcasted_iota(jnp.int32, (nbp, s), 0)
    cur = lax.broadcasted_iota(jnp.int32, (nbp, s), 1) // MOBA_BLOCK
    for r in range(HG):
        for n in range(nb):
            kb = k_ref[0, n * MOBA_BLOCK:(n + 1) * MOBA_BLOCK, r * HEAD_PAD:(r + 1) * HEAD_PAD].astype(F32)
            kmean_ref[r, n:n + 1, :] = jnp.mean(kb, axis=0, keepdims=True)
        q_t = q_ref[0, r * HEAD_PAD:(r + 1) * HEAD_PAD, :]
        kh, kl = _split_bf16(kmean_ref[r])
        gate = (_dot(kh, q_t) + _dot(kl, q_t))[:nbp]
        top = _rank_select(gate, nb, MOBA_TOPK, lambda n: jnp.where(n < cur, 1.0, 0.0))
        sel = ((blk < cur) & top) | (blk == cur)
        mask_rows = jnp.where(sel, 0.0, NEG).astype(BF16)
        pad_rows = jnp.zeros((HEAD_PAD - EXT0 - nbp, s), BF16)
        qx_ref[r] = jnp.concatenate([q_t[:EXT0], mask_rows, pad_rows], axis=0)

    def scores(i, j):
        qoff = pl.multiple_of(i * TQ, TQ)
        koff = pl.multiple_of(j * TQ, TQ)
        return jnp.concatenate(
            [_dot(k_ref[0, pl.ds(koff, TQ), r * HEAD_PAD:(r + 1) * HEAD_PAD], qx_ref[r, :, pl.ds(qoff, TQ)])
             for r in range(HG)], axis=1)

    def pv_fn(j):
        off = pl.multiple_of(j * TQ, TQ)

        def fn(p):
            return jnp.concatenate(
                [_dot(vt_ref[0, r * HEAD_PAD:r * HEAD_PAD + ACC_ROWS, pl.ds(off, TQ)], _lane_block(p, r))
                 for r in range(HG)], axis=1)
        return fn

    _flat_stream(nb, n_far, far_i_ref, far_j_ref, scores, pv_fn, bias_ref, _shift_row(c31_ref, hg * HG),
                 sa_ref, sb_ref, m_ref, acc_ref)

    def finish(i, carry):
        acc = acc_ref[:, _state_lanes(i)]
        out = (acc[:HEAD_DIM] / acc[ONES_ROW:ONES_ROW + 1, :]).astype(BF16)
        for r in range(HG):
            o_ref[0, r * HEAD_DIM:(r + 1) * HEAD_DIM, pl.ds(pl.multiple_of(i * TQ, TQ), TQ)] = _lane_block(out, r)
        return carry

    lax.fori_loop(0, nb, finish, 0)


def _moba(pk, pt, bias_win, c31, far_i, far_j):
    b, s, _ = pk.shape
    nb = s // MOBA_BLOCK
    hw = HG * HEAD_PAD
    smem = pl.BlockSpec(memory_space=pltpu.SMEM)
    return pl.pallas_call(
        functools.partial(_moba_body, nb=nb, n_far=far_i.shape[0]),
        grid=(b, MOBA_HEADS // HG),
        in_specs=[
            smem, smem, smem,
            pl.BlockSpec((1, hw, s), lambda bi, hg: (bi, TSLOT_QA // HG + hg, 0)),
            pl.BlockSpec((1, s, hw), lambda bi, hg: (bi, 0, KSLOT_KA // HG + hg)),
            pl.BlockSpec((1, hw, s), lambda bi, hg: (bi, TSLOT_VA // HG + hg, 0)),
            pl.BlockSpec((1, 3 * TQ, WIDE), lambda bi, hg: (hg, 0, 0)),
        ],
        out_specs=pl.BlockSpec((1, HG * HEAD_DIM, s), lambda bi, hg: (bi, hg, 0)),
        out_shape=jax.ShapeDtypeStruct((b, MOBA_HEADS * HEAD_DIM, s), BF16),
        scratch_shapes=[
            pltpu.VMEM((HG, HEAD_PAD, HEAD_PAD), F32),
            pltpu.VMEM((HG, HEAD_PAD, s), BF16),
            pltpu.VMEM((TQ, WIDE), F32),
            pltpu.VMEM((TQ, WIDE), F32),
            pltpu.VMEM((1, nb * WIDE), F32),
            pltpu.VMEM((ACC_ROWS, nb * WIDE), F32),
        ],
        compiler_params=pltpu.CompilerParams(
            dimension_semantics=("parallel", "arbitrary"), vmem_limit_bytes=VMEM_LIMIT),
        name="moba",
    )(c31, far_i, far_j, pt, pk, pt, bias_win)


def _nsa_body(c31_ref, far_i_ref, far_j_ref, q_ref, ks_ref, kw_ref, vst_ref, vwt_ref, kc_ref, vct_ref,
              gates_ref, bias_ref, bcmp_ref, ovt_ref, o_ref,
              qx_ref, sa_ref, sb_ref, m_ref, accs_ref, part_ref, *, ncp, nslc, nq, n_far):
    g = pl.program_id(1)

    def front(i, n_win):
        qoff = pl.multiple_of(i * TQ, TQ)
        q4 = jnp.concatenate([q_ref[0, r * HEAD_PAD:(r + 1) * HEAD_PAD, pl.ds(qoff, TQ)] for r in range(HG)],
                             axis=1)

        band_off = pl.multiple_of((nq - 1 - i) * (TQ // CMP_STRIDE), TQ // CMP_STRIDE)
        s_t = _dot(kc_ref[0, 0], q4) + bcmp_ref[0, pl.ds(band_off, ncp), :]
        m = jnp.max(s_t, axis=0, keepdims=True)
        p = jnp.where(s_t > 0.5 * NEG, jnp.exp2(s_t - m), 0.0)
        den = jnp.sum(p, axis=0, keepdims=True)
        pn = p / jnp.where(den > 0.0, den, 1.0)
        o_cmp = _dot(vct_ref[0, 0, :ACC_ROWS, :], pn.astype(BF16))
        psum = _lane_block(pn, 0)
        for r in range(1, HG):
            psum = psum + _lane_block(pn, r)

        ph, plo = _split_bf16(psum)
        imp = (_dot(ovt_ref[...], ph) + _dot(ovt_ref[...], plo))[:EXT0]
        blk = lax.broadcasted_iota(jnp.int32, imp.shape, 0)
        cur = (i * TQ + lax.broadcasted_iota(jnp.int32, imp.shape, 1)) // SLC_BLOCK
        forced = (blk == 0) | (blk == cur) | (blk == cur - 1)
        score = jnp.where(forced, 1e30, jnp.where(blk <= cur, imp, -1.0))
        score = jnp.where(blk < nslc, score, -2.0)
        sel = _rank_select(score, nslc, SLC_TOPN) & (blk < nslc)
        mask_rows = jnp.where(sel, 0.0, NEG).astype(BF16)
        qx_ref[:, _state_lanes(i)] = jnp.concatenate(
            [q4[:EXT0], jnp.concatenate([mask_rows] * HG, axis=1)], axis=0)

        start = pl.multiple_of((i + 1 - n_win) * TQ, TQ)
        s_w = _dot(kw_ref[0, pl.ds(start, n_win * TQ), :], q4) + bias_ref[0, (3 - n_win) * TQ:, :]
        p_w = jnp.exp2(s_w - jnp.max(s_w, axis=0, keepdims=True)).astype(BF16)
        acc_w = _dot(vwt_ref[0, :ACC_ROWS, pl.ds(start, n_win * TQ)], p_w)
        o_win = acc_w / acc_w[ONES_ROW:ONES_ROW + 1, :]

        gt = gates_ref[0, :, pl.ds(qoff, TQ)]
        part_ref[:, _state_lanes(i)] = jnp.concatenate(
            [gt[N_GATES * r:N_GATES * r + 1, :] * _lane_block(o_cmp, r)
             + gt[N_GATES * r + 2:N_GATES * r + 3, :] * _lane_block(o_win, r) for r in range(HG)], axis=1)

    for i0 in range(min(2, nq)):
        front(jnp.asarray(i0, jnp.int32), i0 + 1)

    def front_loop(i, carry):
        front(i, 3)
        return carry

    lax.fori_loop(2, nq, front_loop, 0)

    def scores(i, j):
        return _dot(ks_ref[0, pl.ds(pl.multiple_of(j * TQ, TQ), TQ), :], qx_ref[:, _state_lanes(i)])

    def pv_fn(j):
        off = pl.multiple_of(j * TQ, TQ)
        return lambda p: _dot(vst_ref[0, :ACC_ROWS, pl.ds(off, TQ)], p)

    _flat_stream(nq, n_far, far_i_ref, far_j_ref, scores, pv_fn, bias_ref,
                 _shift_row(c31_ref, MOBA_HEADS + g * HG), sa_ref, sb_ref, m_ref, accs_ref)

    def finish(i, carry):
        qoff = pl.multiple_of(i * TQ, TQ)
        acc = accs_ref[:, _state_lanes(i)]
        o_slc = acc / acc[ONES_ROW:ONES_ROW + 1, :]
        part = part_ref[:, _state_lanes(i)]
        gt = gates_ref[0, :, pl.ds(qoff, TQ)]
        for r in range(HG):
            out = _lane_block(part, r) + gt[N_GATES * r + 1:N_GATES * r + 2, :] * _lane_block(o_slc, r)
            o_ref[0, r * HEAD_DIM:(r + 1) * HEAD_DIM, pl.ds(qoff, TQ)] = out[:HEAD_DIM].astype(BF16)
        return carry

    lax.fori_loop(0, nq, finish, 0)


def _nsa(pk, pt, kc, vct, gates_t, bias_win, bias_cmp, ovt, c31, far_i, far_j):
    b, s, _ = pk.shape
    nq = s // TQ
    ncp = s // CMP_STRIDE
    nslc = s // SLC_BLOCK
    hw = HG * HEAD_PAD
    band_rows = bias_cmp.shape[1]
    hgrp = MOBA_HEADS // HG
    smem = pl.BlockSpec(memory_space=pltpu.SMEM)
    k_spec = lambda slot: pl.BlockSpec((1, s, HEAD_PAD), lambda bi, g: (bi, 0, slot + g))
    vt_spec = lambda slot: pl.BlockSpec((1, HEAD_PAD, s), lambda bi, g: (bi, slot + g, 0))
    bias_spec = lambda rows: pl.BlockSpec((1, rows, WIDE), lambda bi, g: (hgrp + g, 0, 0),
                                          pipeline_mode=pl.Buffered(1))
    return pl.pallas_call(
        functools.partial(_nsa_body, ncp=ncp, nslc=nslc, nq=nq, n_far=far_i.shape[0]),
        grid=(b, NSA_GROUPS),
        in_specs=[
            smem, smem, smem,
            pl.BlockSpec((1, hw, s), lambda bi, g: (bi, TSLOT_QB // HG + g, 0)),
            k_spec(KSLOT_KS), k_spec(KSLOT_KW), vt_spec(TSLOT_VS), vt_spec(TSLOT_VW),
            pl.BlockSpec((1, 1, ncp, HEAD_PAD), lambda bi, g: (bi, g, 0, 0)),
            pl.BlockSpec((1, 1, HEAD_PAD, ncp), lambda bi, g: (bi, g, 0, 0)),
            pl.BlockSpec((1, GATE_ROWS, s), lambda bi, g: (bi, g, 0)),
            bias_spec(3 * TQ), bias_spec(band_rows),
            pl.BlockSpec((HEAD_PAD, ncp), lambda bi, g: (0, 0)),
        ],
        out_specs=pl.BlockSpec((1, HG * HEAD_DIM, s), lambda bi, g: (bi, g, 0)),
        out_shape=jax.ShapeDtypeStruct((b, NSA_HEADS * HEAD_DIM, s), BF16),
        scratch_shapes=[
            pltpu.VMEM((HEAD_PAD, nq * WIDE), BF16),
            pltpu.VMEM((TQ, WIDE), F32),
            pltpu.VMEM((TQ, WIDE), F32),
            pltpu.VMEM((1, nq * WIDE), F32),
            pltpu.VMEM((ACC_ROWS, nq * WIDE), F32),
            pltpu.VMEM((ACC_ROWS, nq * WIDE), F32),
        ],
        compiler_params=pltpu.CompilerParams(
            dimension_semantics=("parallel", "arbitrary"), vmem_limit_bytes=VMEM_LIMIT),
        name="nsa",
    )(c31, far_i, far_j, pt, pk, pk, pt, pt, kc, vct, gates_t, bias_win, bias_cmp, ovt)


def _rel_bucket(dist):
    n = jnp.maximum(dist, 0)
    nf = jnp.maximum(n, 1).astype(F32)
    large = REL_MAX_EXACT + (jnp.log(nf / REL_MAX_EXACT) / math.log(REL_MAX_DIST / REL_MAX_EXACT)
                             * (REL_BUCKETS - REL_MAX_EXACT)).astype(jnp.int32)
    return jnp.where(n < REL_MAX_EXACT, n, jnp.minimum(large, REL_BUCKETS - 1))


def _bias_body(tab_ref, *refs):
    h = pl.program_id(0)
    n_tab = len(refs) // 2
    for bk_ref, out_ref in zip(refs[:n_tab], refs[n_tab:]):
        bk = bk_ref[...]
        acc = jnp.full(bk.shape, NEG, F32)
        for bucket in range(REL_BUCKETS):
            acc = jnp.where(bk == bucket, tab_ref[h, bucket], acc)
        out_ref[0] = acc


def _bias_tables(rel_bias, ncp, nq):
    tab = rel_bias.T.astype(F32) * LOG2E
    n_heads = tab.shape[0]
    c = jnp.arange(TQ)[:, None]
    r = jnp.arange(TQ)[None, :]
    d_own = r - c
    bk_own = jnp.where(d_own >= 0, _rel_bucket(d_own), -1)
    bk_prev = _rel_bucket(d_own + TQ)
    bk_win2 = jnp.where(d_own + 2 * TQ < WINDOW, _rel_bucket(d_own + 2 * TQ), -1)
    band_rows = ncp + (nq - 1) * (TQ // CMP_STRIDE)
    u = jnp.arange(band_rows)[:, None] - (nq - 1) * (TQ // CMP_STRIDE)
    d_cmp = r - CMP_STRIDE * u - (CMP_LEN - 1)
    bk_cmp = jnp.where(d_cmp >= 0, _rel_bucket(d_cmp), -1)
    tables = [t.astype(jnp.int32) for t in (jnp.concatenate([bk_win2, bk_prev, bk_own], axis=0), bk_cmp)]
    outs = pl.pallas_call(
        _bias_body,
        grid=(n_heads,),
        in_specs=[pl.BlockSpec(memory_space=pltpu.SMEM)]
        + [pl.BlockSpec(t.shape, lambda h: (0, 0)) for t in tables],
        out_specs=[pl.BlockSpec((1, t.shape[0], TQ), lambda h: (h // HG, 0, h % HG)) for t in tables],
        out_shape=[jax.ShapeDtypeStruct((n_heads // HG, t.shape[0], WIDE), F32) for t in tables],
        compiler_params=pltpu.CompilerParams(dimension_semantics=("parallel",), vmem_limit_bytes=VMEM_LIMIT),
        name="bias_tiles",
    )(tab, *tables)
    return outs, tab[:, REL_BUCKETS - 1]


def _pad_heads(w, n_heads):
    rows = w.shape[0]
    w = w.reshape(rows, n_heads, HEAD_DIM)
    w = jnp.pad(w, ((0, 0), (0, 0), (0, HEAD_PAD - HEAD_DIM)))
    return w.reshape(rows, n_heads * HEAD_PAD)


def _slc_overlap_t(ncp, nslc):
    ratio = SLC_BLOCK // CMP_STRIDE
    mc = CMP_LEN // CMP_STRIDE
    w = np.zeros((HEAD_PAD, nslc * ratio + mc), np.float32)
    for j in range(nslc):
        for a in range(ratio):
            for m in range(mc):
                w[j, ratio * j + a + m] += 1.0
    n_cmp = ncp - 1
    out = np.zeros((HEAD_PAD, ncp), np.float32)
    out[:, :n_cmp] = w[:, :n_cmp]
    return out


def _block_onehot(s, block):
    pat = np.zeros((s, HEAD_PAD), np.float32)
    pat[np.arange(s), EXT0 + np.arange(s) // block] = 1.0
    return pat


def kernel(x, norm_ffn1, w_ffn1_gate, w_ffn1_up, w_ffn1_down, norm_mix, w_in, cmp_pos_k, cmp_w1_k, cmp_w2_k,
           cmp_pos_v, cmp_w1_v, cmp_w2_v, w_out, norm_ffn2, w_ffn2_gate, w_ffn2_up, w_ffn2_down, rel_bias,
           norm_final):
    b, s, d = x.shape
    assert d == D_MODEL and s % (2 * TQ) == 0 and norm_ffn1.shape[0] == 1
    assert s // SLC_BLOCK <= HEAD_PAD - EXT0, "selection-mask rows must fit beside the 64 head dims"
    n = b * s
    nq = s // TQ
    ncp = s // CMP_STRIDE
    nslc = s // SLC_BLOCK
    scale = HEAD_DIM ** -0.5 * LOG2E

    sizes = (512, 512, 512, 512) + (128,) * 6 + (24,)
    splits = np.cumsum(sizes)[:-1].tolist()
    wqa, wka, wva, wqb, wkc, wvc, wks, wvs, wkw, wvw, wgb = jnp.split(w_in[0], splits, axis=1)
    w_k = jnp.concatenate([_pad_heads(wka, 8), _pad_heads(wkc, 2), _pad_heads(wvc, 2), _pad_heads(wks, 2),
                           _pad_heads(wkw, 2)], axis=1).astype(BF16)
    w_t = jnp.concatenate([_pad_heads(wqa * scale, 8), _pad_heads(wva, 8), _pad_heads(wqb * scale, 8),
                           _pad_heads(wvs, 2), _pad_heads(wvw, 2)], axis=1).T.astype(BF16)
    ones_np = np.zeros((N_TSLOTS * HEAD_PAD, 1), np.float32)
    for slot in list(range(TSLOT_VA, TSLOT_VA + 8)) + [TSLOT_VS, TSLOT_VS + 1, TSLOT_VW, TSLOT_VW + 1]:
        ones_np[slot * HEAD_PAD + ONES_ROW, 0] = 1.0
    ones_col = jnp.asarray(ones_np)
    wgt = wgb.reshape(D_MODEL, NSA_GROUPS, NSA_REP * N_GATES)
    wgt = jnp.pad(wgt, ((0, 0), (0, 0), (0, GATE_ROWS - NSA_REP * N_GATES)))
    wgt = wgt.reshape(D_MODEL, NSA_GROUPS * GATE_ROWS).T.astype(BF16)

    def cmp_params(pos, w1, w2):
        pos_p = jnp.pad(pos[0], ((0, 0), (0, HEAD_PAD - HEAD_DIM)))
        w1_p = jnp.pad(w1[0].reshape(CMP_LEN, HEAD_DIM, CMP_HIDDEN), ((0, 0), (0, HEAD_PAD - HEAD_DIM), (0, 0)))
        w2_p = jnp.pad(w2[0], ((0, 0), (0, HEAD_PAD - HEAD_DIM)))
        return pos_p, w1_p.astype(BF16), w2_p.astype(BF16)

    pos_k, w1k, w2k = cmp_params(cmp_pos_k, cmp_w1_k, cmp_w2_k)
    pos_v, w1v, w2v = cmp_params(cmp_pos_v, cmp_w1_v, cmp_w2_v)
    cmp_pos = jnp.stack([pos_k, pos_v])
    cmp_w1 = jnp.stack([w1k, w1v])

    wo = w_out[0]
    wo_a = wo[:MOBA_HEADS * HEAD_DIM].astype(BF16)
    wo_b = wo[MOBA_HEADS * HEAD_DIM:].astype(BF16)

    (bias_win, bias_cmp), c31 = _bias_tables(rel_bias, ncp, nq)
    ovt = jnp.asarray(_slc_overlap_t(ncp, nslc)).astype(BF16)
    pat = jnp.asarray(np.concatenate([_block_onehot(s, MOBA_BLOCK), _block_onehot(s, SLC_BLOCK)], axis=1))
    far = [(i, j) for j in range(nq) for i in range(j + 2, nq)]
    far_i = jnp.asarray(np.array([i for i, _ in far], np.int32))
    far_j = jnp.asarray(np.array([j for _, j in far], np.int32))

    x2d = x.reshape(n, D_MODEL)
    nf = norm_final.reshape(1, D_MODEL)
    x1 = _ffn(x2d, norm_ffn1, w_ffn1_gate[0].astype(BF16), w_ffn1_up[0].astype(BF16),
              w_ffn1_down[0].astype(BF16), nf, final_norm=False)
    pk, pt, gates_t = _inproj(x1, norm_mix, w_k, w_t, ones_col, wgt, pat, b, s)
    pk = pk.reshape(b, s, N_KSLOTS * HEAD_PAD)
    kc, vct = _compress(pk, cmp_pos, cmp_w1, w2k, w2v.T)
    o_a = _moba(pk, pt, bias_win, c31, far_i, far_j)
    o_b = _nsa(pk, pt, kc, vct, gates_t, bias_win, bias_cmp, ovt, c31, far_i, far_j)
    y = _ffn(x1, norm_ffn2, w_ffn2_gate[0].astype(BF16), w_ffn2_up[0].astype(BF16),
             w_ffn2_down[0].astype(BF16), nf, final_norm=True, mixer=(o_a, o_b, wo_a, wo_b))
    return y.reshape(b, s, D_MODEL)
```

```python
import functools
import math

import numpy as np
import jax
import jax.numpy as jnp
from jax import lax
from jax.experimental import pallas as pl
from jax.experimental.pallas import tpu as pltpu

D_MODEL = 1024
D_FF = 2816
HEAD_DIM = 64
HEAD_PAD = 128
ONES_ROW = HEAD_DIM
EXT0 = HEAD_DIM
MOBA_HEADS = 8
NSA_HEADS = 8
NSA_GROUPS = 2
NSA_REP = NSA_HEADS // NSA_GROUPS
HG = 4
MOBA_BLOCK = 256
MOBA_TOPK = 3
CMP_LEN = 32
CMP_STRIDE = 16
CMP_HIDDEN = 256
SLC_BLOCK = 64
SLC_TOPN = 16
WINDOW = 512
N_GATES = 3
GATE_ROWS = 16
REL_BUCKETS = 32
REL_MAX_EXACT = REL_BUCKETS // 2
REL_MAX_DIST = 128
RMS_EPS = 1e-6
TQ = 256
WIDE = HG * TQ
NEG = -1e30
LOG2E = 1.4426950408889634
ACC_ROWS = 80
VMEM_LIMIT = 56 * 1024 * 1024

KSLOT_KA, KSLOT_KC, KSLOT_VC, KSLOT_KS, KSLOT_KW = 0, 4, 5, 6, 7
N_KSLOTS = 8
TROW_QA, TROW_VA, TROW_QB, TROW_VS, TROW_VW = 0, 8, 16, 24, 26
N_TROWS = 28

F32 = jnp.float32
BF16 = jnp.bfloat16


def _dot(a, b):
    return jnp.dot(a, b, preferred_element_type=F32)


def _dot_nt(a, b):
    return lax.dot_general(a, b, (((1,), (1,)), ((), ())), preferred_element_type=F32)


def _split_bf16(x):
    hi = x.astype(BF16)
    lo = (x - hi.astype(F32)).astype(BF16)
    return hi, lo


def _rmsnorm(x, g):
    ms = jnp.mean(x * x, axis=-1, keepdims=True)
    return x * lax.rsqrt(ms + RMS_EPS) * g


def _dot_tn(a_t, b):
    return lax.dot_general(a_t, b, (((0,), (0,)), ((), ())), preferred_element_type=F32)


def _ffn_body(x_ref, g_ref, wg_ref, wu_ref, wd_ref, gf_ref, *rest, final_norm, n_chunks, with_mixer):
    x = x_ref[...]
    if with_mixer:
        oa_ref, ob_ref, woa_ref, wob_ref, o_ref = rest
        x = x + _dot_tn(oa_ref[0], woa_ref[...]) + _dot_tn(ob_ref[0], wob_ref[...])
    else:
        (o_ref,) = rest
    h = _rmsnorm(x, g_ref[...]).astype(BF16)
    tf = D_FF // n_chunks
    acc = None
    for c in range(n_chunks):
        a = _dot(h, wg_ref[:, c * tf:(c + 1) * tf])
        u = _dot(h, wu_ref[:, c * tf:(c + 1) * tf])
        z = ((a * jax.nn.sigmoid(a)) * u).astype(BF16)
        part = _dot(z, wd_ref[c * tf:(c + 1) * tf, :])
        acc = part if acc is None else acc + part
    y = x + 0.5 * acc
    if final_norm:
        y = _rmsnorm(y, gf_ref[...])
    o_ref[...] = y


def _ffn(x2d, g, wg, wu, wd, gf, *, final_norm, mixer=None, tm=512, n_chunks=2):
    n = x2d.shape[0]
    resident = lambda shape: pl.BlockSpec(shape, lambda i: (0, 0), pipeline_mode=pl.Buffered(1))
    in_specs = [
        pl.BlockSpec((tm, D_MODEL), lambda i: (i, 0)),
        resident((1, D_MODEL)),
        resident((D_MODEL, D_FF)), resident((D_MODEL, D_FF)), resident((D_FF, D_MODEL)),
        resident((1, D_MODEL)),
    ]
    args = [x2d, g, wg, wu, wd, gf]
    if mixer is not None:
        o_a, o_b, wo_a, wo_b = mixer
        spt = o_a.shape[2] // tm
        for o_t in (o_a, o_b):
            in_specs.append(pl.BlockSpec((1, o_t.shape[1], tm), lambda i: (i // spt, 0, i % spt)))
        in_specs += [resident(wo_a.shape), resident(wo_b.shape)]
        args += [o_a, o_b, wo_a, wo_b]
    return pl.pallas_call(
        functools.partial(_ffn_body, final_norm=final_norm, n_chunks=n_chunks, with_mixer=mixer is not None),
        grid=(n // tm,),
        in_specs=in_specs,
        out_specs=pl.BlockSpec((tm, D_MODEL), lambda i: (i, 0)),
        out_shape=jax.ShapeDtypeStruct((n, D_MODEL), F32),
        compiler_params=pltpu.CompilerParams(
            dimension_semantics=("parallel",), vmem_limit_bytes=VMEM_LIMIT),
        name="ffn_final" if final_norm else "ffn",
    )(*args)


def _inproj_body(x_ref, g_ref, wk_ref, wt_ref, wgt_ref, pk_ref, pt_ref, gates_ref):
    h = _rmsnorm(x_ref[...], g_ref[...]).astype(BF16)
    pk_ref[...] = _dot(h, wk_ref[...]).astype(BF16)
    pt_ref[0] = _dot_nt(wt_ref[...], h).astype(BF16)
    gates_ref[0] = jax.nn.sigmoid(_dot_nt(wgt_ref[...], h))


def _inproj(x2d, g, wk, wt, wgt, b, s, *, tm=512):
    n = x2d.shape[0]
    nk = wk.shape[1]
    nt = wt.shape[0]
    ng = wgt.shape[0]
    spt = s // tm
    const = lambda shape: pl.BlockSpec(shape, lambda i: (0, 0))
    return pl.pallas_call(
        _inproj_body,
        grid=(n // tm,),
        in_specs=[
            pl.BlockSpec((tm, D_MODEL), lambda i: (i, 0)),
            const((1, D_MODEL)), const((D_MODEL, nk)), const((nt, D_MODEL)), const((ng, D_MODEL)),
        ],
        out_specs=[
            pl.BlockSpec((tm, nk), lambda i: (i, 0)),
            pl.BlockSpec((1, nt, tm), lambda i: (i // spt, 0, i % spt)),
            pl.BlockSpec((1, ng, tm), lambda i: (i // spt, 0, i % spt)),
        ],
        out_shape=[jax.ShapeDtypeStruct((n, nk), BF16), jax.ShapeDtypeStruct((b, nt, s), BF16),
                   jax.ShapeDtypeStruct((b, ng, s), F32)],
        compiler_params=pltpu.CompilerParams(
            dimension_semantics=("parallel",), vmem_limit_bytes=VMEM_LIMIT),
        name="inproj",
    )(x2d, g, wk, wt, wgt)


def _compress_body(kraw_ref, vraw_ref, pos_ref, w1_ref, w2k_ref, w2vt_ref, kc_ref, vct_ref, xs_ref, *, ncp):
    half = CMP_LEN // 2
    for kv, raw_ref in enumerate((kraw_ref, vraw_ref)):
        xs_ref[...] = raw_ref[0].astype(F32)
        y0 = jnp.zeros((ncp, CMP_HIDDEN), F32)
        y1 = jnp.zeros((ncp, CMP_HIDDEN), F32)
        for p in range(half):
            xp = xs_ref[pl.ds(p, ncp, stride=CMP_STRIDE), :]
            y0 = y0 + _dot((xp + pos_ref[kv, 0, p:p + 1, :]).astype(BF16), w1_ref[kv, 0, p])
            y1 = y1 + _dot((xp + pos_ref[kv, 0, half + p:half + p + 1, :]).astype(BF16), w1_ref[kv, 0, half + p])
        h1 = y0 + pltpu.roll(y1, ncp - 1, 0)
        act = (h1 * jax.nn.sigmoid(h1)).astype(BF16)
        if kv == 0:
            kc_ref[0, 0] = _dot(act, w2k_ref[...]).astype(BF16)
        else:
            out_t = _dot_nt(w2vt_ref[...], act)
            row = lax.broadcasted_iota(jnp.int32, out_t.shape, 0)
            vct_ref[0, 0] = (out_t + jnp.where(row == ONES_ROW, 1.0, 0.0)).astype(BF16)


def _compress(pk, pos, w1, w2k, w2vt):
    b, s, _ = pk.shape
    ncp = s // CMP_STRIDE
    full = lambda shape: pl.BlockSpec(shape, lambda bi, g: (0,) * len(shape))
    return pl.pallas_call(
        functools.partial(_compress_body, ncp=ncp),
        grid=(b, NSA_GROUPS),
        in_specs=[
            pl.BlockSpec((1, s, HEAD_PAD), lambda bi, g: (bi, 0, KSLOT_KC)),
            pl.BlockSpec((1, s, HEAD_PAD), lambda bi, g: (bi, 0, KSLOT_VC)),
            pl.BlockSpec((2, 1, CMP_LEN, HEAD_PAD), lambda bi, g: (0, g, 0, 0)),
            pl.BlockSpec((2, 1, CMP_LEN, HEAD_PAD, CMP_HIDDEN), lambda bi, g: (0, g, 0, 0, 0)),
            full((CMP_HIDDEN, HEAD_PAD)),
            full((HEAD_PAD, CMP_HIDDEN)),
        ],
        out_specs=[
            pl.BlockSpec((1, 1, ncp, HEAD_PAD), lambda bi, g: (bi, g, 0, 0)),
            pl.BlockSpec((1, 1, HEAD_PAD, ncp), lambda bi, g: (bi, g, 0, 0)),
        ],
        out_shape=[
            jax.ShapeDtypeStruct((b, NSA_GROUPS, ncp, HEAD_PAD), BF16),
            jax.ShapeDtypeStruct((b, NSA_GROUPS, HEAD_PAD, ncp), BF16),
        ],
        scratch_shapes=[pltpu.VMEM((s, HEAD_PAD), F32)],
        compiler_params=pltpu.CompilerParams(
            dimension_semantics=("parallel", "arbitrary"), vmem_limit_bytes=VMEM_LIMIT),
        name="compress",
    )(pk, pk, pos, w1, w2k, w2vt)


def _flash_update(s_t, shift, pv_fn, m_ref, acc_ref):
    m_prev = m_ref[...]
    m_new = jnp.maximum(m_prev, jnp.max(s_t, axis=0, keepdims=True) + shift)
    alpha = jnp.exp2(m_prev - m_new)
    p = jnp.exp2(s_t - (m_new - shift)).astype(BF16)
    acc_ref[...] = alpha * acc_ref[...] + pv_fn(p)
    m_ref[...] = m_new


def _state_lanes(i):
    return pl.ds(pl.multiple_of(i * WIDE, WIDE), WIDE)


def _tile_phase(count, tile_fn, scores_fn, step_fn, sa_ref, sb_ref):
    if count == 0:
        return
    tile = lambda n: tile_fn(jnp.minimum(jnp.asarray(n, jnp.int32), count - 1))
    sa_ref[...] = scores_fn(*tile(0))

    def pair(p, carry):
        n = 2 * p
        sb_ref[...] = scores_fn(*tile(n + 1))
        step_fn(sa_ref, *tile(n))
        sa_ref[...] = scores_fn(*tile(n + 2))
        step_fn(sb_ref, *tile(n + 1))
        return carry

    lax.fori_loop(0, count // 2, pair, 0, unroll=4)
    if count % 2:
        step_fn(sa_ref, *tile(count - 1))


def _flat_stream(nq, n_far, far_i_ref, far_j_ref, scores_fn, pv_fn, bias_ref, shift_far,
                 sa_ref, sb_ref, m_ref, acc_ref):
    zero = jnp.zeros((1, WIDE), F32)

    def own_step(s_ref, i, j):
        s_t = s_ref[...] + bias_ref[0, 2 * TQ:3 * TQ, :]
        m = jnp.max(s_t, axis=0, keepdims=True)
        m_ref[:, _state_lanes(i)] = m
        acc_ref[:, _state_lanes(i)] = pv_fn(j)(jnp.exp2(s_t - m).astype(BF16))

    def prev_step(s_ref, i, j):
        _flash_update(s_ref[...] + bias_ref[0, TQ:2 * TQ, :], zero, pv_fn(j),
                      m_ref.at[:, _state_lanes(i)], acc_ref.at[:, _state_lanes(i)])

    def far_step(s_ref, i, j):
        _flash_update(s_ref[...], shift_far, pv_fn(j), m_ref.at[:, _state_lanes(i)], acc_ref.at[:, _state_lanes(i)])

    _tile_phase(nq, lambda n: (n, n), scores_fn, own_step, sa_ref, sb_ref)
    _tile_phase(nq - 1, lambda n: (n + 1, n), scores_fn, prev_step, sa_ref, sb_ref)
    _tile_phase(n_far, lambda n: (far_i_ref[n], far_j_ref[n]), scores_fn, far_step, sa_ref, sb_ref)


def _lane_block(x, r):
    return x[:, r * TQ:(r + 1) * TQ]


def _shift_row(c31_ref, head0):
    return jnp.concatenate([jnp.full((1, TQ), c31_ref[head0 + r], F32) for r in range(HG)], axis=1)


def _rank_select(score, n_rows, top, row_weight=None):
    if row_weight is not None:
        blk = lax.broadcasted_iota(jnp.int32, score.shape, 0)
        cnt = jnp.zeros(score.shape, F32)
        for j in range(n_rows):
            row = score[j:j + 1, :]
            ahead = jnp.where(row > score, 1.0, jnp.where((row == score) & (blk > j), 1.0, 0.0))
            cnt = cnt + ahead * row_weight(j)
        return cnt < top
    sub = 8
    n_groups = score.shape[0] // sub
    groups = [score[g * sub:(g + 1) * sub] for g in range(n_groups)]
    counts = [jnp.zeros(groups[0].shape, F32) for _ in range(n_groups)]
    local = lax.broadcasted_iota(jnp.int32, groups[0].shape, 0)
    for j in range(n_rows):
        row = score[j:j + 1, :]
        weight = None if row_weight is None else row_weight(j)
        for g in range(n_groups):
            one = 1.0 if weight is None else weight[g * sub:(g + 1) * sub]
            if j < g * sub:
                ahead = jnp.where(row >= groups[g], one, 0.0)
            elif j >= (g + 1) * sub:
                ahead = jnp.where(row > groups[g], one, 0.0)
            else:
                ahead = jnp.where(local > j - g * sub, jnp.where(row >= groups[g], one, 0.0),
                                  jnp.where(row > groups[g], one, 0.0))
            counts[g] = counts[g] + ahead
    return jnp.concatenate(counts, axis=0) < top


def _key_operand(slot, half, pat):
    if not (isinstance(half, int) and half == 0):
        slot = pltpu.roll(slot.astype(F32), HEAD_DIM * half, 1).astype(BF16)
    lane = lax.broadcasted_iota(jnp.int32, slot.shape, 1)
    return jnp.where(lane < HEAD_DIM, slot, pat)


def _value_operand(v_t):
    rows = lax.broadcasted_iota(jnp.int32, (ACC_ROWS - HEAD_DIM, v_t.shape[1]), 0)
    return jnp.concatenate([v_t, jnp.where(rows == 0, 1.0, 0.0).astype(BF16)], axis=0)


def _moba_body(c31_ref, far_i_ref, far_j_ref, q_ref, k_ref, vt_ref, pat_ref, bias_ref, o_ref,
               kmean_ref, kx_ref, vx_ref, qx_ref, sa_ref, sb_ref, m_ref, acc_ref, *, nb, n_far):
    hg = pl.program_id(1)
    s = q_ref.shape[2]
    nbp = -(-nb // 16) * 16

    kmean_ref[...] = jnp.zeros_like(kmean_ref)
    blk = lax.broadcasted_iota(jnp.int32, (nbp, s), 0)
    cur = lax.broadcasted_iota(jnp.int32, (nbp, s), 1) // MOBA_BLOCK
    zero_rows = jnp.zeros((HEAD_PAD - HEAD_DIM, s), BF16)
    for r in range(HG):
        kx = _key_operand(k_ref[0, :, (r // 2) * HEAD_PAD:(r // 2 + 1) * HEAD_PAD], r % 2, pat_ref[...])
        kx_ref[:, r * HEAD_PAD:(r + 1) * HEAD_PAD] = kx
        vx_ref[r] = _value_operand(vt_ref[0, r * HEAD_DIM:(r + 1) * HEAD_DIM, :])
        for n in range(nb):
            kb = kx[n * MOBA_BLOCK:(n + 1) * MOBA_BLOCK, :].astype(F32)
            kmean_ref[r, n:n + 1, :] = jnp.mean(kb, axis=0, keepdims=True)
        q_t = q_ref[0, r * HEAD_DIM:(r + 1) * HEAD_DIM, :]
        q_pad = jnp.concatenate([q_t, zero_rows], axis=0)
        kh, kl = _split_bf16(kmean_ref[r])
        gate = (_dot(kh, q_pad) + _dot(kl, q_pad))[:nbp]
        top = _rank_select(gate, nb, MOBA_TOPK, lambda n: jnp.where(n < cur, 1.0, 0.0))
        sel = ((blk < cur) & top) | (blk == cur)
        mask_rows = jnp.where(sel, 0.0, NEG).astype(BF16)
        pad_rows = jnp.zeros((HEAD_PAD - EXT0 - nbp, s), BF16)
        qx_ref[r] = jnp.concatenate([q_t, mask_rows, pad_rows], axis=0)

    def scores(i, j):
        qoff = pl.multiple_of(i * TQ, TQ)
        koff = pl.multiple_of(j * TQ, TQ)
        return jnp.concatenate(
            [_dot(kx_ref[pl.ds(koff, TQ), r * HEAD_PAD:(r + 1) * HEAD_PAD], qx_ref[r, :, pl.ds(qoff, TQ)])
             for r in range(HG)], axis=1)

    def pv_fn(j):
        off = pl.multiple_of(j * TQ, TQ)

        def fn(p):
            return jnp.concatenate(
                [_dot(vx_ref[r, :, pl.ds(off, TQ)], _lane_block(p, r)) for r in range(HG)], axis=1)
        return fn

    _flat_stream(nb, n_far, far_i_ref, far_j_ref, scores, pv_fn, bias_ref, _shift_row(c31_ref, hg * HG),
                 sa_ref, sb_ref, m_ref, acc_ref)

    def finish(i, carry):
        acc = acc_ref[:, _state_lanes(i)]
        out = (acc[:HEAD_DIM] / acc[ONES_ROW:ONES_ROW + 1, :]).astype(BF16)
        for r in range(HG):
            o_ref[0, r * HEAD_DIM:(r + 1) * HEAD_DIM, pl.ds(pl.multiple_of(i * TQ, TQ), TQ)] = _lane_block(out, r)
        return carry

    lax.fori_loop(0, nb, finish, 0)


def _moba(pk, pt, pat, bias_win, c31, far_i, far_j):
    b, s, _ = pk.shape
    nb = s // MOBA_BLOCK
    hrows = HG * HEAD_DIM
    smem = pl.BlockSpec(memory_space=pltpu.SMEM)
    return pl.pallas_call(
        functools.partial(_moba_body, nb=nb, n_far=far_i.shape[0]),
        grid=(b, MOBA_HEADS // HG),
        in_specs=[
            smem, smem, smem,
            pl.BlockSpec((1, hrows, s), lambda bi, hg: (bi, TROW_QA // HG + hg, 0)),
            pl.BlockSpec((1, s, hrows), lambda bi, hg: (bi, 0, KSLOT_KA * HEAD_PAD // hrows + hg)),
            pl.BlockSpec((1, hrows, s), lambda bi, hg: (bi, TROW_VA // HG + hg, 0)),
            pl.BlockSpec((s, HEAD_PAD), lambda bi, hg: (0, 0), pipeline_mode=pl.Buffered(1)),
            pl.BlockSpec((1, 3 * TQ, WIDE), lambda bi, hg: (hg, 0, 0)),
        ],
        out_specs=pl.BlockSpec((1, HG * HEAD_DIM, s), lambda bi, hg: (bi, hg, 0)),
        out_shape=jax.ShapeDtypeStruct((b, MOBA_HEADS * HEAD_DIM, s), BF16),
        scratch_shapes=[
            pltpu.VMEM((HG, HEAD_PAD, HEAD_PAD), F32),
            pltpu.VMEM((s, HG * HEAD_PAD), BF16),
            pltpu.VMEM((HG, ACC_ROWS, s), BF16),
            pltpu.VMEM((HG, HEAD_PAD, s), BF16),
            pltpu.VMEM((TQ, WIDE), F32),
            pltpu.VMEM((TQ, WIDE), F32),
            pltpu.VMEM((1, nb * WIDE), F32),
            pltpu.VMEM((ACC_ROWS, nb * WIDE), F32),
        ],
        compiler_params=pltpu.CompilerParams(
            dimension_semantics=("parallel", "arbitrary"), vmem_limit_bytes=VMEM_LIMIT),
        name="moba",
    )(c31, far_i, far_j, pt, pk, pt, pat, bias_win)


def _nsa_body(c31_ref, far_i_ref, far_j_ref, q_ref, ks_ref, kw_ref, vst_ref, vwt_ref, kc_ref, vct_ref,
              gates_ref, pat_ref, bias_ref, bcmp_ref, ovt_ref, o_ref,
              ksx_ref, kwx_ref, vsx_ref, vwx_ref, qx_ref, sa_ref, sb_ref, m_ref, accs_ref, part_ref,
              *, ncp, nslc, nq, n_far):
    g = pl.program_id(1)

    ksx_ref[...] = _key_operand(ks_ref[0], g, pat_ref[...])
    kwx_ref[...] = _key_operand(kw_ref[0], g, jnp.zeros((), BF16))
    grows = pl.ds(pl.multiple_of(g * HEAD_DIM, HEAD_DIM), HEAD_DIM)
    vsx_ref[...] = _value_operand(vst_ref[0, grows, :])
    vwx_ref[...] = _value_operand(vwt_ref[0, grows, :])
    zero_rows = jnp.zeros((HEAD_PAD - HEAD_DIM, WIDE), BF16)

    def front(i, n_win):
        qoff = pl.multiple_of(i * TQ, TQ)
        q4c = jnp.concatenate([q_ref[0, r * HEAD_DIM:(r + 1) * HEAD_DIM, pl.ds(qoff, TQ)] for r in range(HG)],
                              axis=1)
        q4 = jnp.concatenate([q4c, zero_rows], axis=0)

        band_off = pl.multiple_of((nq - 1 - i) * (TQ // CMP_STRIDE), TQ // CMP_STRIDE)
        s_t = _dot(kc_ref[0, 0], q4) + bcmp_ref[0, pl.ds(band_off, ncp), :]
        m = jnp.max(s_t, axis=0, keepdims=True)
        p = jnp.where(s_t > 0.5 * NEG, jnp.exp2(s_t - m), 0.0)
        den = jnp.sum(p, axis=0, keepdims=True)
        pn = p / jnp.where(den > 0.0, den, 1.0)
        o_cmp = _dot(vct_ref[0, 0, :ACC_ROWS, :], pn.astype(BF16))
        psum = _lane_block(pn, 0)
        for r in range(1, HG):
            psum = psum + _lane_block(pn, r)

        ph, plo = _split_bf16(psum)
        imp = (_dot(ovt_ref[...], ph) + _dot(ovt_ref[...], plo))[:EXT0]
        blk = lax.broadcasted_iota(jnp.int32, imp.shape, 0)
        cur = (i * TQ + lax.broadcasted_iota(jnp.int32, imp.shape, 1)) // SLC_BLOCK
        forced = (blk == 0) | (blk == cur) | (blk == cur - 1)
        score = jnp.where(forced, 1e30, jnp.where(blk <= cur, imp, -1.0))
        score = jnp.where(blk < nslc, score, -2.0)
        sel = _rank_select(score, nslc, SLC_TOPN) & (blk < nslc)
        mask_rows = jnp.where(sel, 0.0, NEG).astype(BF16)
        qx_ref[:, _state_lanes(i)] = jnp.concatenate(
            [q4c, jnp.concatenate([mask_rows] * HG, axis=1)], axis=0)

        start = pl.multiple_of((i + 1 - n_win) * TQ, TQ)
        s_w = _dot(kwx_ref[pl.ds(start, n_win * TQ), :], q4) + bias_ref[0, (3 - n_win) * TQ:, :]
        p_w = jnp.exp2(s_w - jnp.max(s_w, axis=0, keepdims=True)).astype(BF16)
        acc_w = _dot(vwx_ref[:, pl.ds(start, n_win * TQ)], p_w)
        o_win = acc_w / acc_w[ONES_ROW:ONES_ROW + 1, :]

        gt = gates_ref[0, :, pl.ds(qoff, TQ)]
        part_ref[:, _state_lanes(i)] = jnp.concatenate(
            [gt[N_GATES * r:N_GATES * r + 1, :] * _lane_block(o_cmp, r)
             + gt[N_GATES * r + 2:N_GATES * r + 3, :] * _lane_block(o_win, r) for r in range(HG)], axis=1)

    for i0 in range(min(2, nq)):
        front(jnp.asarray(i0, jnp.int32), i0 + 1)

    def front_loop(i, carry):
        front(i, 3)
        return carry

    lax.fori_loop(2, nq, front_loop, 0)

    def scores(i, j):
        return _dot(ksx_ref[pl.ds(pl.multiple_of(j * TQ, TQ), TQ), :], qx_ref[:, _state_lanes(i)])

    def pv_fn(j):
        off = pl.multiple_of(j * TQ, TQ)
        return lambda p: _dot(vsx_ref[:, pl.ds(off, TQ)], p)

    _flat_stream(nq, n_far, far_i_ref, far_j_ref, scores, pv_fn, bias_ref,
                 _shift_row(c31_ref, MOBA_HEADS + g * HG), sa_ref, sb_ref, m_ref, accs_ref)

    def finish(i, carry):
        qoff = pl.multiple_of(i * TQ, TQ)
        acc = accs_ref[:, _state_lanes(i)]
        o_slc = acc / acc[ONES_ROW:ONES_ROW + 1, :]
        part = part_ref[:, _state_lanes(i)]
        gt = gates_ref[0, :, pl.ds(qoff, TQ)]
        for r in range(HG):
            out = _lane_block(part, r) + gt[N_GATES * r + 1:N_GATES * r + 2, :] * _lane_block(o_slc, r)
            o_ref[0, r * HEAD_DIM:(r + 1) * HEAD_DIM, pl.ds(qoff, TQ)] = out[:HEAD_DIM].astype(BF16)
        return carry

    lax.fori_loop(0, nq, finish, 0)


def _nsa(pk, pt, kc, vct, gates_t, pat, bias_win, bias_cmp, ovt, c31, far_i, far_j):
    b, s, _ = pk.shape
    nq = s // TQ
    ncp = s // CMP_STRIDE
    nslc = s // SLC_BLOCK
    hrows = HG * HEAD_DIM
    band_rows = bias_cmp.shape[1]
    hgrp = MOBA_HEADS // HG
    smem = pl.BlockSpec(memory_space=pltpu.SMEM)
    k_spec = lambda slot: pl.BlockSpec((1, s, HEAD_PAD), lambda bi, g: (bi, 0, slot))
    vt_spec = lambda row: pl.BlockSpec((1, HEAD_PAD, s), lambda bi, g: (bi, row * HEAD_DIM // HEAD_PAD, 0))
    bias_spec = lambda rows: pl.BlockSpec((1, rows, WIDE), lambda bi, g: (hgrp + g, 0, 0),
                                          pipeline_mode=pl.Buffered(1))
    return pl.pallas_call(
        functools.partial(_nsa_body, ncp=ncp, nslc=nslc, nq=nq, n_far=far_i.shape[0]),
        grid=(b, NSA_GROUPS),
        in_specs=[
            smem, smem, smem,
            pl.BlockSpec((1, hrows, s), lambda bi, g: (bi, TROW_QB // HG + g, 0)),
            k_spec(KSLOT_KS), k_spec(KSLOT_KW), vt_spec(TROW_VS), vt_spec(TROW_VW),
            pl.BlockSpec((1, 1, ncp, HEAD_PAD), lambda bi, g: (bi, g, 0, 0)),
            pl.BlockSpec((1, 1, HEAD_PAD, ncp), lambda bi, g: (bi, g, 0, 0)),
            pl.BlockSpec((1, GATE_ROWS, s), lambda bi, g: (bi, g, 0)),
            pl.BlockSpec((s, HEAD_PAD), lambda bi, g: (0, 0), pipeline_mode=pl.Buffered(1)),
            bias_spec(3 * TQ), bias_spec(band_rows),
            pl.BlockSpec((HEAD_PAD, ncp), lambda bi, g: (0, 0)),
        ],
        out_specs=pl.BlockSpec((1, HG * HEAD_DIM, s), lambda bi, g: (bi, g, 0)),
        out_shape=jax.ShapeDtypeStruct((b, NSA_HEADS * HEAD_DIM, s), BF16),
        scratch_shapes=[
            pltpu.VMEM((s, HEAD_PAD), BF16),
            pltpu.VMEM((s, HEAD_PAD), BF16),
            pltpu.VMEM((ACC_ROWS, s), BF16),
            pltpu.VMEM((ACC_ROWS, s), BF16),
            pltpu.VMEM((HEAD_PAD, nq * WIDE), BF16),
            pltpu.VMEM((TQ, WIDE), F32),
            pltpu.VMEM((TQ, WIDE), F32),
            pltpu.VMEM((1, nq * WIDE), F32),
            pltpu.VMEM((ACC_ROWS, nq * WIDE), F32),
            pltpu.VMEM((ACC_ROWS, nq * WIDE), F32),
        ],
        compiler_params=pltpu.CompilerParams(
            dimension_semantics=("parallel", "arbitrary"), vmem_limit_bytes=VMEM_LIMIT),
        name="nsa",
    )(c31, far_i, far_j, pt, pk, pk, pt, pt, kc, vct, gates_t, pat, bias_win, bias_cmp, ovt)


def _rel_bucket(dist):
    n = jnp.maximum(dist, 0)
    nf = jnp.maximum(n, 1).astype(F32)
    large = REL_MAX_EXACT + (jnp.log(nf / REL_MAX_EXACT) / math.log(REL_MAX_DIST / REL_MAX_EXACT)
                             * (REL_BUCKETS - REL_MAX_EXACT)).astype(jnp.int32)
    return jnp.where(n < REL_MAX_EXACT, n, jnp.minimum(large, REL_BUCKETS - 1))


def _bias_body(tab_ref, *refs):
    h = pl.program_id(0)
    n_tab = len(refs) // 2
    for bk_ref, out_ref in zip(refs[:n_tab], refs[n_tab:]):
        bk = bk_ref[...]
        acc = jnp.full(bk.shape, NEG, F32)
        for bucket in range(REL_BUCKETS):
            acc = jnp.where(bk == bucket, tab_ref[h, bucket], acc)
        out_ref[0] = acc


def _bias_tables(rel_bias, ncp, nq):
    tab = rel_bias.T.astype(F32) * LOG2E
    n_heads = tab.shape[0]
    c = jnp.arange(TQ)[:, None]
    r = jnp.arange(TQ)[None, :]
    d_own = r - c
    bk_own = jnp.where(d_own >= 0, _rel_bucket(d_own), -1)
    bk_prev = _rel_bucket(d_own + TQ)
    bk_win2 = jnp.where(d_own + 2 * TQ < WINDOW, _rel_bucket(d_own + 2 * TQ), -1)
    band_rows = ncp + (nq - 1) * (TQ // CMP_STRIDE)
    u = jnp.arange(band_rows)[:, None] - (nq - 1) * (TQ // CMP_STRIDE)
    d_cmp = r - CMP_STRIDE * u - (CMP_LEN - 1)
    bk_cmp = jnp.where(d_cmp >= 0, _rel_bucket(d_cmp), -1)
    tables = [t.astype(jnp.int32) for t in (jnp.concatenate([bk_win2, bk_prev, bk_own], axis=0), bk_cmp)]
    outs = pl.pallas_call(
        _bias_body,
        grid=(n_heads,),
        in_specs=[pl.BlockSpec(memory_space=pltpu.SMEM)]
        + [pl.BlockSpec(t.shape, lambda h: (0, 0)) for t in tables],
        out_specs=[pl.BlockSpec((1, t.shape[0], TQ), lambda h: (h // HG, 0, h % HG)) for t in tables],
        out_shape=[jax.ShapeDtypeStruct((n_heads // HG, t.shape[0], WIDE), F32) for t in tables],
        compiler_params=pltpu.CompilerParams(dimension_semantics=("parallel",), vmem_limit_bytes=VMEM_LIMIT),
        name="bias_tiles",
    )(tab, *tables)
    return outs, tab[:, REL_BUCKETS - 1]


def _slc_overlap_t(ncp, nslc):
    ratio = SLC_BLOCK // CMP_STRIDE
    mc = CMP_LEN // CMP_STRIDE
    w = np.zeros((HEAD_PAD, nslc * ratio + mc), np.float32)
    for j in range(nslc):
        for a in range(ratio):
            for m in range(mc):
                w[j, ratio * j + a + m] += 1.0
    n_cmp = ncp - 1
    out = np.zeros((HEAD_PAD, ncp), np.float32)
    out[:, :n_cmp] = w[:, :n_cmp]
    return out


def _block_onehot(s, block):
    pat = np.zeros((s, HEAD_PAD), np.float32)
    pat[np.arange(s), EXT0 + np.arange(s) // block] = 1.0
    return pat


def kernel(x, norm_ffn1, w_ffn1_gate, w_ffn1_up, w_ffn1_down, norm_mix, w_in, cmp_pos_k, cmp_w1_k, cmp_w2_k,
           cmp_pos_v, cmp_w1_v, cmp_w2_v, w_out, norm_ffn2, w_ffn2_gate, w_ffn2_up, w_ffn2_down, rel_bias,
           norm_final):
    b, s, d = x.shape
    assert d == D_MODEL and s % (2 * TQ) == 0 and norm_ffn1.shape[0] == 1
    assert s // SLC_BLOCK <= HEAD_PAD - EXT0, "selection-mask rows must fit beside the 64 head dims"
    n = b * s
    nq = s // TQ
    ncp = s // CMP_STRIDE
    nslc = s // SLC_BLOCK
    scale = HEAD_DIM ** -0.5 * LOG2E

    sizes = (512, 512, 512, 512) + (128,) * 6 + (24,)
    splits = np.cumsum(sizes)[:-1].tolist()
    wqa, wka, wva, wqb, wkc, wvc, wks, wvs, wkw, wvw, wgb = jnp.split(w_in[0], splits, axis=1)
    w_k = jnp.concatenate([wka, wkc, wvc, wks, wkw], axis=1).astype(BF16)
    w_t = jnp.concatenate([wqa * scale, wva, wqb * scale, wvs, wvw], axis=1).T.astype(BF16)
    assert w_k.shape[1] == N_KSLOTS * HEAD_PAD and w_t.shape[0] == N_TROWS * HEAD_DIM
    wgt = wgb.reshape(D_MODEL, NSA_GROUPS, NSA_REP * N_GATES)
    wgt = jnp.pad(wgt, ((0, 0), (0, 0), (0, GATE_ROWS - NSA_REP * N_GATES)))
    wgt = wgt.reshape(D_MODEL, NSA_GROUPS * GATE_ROWS).T.astype(BF16)

    def cmp_params(pos, w1, w2):
        w1_h = w1[0].reshape(CMP_LEN, HEAD_DIM, CMP_HIDDEN)
        pos_g, w1_g = [], []
        for grp in range(NSA_GROUPS):
            pad = ((0, 0), (grp * HEAD_DIM, HEAD_PAD - (grp + 1) * HEAD_DIM))
            pos_g.append(jnp.pad(pos[0], pad))
            w1_g.append(jnp.pad(w1_h, pad + ((0, 0),)))
        w2_p = jnp.pad(w2[0], ((0, 0), (0, HEAD_PAD - HEAD_DIM)))
        return jnp.stack(pos_g), jnp.stack(w1_g).astype(BF16), w2_p.astype(BF16)

    pos_k, w1k, w2k = cmp_params(cmp_pos_k, cmp_w1_k, cmp_w2_k)
    pos_v, w1v, w2v = cmp_params(cmp_pos_v, cmp_w1_v, cmp_w2_v)
    cmp_pos = jnp.stack([pos_k, pos_v])
    cmp_w1 = jnp.stack([w1k, w1v])

    wo = w_out[0]
    wo_a = wo[:MOBA_HEADS * HEAD_DIM].astype(BF16)
    wo_b = wo[MOBA_HEADS * HEAD_DIM:].astype(BF16)

    (bias_win, bias_cmp), c31 = _bias_tables(rel_bias, ncp, nq)
    ovt = jnp.asarray(_slc_overlap_t(ncp, nslc)).astype(BF16)
    pat_moba = jnp.asarray(_block_onehot(s, MOBA_BLOCK)).astype(BF16)
    pat_slc = jnp.asarray(_block_onehot(s, SLC_BLOCK)).astype(BF16)
    far = [(i, j) for j in range(nq) for i in range(j + 2, nq)]
    far_i = jnp.asarray(np.array([i for i, _ in far], np.int32))
    far_j = jnp.asarray(np.array([j for _, j in far], np.int32))

    x2d = x.reshape(n, D_MODEL)
    nf = norm_final.reshape(1, D_MODEL)
    x1 = _ffn(x2d, norm_ffn1, w_ffn1_gate[0].astype(BF16), w_ffn1_up[0].astype(BF16),
              w_ffn1_down[0].astype(BF16), nf, final_norm=False)
    pk, pt, gates_t = _inproj(x1, norm_mix, w_k, w_t, wgt, b, s)
    pk = pk.reshape(b, s, N_KSLOTS * HEAD_PAD)
    kc, vct = _compress(pk, cmp_pos, cmp_w1, w2k, w2v.T)
    o_a = _moba(pk, pt, pat_moba, bias_win, c31, far_i, far_j)
    o_b = _nsa(pk, pt, kc, vct, gates_t, pat_slc, bias_win, bias_cmp, ovt, c31, far_i, far_j)
    y = _ffn(x1, norm_ffn2, w_ffn2_gate[0].astype(BF16), w_ffn2_up[0].astype(BF16),
             w_ffn2_down[0].astype(BF16), nf, final_norm=True, mixer=(o_a, o_b, wo_a, wo_b))
    return y.reshape(b, s, D_MODEL)
```

```python
import functools
import math

import numpy as np
import jax
import jax.numpy as jnp
from jax import lax
from jax.experimental import pallas as pl
from jax.experimental.pallas import tpu as pltpu

D_MODEL = 1024
D_FF = 2816
HEAD_DIM = 64
HEAD_PAD = 128
ONES_ROW = HEAD_DIM
EXT0 = HEAD_DIM
MOBA_HEADS = 8
NSA_HEADS = 8
NSA_GROUPS = 2
NSA_REP = NSA_HEADS // NSA_GROUPS
HG = 4
MOBA_BLOCK = 256
MOBA_TOPK = 3
CMP_LEN = 32
CMP_STRIDE = 16
CMP_HIDDEN = 256
SLC_BLOCK = 64
SLC_TOPN = 16
WINDOW = 512
N_GATES = 3
GATE_ROWS = 16
REL_BUCKETS = 32
REL_MAX_EXACT = REL_BUCKETS // 2
REL_MAX_DIST = 128
RMS_EPS = 1e-6
TQ = 256
WIDE = HG * TQ
NEG = -1e30
LOG2E = 1.4426950408889634
ACC_ROWS = 80
VMEM_LIMIT = 56 * 1024 * 1024

KSLOT_KA, KSLOT_KC, KSLOT_VC, KSLOT_KS, KSLOT_KW = 0, 4, 5, 6, 7
N_KSLOTS = 8
TROW_QA, TROW_VA, TROW_QB, TROW_VS, TROW_VW = 0, 8, 16, 24, 26
N_TROWS = 28

F32 = jnp.float32
BF16 = jnp.bfloat16


def _dot(a, b):
    return jnp.dot(a, b, preferred_element_type=F32)


def _dot_nt(a, b):
    return lax.dot_general(a, b, (((1,), (1,)), ((), ())), preferred_element_type=F32)


def _split_bf16(x):
    hi = x.astype(BF16)
    lo = (x - hi.astype(F32)).astype(BF16)
    return hi, lo


def _rmsnorm(x, g):
    ms = jnp.mean(x * x, axis=-1, keepdims=True)
    return x * lax.rsqrt(ms + RMS_EPS) * g


def _dot_tn(a_t, b):
    return lax.dot_general(a_t, b, (((0,), (0,)), ((), ())), preferred_element_type=F32)


def _ffn_body(x_ref, g_ref, wg_ref, wu_ref, wd_ref, gf_ref, *rest, final_norm, n_chunks, with_mixer):
    x = x_ref[...]
    if with_mixer:
        oa_ref, ob_ref, woa_ref, wob_ref, o_ref = rest
        x = x + _dot_tn(oa_ref[0], woa_ref[...]) + _dot_tn(ob_ref[0], wob_ref[...])
    else:
        (o_ref,) = rest
    h = _rmsnorm(x, g_ref[...]).astype(BF16)
    tf = D_FF // n_chunks
    acc = None
    for c in range(n_chunks):
        a = _dot(h, wg_ref[:, c * tf:(c + 1) * tf])
        u = _dot(h, wu_ref[:, c * tf:(c + 1) * tf])
        z = ((a * jax.nn.sigmoid(a)) * u).astype(BF16)
        part = _dot(z, wd_ref[c * tf:(c + 1) * tf, :])
        acc = part if acc is None else acc + part
    y = x + 0.5 * acc
    if final_norm:
        y = _rmsnorm(y, gf_ref[...])
    o_ref[...] = y


def _ffn(x2d, g, wg, wu, wd, gf, *, final_norm, mixer=None, tm=512, n_chunks=1):
    n = x2d.shape[0]
    resident = lambda shape: pl.BlockSpec(shape, lambda i: (0, 0), pipeline_mode=pl.Buffered(1))
    in_specs = [
        pl.BlockSpec((tm, D_MODEL), lambda i: (i, 0)),
        resident((1, D_MODEL)),
        resident((D_MODEL, D_FF)), resident((D_MODEL, D_FF)), resident((D_FF, D_MODEL)),
        resident((1, D_MODEL)),
    ]
    args = [x2d, g, wg, wu, wd, gf]
    if mixer is not None:
        o_a, o_b, wo_a, wo_b = mixer
        spt = o_a.shape[2] // tm
        for o_t in (o_a, o_b):
            in_specs.append(pl.BlockSpec((1, o_t.shape[1], tm), lambda i: (i // spt, 0, i % spt)))
        in_specs += [resident(wo_a.shape), resident(wo_b.shape)]
        args += [o_a, o_b, wo_a, wo_b]
    return pl.pallas_call(
        functools.partial(_ffn_body, final_norm=final_norm, n_chunks=n_chunks, with_mixer=mixer is not None),
        grid=(n // tm,),
        in_specs=in_specs,
        out_specs=pl.BlockSpec((tm, D_MODEL), lambda i: (i, 0)),
        out_shape=jax.ShapeDtypeStruct((n, D_MODEL), F32),
        compiler_params=pltpu.CompilerParams(
            dimension_semantics=("parallel",), vmem_limit_bytes=VMEM_LIMIT),
        name="ffn_final" if final_norm else "ffn",
    )(*args)


def _inproj_body(x_ref, g_ref, wk_ref, wt_ref, wgt_ref, pk_ref, pt_ref, gates_ref):
    h = _rmsnorm(x_ref[...], g_ref[...]).astype(BF16)
    pk_ref[...] = _dot(h, wk_ref[...]).astype(BF16)
    pt_ref[0] = _dot_nt(wt_ref[...], h).astype(BF16)
    gates_ref[0] = jax.nn.sigmoid(_dot_nt(wgt_ref[...], h))


def _inproj(x2d, g, wk, wt, wgt, b, s, *, tm=512):
    n = x2d.shape[0]
    nk = wk.shape[1]
    nt = wt.shape[0]
    ng = wgt.shape[0]
    spt = s // tm
    const = lambda shape: pl.BlockSpec(shape, lambda i: (0, 0))
    return pl.pallas_call(
        _inproj_body,
        grid=(n // tm,),
        in_specs=[
            pl.BlockSpec((tm, D_MODEL), lambda i: (i, 0)),
            const((1, D_MODEL)), const((D_MODEL, nk)), const((nt, D_MODEL)), const((ng, D_MODEL)),
        ],
        out_specs=[
            pl.BlockSpec((tm, nk), lambda i: (i, 0)),
            pl.BlockSpec((1, nt, tm), lambda i: (i // spt, 0, i % spt)),
            pl.BlockSpec((1, ng, tm), lambda i: (i // spt, 0, i % spt)),
        ],
        out_shape=[jax.ShapeDtypeStruct((n, nk), BF16), jax.ShapeDtypeStruct((b, nt, s), BF16),
                   jax.ShapeDtypeStruct((b, ng, s), F32)],
        compiler_params=pltpu.CompilerParams(
            dimension_semantics=("parallel",), vmem_limit_bytes=VMEM_LIMIT),
        name="inproj",
    )(x2d, g, wk, wt, wgt)


def _compress_body(kraw_ref, vraw_ref, pos_ref, w1_ref, w2k_ref, w2vt_ref, kc_ref, vct_ref, xs_ref, *, ncp):
    half = CMP_LEN // 2
    for kv, raw_ref in enumerate((kraw_ref, vraw_ref)):
        xs_ref[...] = raw_ref[0].astype(F32)
        y0 = jnp.zeros((ncp, CMP_HIDDEN), F32)
        y1 = jnp.zeros((ncp, CMP_HIDDEN), F32)
        for p in range(half):
            xp = xs_ref[pl.ds(p, ncp, stride=CMP_STRIDE), :]
            y0 = y0 + _dot((xp + pos_ref[kv, 0, p:p + 1, :]).astype(BF16), w1_ref[kv, 0, p])
            y1 = y1 + _dot((xp + pos_ref[kv, 0, half + p:half + p + 1, :]).astype(BF16), w1_ref[kv, 0, half + p])
        h1 = y0 + pltpu.roll(y1, ncp - 1, 0)
        act = (h1 * jax.nn.sigmoid(h1)).astype(BF16)
        if kv == 0:
            kc_ref[0, 0] = _dot(act, w2k_ref[...]).astype(BF16)
        else:
            out_t = _dot_nt(w2vt_ref[...], act)
            row = lax.broadcasted_iota(jnp.int32, out_t.shape, 0)
            vct_ref[0, 0] = (out_t + jnp.where(row == ONES_ROW, 1.0, 0.0)).astype(BF16)


def _compress(pk, pos, w1, w2k, w2vt):
    b, s, _ = pk.shape
    ncp = s // CMP_STRIDE
    full = lambda shape: pl.BlockSpec(shape, lambda bi, g: (0,) * len(shape))
    return pl.pallas_call(
        functools.partial(_compress_body, ncp=ncp),
        grid=(b, NSA_GROUPS),
        in_specs=[
            pl.BlockSpec((1, s, HEAD_PAD), lambda bi, g: (bi, 0, KSLOT_KC)),
            pl.BlockSpec((1, s, HEAD_PAD), lambda bi, g: (bi, 0, KSLOT_VC)),
            pl.BlockSpec((2, 1, CMP_LEN, HEAD_PAD), lambda bi, g: (0, g, 0, 0)),
            pl.BlockSpec((2, 1, CMP_LEN, HEAD_PAD, CMP_HIDDEN), lambda bi, g: (0, g, 0, 0, 0)),
            full((CMP_HIDDEN, HEAD_PAD)),
            full((HEAD_PAD, CMP_HIDDEN)),
        ],
        out_specs=[
            pl.BlockSpec((1, 1, ncp, HEAD_PAD), lambda bi, g: (bi, g, 0, 0)),
            pl.BlockSpec((1, 1, HEAD_PAD, ncp), lambda bi, g: (bi, g, 0, 0)),
        ],
        out_shape=[
            jax.ShapeDtypeStruct((b, NSA_GROUPS, ncp, HEAD_PAD), BF16),
            jax.ShapeDtypeStruct((b, NSA_GROUPS, HEAD_PAD, ncp), BF16),
        ],
        scratch_shapes=[pltpu.VMEM((s, HEAD_PAD), F32)],
        compiler_params=pltpu.CompilerParams(
            dimension_semantics=("parallel", "arbitrary"), vmem_limit_bytes=VMEM_LIMIT),
        name="compress",
    )(pk, pk, pos, w1, w2k, w2vt)


def _flash_update(s_t, shift, pv_fn, m_ref, acc_ref):
    m_prev = m_ref[...]
    m_new = jnp.maximum(m_prev, jnp.max(s_t, axis=0, keepdims=True) + shift)
    alpha = jnp.exp2(m_prev - m_new)
    p = jnp.exp2(s_t - (m_new - shift)).astype(BF16)
    acc_ref[...] = alpha * acc_ref[...] + pv_fn(p)
    m_ref[...] = m_new


def _state_lanes(i):
    return pl.ds(pl.multiple_of(i * WIDE, WIDE), WIDE)


def _tile_phase(count, tile_fn, scores_fn, step_fn, sa_ref, sb_ref):
    if count == 0:
        return
    tile = lambda n: tile_fn(jnp.minimum(jnp.asarray(n, jnp.int32), count - 1))
    sa_ref[...] = scores_fn(*tile(0))

    def pair(p, carry):
        n = 2 * p
        sb_ref[...] = scores_fn(*tile(n + 1))
        step_fn(sa_ref, *tile(n))
        sa_ref[...] = scores_fn(*tile(n + 2))
        step_fn(sb_ref, *tile(n + 1))
        return carry

    lax.fori_loop(0, count // 2, pair, 0, unroll=4)
    if count % 2:
        step_fn(sa_ref, *tile(count - 1))


def _flat_stream(nq, n_far, far_i_ref, far_j_ref, scores_fn, pv_fn, bias_ref, shift_far,
                 sa_ref, sb_ref, m_ref, acc_ref):
    zero = jnp.zeros((1, WIDE), F32)

    def own_step(s_ref, i, j):
        s_t = s_ref[...] + bias_ref[0, 2 * TQ:3 * TQ, :]
        m = jnp.max(s_t, axis=0, keepdims=True)
        m_ref[:, _state_lanes(i)] = m
        acc_ref[:, _state_lanes(i)] = pv_fn(j)(jnp.exp2(s_t - m).astype(BF16))

    def prev_step(s_ref, i, j):
        _flash_update(s_ref[...] + bias_ref[0, TQ:2 * TQ, :], zero, pv_fn(j),
                      m_ref.at[:, _state_lanes(i)], acc_ref.at[:, _state_lanes(i)])

    def far_step(s_ref, i, j):
        _flash_update(s_ref[...], shift_far, pv_fn(j), m_ref.at[:, _state_lanes(i)], acc_ref.at[:, _state_lanes(i)])

    _tile_phase(nq, lambda n: (n, n), scores_fn, own_step, sa_ref, sb_ref)
    _tile_phase(nq - 1, lambda n: (n + 1, n), scores_fn, prev_step, sa_ref, sb_ref)
    _tile_phase(n_far, lambda n: (far_i_ref[n], far_j_ref[n]), scores_fn, far_step, sa_ref, sb_ref)


def _lane_block(x, r):
    return x[:, r * TQ:(r + 1) * TQ]


def _shift_row(c31_ref, head0):
    return jnp.concatenate([jnp.full((1, TQ), c31_ref[head0 + r], F32) for r in range(HG)], axis=1)


def _rank_select(score, n_rows, top, row_weight=None):
    if row_weight is not None:
        blk = lax.broadcasted_iota(jnp.int32, score.shape, 0)
        cnt = jnp.zeros(score.shape, F32)
        for j in range(n_rows):
            row = score[j:j + 1, :]
            ahead = jnp.where(row > score, 1.0, jnp.where((row == score) & (blk > j), 1.0, 0.0))
            cnt = cnt + ahead * row_weight(j)
        return cnt < top
    sub = 8
    n_groups = score.shape[0] // sub
    groups = [score[g * sub:(g + 1) * sub] for g in range(n_groups)]
    counts = [jnp.zeros(groups[0].shape, F32) for _ in range(n_groups)]
    local = lax.broadcasted_iota(jnp.int32, groups[0].shape, 0)
    for j in range(n_rows):
        row = score[j:j + 1, :]
        weight = None if row_weight is None else row_weight(j)
        for g in range(n_groups):
            one = 1.0 if weight is None else weight[g * sub:(g + 1) * sub]
            if j < g * sub:
                ahead = jnp.where(row >= groups[g], one, 0.0)
            elif j >= (g + 1) * sub:
                ahead = jnp.where(row > groups[g], one, 0.0)
            else:
                ahead = jnp.where(local > j - g * sub, jnp.where(row >= groups[g], one, 0.0),
                                  jnp.where(row > groups[g], one, 0.0))
            counts[g] = counts[g] + ahead
    return jnp.concatenate(counts, axis=0) < top


def _key_operand(slot, half, pat):
    if not (isinstance(half, int) and half == 0):
        slot = pltpu.roll(slot.astype(F32), HEAD_DIM * half, 1).astype(BF16)
    lane = lax.broadcasted_iota(jnp.int32, slot.shape, 1)
    return jnp.where(lane < HEAD_DIM, slot, pat)


def _value_operand(v_t):
    rows = lax.broadcasted_iota(jnp.int32, (ACC_ROWS - HEAD_DIM, v_t.shape[1]), 0)
    return jnp.concatenate([v_t, jnp.where(rows == 0, 1.0, 0.0).astype(BF16)], axis=0)


def _moba_body(c31_ref, far_i_ref, far_j_ref, q_ref, k_ref, vt_ref, pat_ref, bias_ref, o_ref,
               kmean_ref, kx_ref, vx_ref, qx_ref, sa_ref, sb_ref, m_ref, acc_ref, *, nb, n_far):
    hg = pl.program_id(1)
    s = q_ref.shape[2]
    nbp = -(-nb // 16) * 16

    kmean_ref[...] = jnp.zeros_like(kmean_ref)
    blk = lax.broadcasted_iota(jnp.int32, (nbp, s), 0)
    cur = lax.broadcasted_iota(jnp.int32, (nbp, s), 1) // MOBA_BLOCK
    zero_rows = jnp.zeros((HEAD_PAD - HEAD_DIM, s), BF16)
    for r in range(HG):
        kx = _key_operand(k_ref[0, :, (r // 2) * HEAD_PAD:(r // 2 + 1) * HEAD_PAD], r % 2, pat_ref[...])
        kx_ref[:, r * HEAD_PAD:(r + 1) * HEAD_PAD] = kx
        vx_ref[r] = _value_operand(vt_ref[0, r * HEAD_DIM:(r + 1) * HEAD_DIM, :])
        for n in range(nb):
            kb = kx[n * MOBA_BLOCK:(n + 1) * MOBA_BLOCK, :].astype(F32)
            kmean_ref[r, n:n + 1, :] = jnp.mean(kb, axis=0, keepdims=True)
        q_t = q_ref[0, r * HEAD_DIM:(r + 1) * HEAD_DIM, :]
        q_pad = jnp.concatenate([q_t, zero_rows], axis=0)
        kh, kl = _split_bf16(kmean_ref[r])
        gate = (_dot(kh, q_pad) + _dot(kl, q_pad))[:nbp]
        top = _rank_select(gate, nb, MOBA_TOPK, lambda n: jnp.where(n < cur, 1.0, 0.0))
        sel = ((blk < cur) & top) | (blk == cur)
        mask_rows = jnp.where(sel, 0.0, NEG).astype(BF16)
        pad_rows = jnp.zeros((HEAD_PAD - EXT0 - nbp, s), BF16)
        qx_ref[r] = jnp.concatenate([q_t, mask_rows, pad_rows], axis=0)

    def scores(i, j):
        qoff = pl.multiple_of(i * TQ, TQ)
        koff = pl.multiple_of(j * TQ, TQ)
        return jnp.concatenate(
            [_dot(kx_ref[pl.ds(koff, TQ), r * HEAD_PAD:(r + 1) * HEAD_PAD], qx_ref[r, :, pl.ds(qoff, TQ)])
             for r in range(HG)], axis=1)

    def pv_fn(j):
        off = pl.multiple_of(j * TQ, TQ)

        def fn(p):
            return jnp.concatenate(
                [_dot(vx_ref[r, :, pl.ds(off, TQ)], _lane_block(p, r)) for r in range(HG)], axis=1)
        return fn

    _flat_stream(nb, n_far, far_i_ref, far_j_ref, scores, pv_fn, bias_ref, _shift_row(c31_ref, hg * HG),
                 sa_ref, sb_ref, m_ref, acc_ref)

    def finish(i, carry):
        acc = acc_ref[:, _state_lanes(i)]
        out = (acc[:HEAD_DIM] / acc[ONES_ROW:ONES_ROW + 1, :]).astype(BF16)
        for r in range(HG):
            o_ref[0, r * HEAD_DIM:(r + 1) * HEAD_DIM, pl.ds(pl.multiple_of(i * TQ, TQ), TQ)] = _lane_block(out, r)
        return carry

    lax.fori_loop(0, nb, finish, 0)


def _moba(pk, pt, pat, bias_win, c31, far_i, far_j):
    b, s, _ = pk.shape
    nb = s // MOBA_BLOCK
    hrows = HG * HEAD_DIM
    smem = pl.BlockSpec(memory_space=pltpu.SMEM)
    return pl.pallas_call(
        functools.partial(_moba_body, nb=nb, n_far=far_i.shape[0]),
        grid=(b, MOBA_HEADS // HG),
        in_specs=[
            smem, smem, smem,
            pl.BlockSpec((1, hrows, s), lambda bi, hg: (bi, TROW_QA // HG + hg, 0)),
            pl.BlockSpec((1, s, hrows), lambda bi, hg: (bi, 0, KSLOT_KA * HEAD_PAD // hrows + hg)),
            pl.BlockSpec((1, hrows, s), lambda bi, hg: (bi, TROW_VA // HG + hg, 0)),
            pl.BlockSpec((s, HEAD_PAD), lambda bi, hg: (0, 0), pipeline_mode=pl.Buffered(1)),
            pl.BlockSpec((1, 3 * TQ, WIDE), lambda bi, hg: (hg, 0, 0)),
        ],
        out_specs=pl.BlockSpec((1, HG * HEAD_DIM, s), lambda bi, hg: (bi, hg, 0)),
        out_shape=jax.ShapeDtypeStruct((b, MOBA_HEADS * HEAD_DIM, s), BF16),
        scratch_shapes=[
            pltpu.VMEM((HG, HEAD_PAD, HEAD_PAD), F32),
            pltpu.VMEM((s, HG * HEAD_PAD), BF16),
            pltpu.VMEM((HG, ACC_ROWS, s), BF16),
            pltpu.VMEM((HG, HEAD_PAD, s), BF16),
            pltpu.VMEM((TQ, WIDE), F32),
            pltpu.VMEM((TQ, WIDE), F32),
            pltpu.VMEM((1, nb * WIDE), F32),
            pltpu.VMEM((ACC_ROWS, nb * WIDE), F32),
        ],
        compiler_params=pltpu.CompilerParams(
            dimension_semantics=("parallel", "arbitrary"), vmem_limit_bytes=VMEM_LIMIT),
        name="moba",
    )(c31, far_i, far_j, pt, pk, pt, pat, bias_win)


def _nsa_body(c31_ref, far_i_ref, far_j_ref, q_ref, ks_ref, kw_ref, vst_ref, vwt_ref, kc_ref, vct_ref,
              gates_ref, pat_ref, bias_ref, bcmp_ref, ovt_ref, o_ref,
              ksx_ref, kwx_ref, vsx_ref, vwx_ref, qx_ref, sa_ref, sb_ref, m_ref, accs_ref, part_ref,
              *, ncp, nslc, nq, n_far):
    g = pl.program_id(1)

    ksx_ref[...] = _key_operand(ks_ref[0], g, pat_ref[...])
    kwx_ref[...] = _key_operand(kw_ref[0], g, jnp.zeros((), BF16))
    grows = pl.ds(pl.multiple_of(g * HEAD_DIM, HEAD_DIM), HEAD_DIM)
    vsx_ref[...] = _value_operand(vst_ref[0, grows, :])
    vwx_ref[...] = _value_operand(vwt_ref[0, grows, :])
    zero_rows = jnp.zeros((HEAD_PAD - HEAD_DIM, WIDE), BF16)

    def front(i, n_win, n_act):
        qoff = pl.multiple_of(i * TQ, TQ)
        q4c = jnp.concatenate([q_ref[0, r * HEAD_DIM:(r + 1) * HEAD_DIM, pl.ds(qoff, TQ)] for r in range(HG)],
                              axis=1)
        q4 = jnp.concatenate([q4c, zero_rows], axis=0)

        band_off = pl.multiple_of((nq - 1 - i) * (TQ // CMP_STRIDE), TQ // CMP_STRIDE)
        s_t = _dot(kc_ref[0, 0], q4) + bcmp_ref[0, pl.ds(band_off, ncp), :]
        m = jnp.max(s_t, axis=0, keepdims=True)
        p = jnp.exp2(s_t - m)
        den = jnp.sum(p, axis=0, keepdims=True)
        pn = p * jnp.where(m > 0.5 * NEG, 1.0 / den, 0.0)
        o_cmp = _dot(vct_ref[0, 0, :ACC_ROWS, :], pn.astype(BF16))
        psum = _lane_block(pn, 0)
        for r in range(1, HG):
            psum = psum + _lane_block(pn, r)

        ph, plo = _split_bf16(psum)
        imp = (_dot(ovt_ref[...], ph) + _dot(ovt_ref[...], plo))[:n_act]
        blk = lax.broadcasted_iota(jnp.int32, imp.shape, 0)
        cur = (i * TQ + lax.broadcasted_iota(jnp.int32, imp.shape, 1)) // SLC_BLOCK
        forced = (blk == 0) | (blk == cur) | (blk == cur - 1)
        score = jnp.where(forced, 1e30, jnp.where(blk <= cur, imp, -1.0))
        sel = _rank_select(score, n_act, SLC_TOPN)
        mask_rows = jnp.where(sel, 0.0, NEG)
        if n_act < EXT0:
            mask_rows = jnp.concatenate([mask_rows, jnp.full((EXT0 - n_act, TQ), NEG, F32)], axis=0)
        mask_rows = mask_rows.astype(BF16)
        qx_ref[:, _state_lanes(i)] = jnp.concatenate(
            [q4c, jnp.concatenate([mask_rows] * HG, axis=1)], axis=0)

        start = pl.multiple_of((i + 1 - n_win) * TQ, TQ)
        s_w = _dot(kwx_ref[pl.ds(start, n_win * TQ), :], q4) + bias_ref[0, (3 - n_win) * TQ:, :]
        p_w = jnp.exp2(s_w - jnp.max(s_w, axis=0, keepdims=True)).astype(BF16)
        acc_w = _dot(vwx_ref[:, pl.ds(start, n_win * TQ)], p_w)
        o_win = acc_w / acc_w[ONES_ROW:ONES_ROW + 1, :]

        gt = gates_ref[0, :, pl.ds(qoff, TQ)]
        part_ref[:, _state_lanes(i)] = jnp.concatenate(
            [gt[N_GATES * r:N_GATES * r + 1, :] * _lane_block(o_cmp, r)
             + gt[N_GATES * r + 2:N_GATES * r + 3, :] * _lane_block(o_win, r) for r in range(HG)], axis=1)

    blocks_upto = lambda i_end: min(-(-nslc // 8) * 8, -(-i_end * TQ // SLC_BLOCK // 8) * 8)
    for i0 in range(min(2, nq)):
        front(jnp.asarray(i0, jnp.int32), i0 + 1, blocks_upto(i0 + 1))

    mid = max(2, nq // 2)
    for lo, hi in ((2, mid), (mid, nq)):
        def front_loop(i, carry, n_act=blocks_upto(hi)):
            front(i, 3, n_act)
            return carry

        if hi > lo:
            lax.fori_loop(lo, hi, front_loop, 0, unroll=2)

    def scores(i, j):
        return _dot(ksx_ref[pl.ds(pl.multiple_of(j * TQ, TQ), TQ), :], qx_ref[:, _state_lanes(i)])

    def pv_fn(j):
        off = pl.multiple_of(j * TQ, TQ)
        return lambda p: _dot(vsx_ref[:, pl.ds(off, TQ)], p)

    _flat_stream(nq, n_far, far_i_ref, far_j_ref, scores, pv_fn, bias_ref,
                 _shift_row(c31_ref, MOBA_HEADS + g * HG), sa_ref, sb_ref, m_ref, accs_ref)

    def finish(i, carry):
        qoff = pl.multiple_of(i * TQ, TQ)
        acc = accs_ref[:, _state_lanes(i)]
        o_slc = acc / acc[ONES_ROW:ONES_ROW + 1, :]
        part = part_ref[:, _state_lanes(i)]
        gt = gates_ref[0, :, pl.ds(qoff, TQ)]
        for r in range(HG):
            out = _lane_block(part, r) + gt[N_GATES * r + 1:N_GATES * r + 2, :] * _lane_block(o_slc, r)
            o_ref[0, r * HEAD_DIM:(r + 1) * HEAD_DIM, pl.ds(qoff, TQ)] = out[:HEAD_DIM].astype(BF16)
        return carry

    lax.fori_loop(0, nq, finish, 0)


def _nsa(pk, pt, kc, vct, gates_t, pat, bias_win, bias_cmp, ovt, c31, far_i, far_j):
    b, s, _ = pk.shape
    nq = s // TQ
    ncp = s // CMP_STRIDE
    nslc = s // SLC_BLOCK
    hrows = HG * HEAD_DIM
    band_rows = bias_cmp.shape[1]
    hgrp = MOBA_HEADS // HG
    smem = pl.BlockSpec(memory_space=pltpu.SMEM)
    k_spec = lambda slot: pl.BlockSpec((1, s, HEAD_PAD), lambda bi, g: (bi, 0, slot))
    vt_spec = lambda row: pl.BlockSpec((1, HEAD_PAD, s), lambda bi, g: (bi, row * HEAD_DIM // HEAD_PAD, 0))
    bias_spec = lambda rows: pl.BlockSpec((1, rows, WIDE), lambda bi, g: (hgrp + g, 0, 0),
                                          pipeline_mode=pl.Buffered(1))
    return pl.pallas_call(
        functools.partial(_nsa_body, ncp=ncp, nslc=nslc, nq=nq, n_far=far_i.shape[0]),
        grid=(b, NSA_GROUPS),
        in_specs=[
            smem, smem, smem,
            pl.BlockSpec((1, hrows, s), lambda bi, g: (bi, TROW_QB // HG + g, 0)),
            k_spec(KSLOT_KS), k_spec(KSLOT_KW), vt_spec(TROW_VS), vt_spec(TROW_VW),
            pl.BlockSpec((1, 1, ncp, HEAD_PAD), lambda bi, g: (bi, g, 0, 0)),
            pl.BlockSpec((1, 1, HEAD_PAD, ncp), lambda bi, g: (bi, g, 0, 0)),
            pl.BlockSpec((1, GATE_ROWS, s), lambda bi, g: (bi, g, 0)),
            pl.BlockSpec((s, HEAD_PAD), lambda bi, g: (0, 0), pipeline_mode=pl.Buffered(1)),
            bias_spec(3 * TQ), bias_spec(band_rows),
            pl.BlockSpec((HEAD_PAD, ncp), lambda bi, g: (0, 0)),
        ],
        out_specs=pl.BlockSpec((1, HG * HEAD_DIM, s), lambda bi, g: (bi, g, 0)),
        out_shape=jax.ShapeDtypeStruct((b, NSA_HEADS * HEAD_DIM, s), BF16),
        scratch_shapes=[
            pltpu.VMEM((s, HEAD_PAD), BF16),
            pltpu.VMEM((s, HEAD_PAD), BF16),
            pltpu.VMEM((ACC_ROWS, s), BF16),
            pltpu.VMEM((ACC_ROWS, s), BF16),
            pltpu.VMEM((HEAD_PAD, nq * WIDE), BF16),
            pltpu.VMEM((TQ, WIDE), F32),
            pltpu.VMEM((TQ, WIDE), F32),
            pltpu.VMEM((1, nq * WIDE), F32),
            pltpu.VMEM((ACC_ROWS, nq * WIDE), F32),
            pltpu.VMEM((ACC_ROWS, nq * WIDE), F32),
        ],
        compiler_params=pltpu.CompilerParams(
            dimension_semantics=("parallel", "arbitrary"), vmem_limit_bytes=VMEM_LIMIT),
        name="nsa",
    )(c31, far_i, far_j, pt, pk, pk, pt, pt, kc, vct, gates_t, pat, bias_win, bias_cmp, ovt)


def _rel_bucket(dist):
    n = jnp.maximum(dist, 0)
    nf = jnp.maximum(n, 1).astype(F32)
    large = REL_MAX_EXACT + (jnp.log(nf / REL_MAX_EXACT) / math.log(REL_MAX_DIST / REL_MAX_EXACT)
                             * (REL_BUCKETS - REL_MAX_EXACT)).astype(jnp.int32)
    return jnp.where(n < REL_MAX_EXACT, n, jnp.minimum(large, REL_BUCKETS - 1))


def _bias_body(tab_ref, *refs):
    h = pl.program_id(0)
    n_tab = len(refs) // 2
    for bk_ref, out_ref in zip(refs[:n_tab], refs[n_tab:]):
        bk = bk_ref[...]
        acc = jnp.full(bk.shape, NEG, F32)
        for bucket in range(REL_BUCKETS):
            acc = jnp.where(bk == bucket, tab_ref[h, bucket], acc)
        out_ref[0] = acc


def _bias_tables(rel_bias, ncp, nq):
    tab = rel_bias.T.astype(F32) * LOG2E
    n_heads = tab.shape[0]
    c = jnp.arange(TQ)[:, None]
    r = jnp.arange(TQ)[None, :]
    d_own = r - c
    bk_own = jnp.where(d_own >= 0, _rel_bucket(d_own), -1)
    bk_prev = _rel_bucket(d_own + TQ)
    bk_win2 = jnp.where(d_own + 2 * TQ < WINDOW, _rel_bucket(d_own + 2 * TQ), -1)
    band_rows = ncp + (nq - 1) * (TQ // CMP_STRIDE)
    u = jnp.arange(band_rows)[:, None] - (nq - 1) * (TQ // CMP_STRIDE)
    d_cmp = r - CMP_STRIDE * u - (CMP_LEN - 1)
    bk_cmp = jnp.where(d_cmp >= 0, _rel_bucket(d_cmp), -1)
    tables = [t.astype(jnp.int32) for t in (jnp.concatenate([bk_win2, bk_prev, bk_own], axis=0), bk_cmp)]
    outs = pl.pallas_call(
        _bias_body,
        grid=(n_heads,),
        in_specs=[pl.BlockSpec(memory_space=pltpu.SMEM)]
        + [pl.BlockSpec(t.shape, lambda h: (0, 0)) for t in tables],
        out_specs=[pl.BlockSpec((1, t.shape[0], TQ), lambda h: (h // HG, 0, h % HG)) for t in tables],
        out_shape=[jax.ShapeDtypeStruct((n_heads // HG, t.shape[0], WIDE), F32) for t in tables],
        compiler_params=pltpu.CompilerParams(dimension_semantics=("parallel",), vmem_limit_bytes=VMEM_LIMIT),
        name="bias_tiles",
    )(tab, *tables)
    return outs, tab[:, REL_BUCKETS - 1]


def _slc_overlap_t(ncp, nslc):
    ratio = SLC_BLOCK // CMP_STRIDE
    mc = CMP_LEN // CMP_STRIDE
    w = np.zeros((HEAD_PAD, nslc * ratio + mc), np.float32)
    for j in range(nslc):
        for a in range(ratio):
            for m in range(mc):
                w[j, ratio * j + a + m] += 1.0
    n_cmp = ncp - 1
    out = np.zeros((HEAD_PAD, ncp), np.float32)
    out[:, :n_cmp] = w[:, :n_cmp]
    return out


def _block_onehot(s, block):
    pat = np.zeros((s, HEAD_PAD), np.float32)
    pat[np.arange(s), EXT0 + np.arange(s) // block] = 1.0
    return pat


def kernel(x, norm_ffn1, w_ffn1_gate, w_ffn1_up, w_ffn1_down, norm_mix, w_in, cmp_pos_k, cmp_w1_k, cmp_w2_k,
           cmp_pos_v, cmp_w1_v, cmp_w2_v, w_out, norm_ffn2, w_ffn2_gate, w_ffn2_up, w_ffn2_down, rel_bias,
           norm_final):
    b, s, d = x.shape
    assert d == D_MODEL and s % (2 * TQ) == 0 and norm_ffn1.shape[0] == 1
    assert s // SLC_BLOCK <= HEAD_PAD - EXT0, "selection-mask rows must fit beside the 64 head dims"
    n = b * s
    nq = s // TQ
    ncp = s // CMP_STRIDE
    nslc = s // SLC_BLOCK
    scale = HEAD_DIM ** -0.5 * LOG2E

    sizes = (512, 512, 512, 512) + (128,) * 6 + (24,)
    splits = np.cumsum(sizes)[:-1].tolist()
    wqa, wka, wva, wqb, wkc, wvc, wks, wvs, wkw, wvw, wgb = jnp.split(w_in[0], splits, axis=1)
    w_k = jnp.concatenate([wka, wkc, wvc, wks, wkw], axis=1).astype(BF16)
    w_t = jnp.concatenate([wqa * scale, wva, wqb * scale, wvs, wvw], axis=1).T.astype(BF16)
    assert w_k.shape[1] == N_KSLOTS * HEAD_PAD and w_t.shape[0] == N_TROWS * HEAD_DIM
    wgt = wgb.reshape(D_MODEL, NSA_GROUPS, NSA_REP * N_GATES)
    wgt = jnp.pad(wgt, ((0, 0), (0, 0), (0, GATE_ROWS - NSA_REP * N_GATES)))
    wgt = wgt.reshape(D_MODEL, NSA_GROUPS * GATE_ROWS).T.astype(BF16)

    def cmp_params(pos, w1, w2):
        w1_h = w1[0].reshape(CMP_LEN, HEAD_DIM, CMP_HIDDEN)
        pos_g, w1_g = [], []
        for grp in range(NSA_GROUPS):
            pad = ((0, 0), (grp * HEAD_DIM, HEAD_PAD - (grp + 1) * HEAD_DIM))
            pos_g.append(jnp.pad(pos[0], pad))
            w1_g.append(jnp.pad(w1_h, pad + ((0, 0),)))
        w2_p = jnp.pad(w2[0], ((0, 0), (0, HEAD_PAD - HEAD_DIM)))
        return jnp.stack(pos_g), jnp.stack(w1_g).astype(BF16), w2_p.astype(BF16)

    pos_k, w1k, w2k = cmp_params(cmp_pos_k, cmp_w1_k, cmp_w2_k)
    pos_v, w1v, w2v = cmp_params(cmp_pos_v, cmp_w1_v, cmp_w2_v)
    cmp_pos = jnp.stack([pos_k, pos_v])
    cmp_w1 = jnp.stack([w1k, w1v])

    wo = w_out[0]
    wo_a = wo[:MOBA_HEADS * HEAD_DIM].astype(BF16)
    wo_b = wo[MOBA_HEADS * HEAD_DIM:].astype(BF16)

    (bias_win, bias_cmp), c31 = _bias_tables(rel_bias, ncp, nq)
    ovt = jnp.asarray(_slc_overlap_t(ncp, nslc)).astype(BF16)
    pat_moba = jnp.asarray(_block_onehot(s, MOBA_BLOCK)).astype(BF16)
    pat_slc = jnp.asarray(_block_onehot(s, SLC_BLOCK)).astype(BF16)
    far = [(i, j) for j in range(nq) for i in range(j + 2, nq)]
    far_i = jnp.asarray(np.array([i for i, _ in far], np.int32))
    far_j = jnp.asarray(np.array([j for _, j in far], np.int32))

    x2d = x.reshape(n, D_MODEL)
    nf = norm_final.reshape(1, D_MODEL)
    x1 = _ffn(x2d, norm_ffn1, w_ffn1_gate[0].astype(BF16), w_ffn1_up[0].astype(BF16),
              w_ffn1_down[0].astype(BF16), nf, final_norm=False)
    pk, pt, gates_t = _inproj(x1, norm_mix, w_k, w_t, wgt, b, s)
    pk = pk.reshape(b, s, N_KSLOTS * HEAD_PAD)
    kc, vct = _compress(pk, cmp_pos, cmp_w1, w2k, w2v.T)
    o_a = _moba(pk, pt, pat_moba, bias_win, c31, far_i, far_j)
    o_b = _nsa(pk, pt, kc, vct, gates_t, pat_slc, bias_win, bias_cmp, ovt, c31, far_i, far_j)
    y = _ffn(x1, norm_ffn2, w_ffn2_gate[0].astype(BF16), w_ffn2_up[0].astype(BF16),
             w_ffn2_down[0].astype(BF16), nf, final_norm=True, mixer=(o_a, o_b, wo_a, wo_b))
    return y.reshape(b, s, D_MODEL)
```

```python
import functools
import math

import numpy as np
import jax
import jax.numpy as jnp
from jax import lax
from jax.experimental import pallas as pl
from jax.experimental.pallas import tpu as pltpu

D_MODEL = 1024
D_FF = 2816
HEAD_DIM = 64
HEAD_PAD = 128
ONES_ROW = HEAD_DIM
EXT0 = HEAD_DIM
MOBA_HEADS = 8
NSA_HEADS = 8
NSA_GROUPS = 2
NSA_REP = NSA_HEADS // NSA_GROUPS
HG = 4
MOBA_BLOCK = 256
MOBA_TOPK = 3
CMP_LEN = 32
CMP_STRIDE = 16
CMP_HIDDEN = 256
SLC_BLOCK = 64
SLC_TOPN = 16
WINDOW = 512
N_GATES = 3
GATE_ROWS = 16
REL_BUCKETS = 32
REL_MAX_EXACT = REL_BUCKETS // 2
REL_MAX_DIST = 128
RMS_EPS = 1e-6
TQ = 256
WIDE = HG * TQ
NEG = -1e30
LOG2E = 1.4426950408889634
ACC_ROWS = 80
VMEM_LIMIT = 56 * 1024 * 1024

KSLOT_KA, KSLOT_KC, KSLOT_VC, KSLOT_KS, KSLOT_KW = 0, 4, 5, 6, 7
N_KSLOTS = 8
TROW_QA, TROW_VA, TROW_QB, TROW_VS, TROW_VW = 0, 8, 16, 24, 26
N_TROWS = 28

F32 = jnp.float32
BF16 = jnp.bfloat16


def _dot(a, b):
    return jnp.dot(a, b, preferred_element_type=F32)


def _dot_nt(a, b):
    return lax.dot_general(a, b, (((1,), (1,)), ((), ())), preferred_element_type=F32)


def _split_bf16(x):
    hi = x.astype(BF16)
    lo = (x - hi.astype(F32)).astype(BF16)
    return hi, lo


def _rmsnorm(x, g):
    ms = jnp.mean(x * x, axis=-1, keepdims=True)
    return x * lax.rsqrt(ms + RMS_EPS) * g


def _dot_tn(a_t, b):
    return lax.dot_general(a_t, b, (((0,), (0,)), ((), ())), preferred_element_type=F32)


def _ffn_body(x_ref, g_ref, wg_ref, wu_ref, wd_ref, gf_ref, *rest, final_norm, n_chunks, with_mixer):
    x = x_ref[...]
    if with_mixer:
        oa_ref, ob_ref, woa_ref, wob_ref, o_ref = rest
        x = x + _dot_tn(oa_ref[0], woa_ref[...]) + _dot_tn(ob_ref[0], wob_ref[...])
    else:
        (o_ref,) = rest
    h = _rmsnorm(x, g_ref[...]).astype(BF16)
    tf = D_FF // n_chunks
    acc = None
    for c in range(n_chunks):
        a = _dot(h, wg_ref[:, c * tf:(c + 1) * tf])
        u = _dot(h, wu_ref[:, c * tf:(c + 1) * tf])
        z = ((a * jax.nn.sigmoid(a)) * u).astype(BF16)
        part = _dot(z, wd_ref[c * tf:(c + 1) * tf, :])
        acc = part if acc is None else acc + part
    y = x + 0.5 * acc
    if final_norm:
        y = _rmsnorm(y, gf_ref[...])
    o_ref[...] = y


def _ffn(x2d, g, wg, wu, wd, gf, *, final_norm, mixer=None, tm=512, n_chunks=1):
    n = x2d.shape[0]
    resident = lambda shape: pl.BlockSpec(shape, lambda i: (0, 0), pipeline_mode=pl.Buffered(1))
    in_specs = [
        pl.BlockSpec((tm, D_MODEL), lambda i: (i, 0)),
        resident((1, D_MODEL)),
        resident((D_MODEL, D_FF)), resident((D_MODEL, D_FF)), resident((D_FF, D_MODEL)),
        resident((1, D_MODEL)),
    ]
    args = [x2d, g, wg, wu, wd, gf]
    if mixer is not None:
        o_a, o_b, wo_a, wo_b = mixer
        spt = o_a.shape[2] // tm
        for o_t in (o_a, o_b):
            in_specs.append(pl.BlockSpec((1, o_t.shape[1], tm), lambda i: (i // spt, 0, i % spt)))
        in_specs += [resident(wo_a.shape), resident(wo_b.shape)]
        args += [o_a, o_b, wo_a, wo_b]
    return pl.pallas_call(
        functools.partial(_ffn_body, final_norm=final_norm, n_chunks=n_chunks, with_mixer=mixer is not None),
        grid=(n // tm,),
        in_specs=in_specs,
        out_specs=pl.BlockSpec((tm, D_MODEL), lambda i: (i, 0)),
        out_shape=jax.ShapeDtypeStruct((n, D_MODEL), F32),
        compiler_params=pltpu.CompilerParams(
            dimension_semantics=("parallel",), vmem_limit_bytes=VMEM_LIMIT),
        name="ffn_final" if final_norm else "ffn",
    )(*args)


def _inproj_body(x_ref, g_ref, wk_ref, wt_ref, wgt_ref, pk_ref, pt_ref, gates_ref):
    h = _rmsnorm(x_ref[...], g_ref[...]).astype(BF16)
    pk_ref[...] = _dot(h, wk_ref[...]).astype(BF16)
    pt_ref[0] = _dot_nt(wt_ref[...], h).astype(BF16)
    gates_ref[0] = jax.nn.sigmoid(_dot_nt(wgt_ref[...], h))


def _inproj(x2d, g, wk, wt, wgt, b, s, *, tm=512):
    n = x2d.shape[0]
    nk = wk.shape[1]
    nt = wt.shape[0]
    ng = wgt.shape[0]
    spt = s // tm
    const = lambda shape: pl.BlockSpec(shape, lambda i: (0, 0))
    return pl.pallas_call(
        _inproj_body,
        grid=(n // tm,),
        in_specs=[
            pl.BlockSpec((tm, D_MODEL), lambda i: (i, 0)),
            const((1, D_MODEL)), const((D_MODEL, nk)), const((nt, D_MODEL)), const((ng, D_MODEL)),
        ],
        out_specs=[
            pl.BlockSpec((tm, nk), lambda i: (i, 0)),
            pl.BlockSpec((1, nt, tm), lambda i: (i // spt, 0, i % spt)),
            pl.BlockSpec((1, ng, tm), lambda i: (i // spt, 0, i % spt)),
        ],
        out_shape=[jax.ShapeDtypeStruct((n, nk), BF16), jax.ShapeDtypeStruct((b, nt, s), BF16),
                   jax.ShapeDtypeStruct((b, ng, s), F32)],
        compiler_params=pltpu.CompilerParams(
            dimension_semantics=("parallel",), vmem_limit_bytes=VMEM_LIMIT),
        name="inproj",
    )(x2d, g, wk, wt, wgt)


def _compress_body(kraw_ref, vraw_ref, pos_ref, w1_ref, w2k_ref, w2vt_ref, kc_ref, vct_ref, xs_ref, *, ncp):
    half = CMP_LEN // 2
    for kv, raw_ref in enumerate((kraw_ref, vraw_ref)):
        xs_ref[...] = raw_ref[0].astype(F32)
        y0 = jnp.zeros((ncp, CMP_HIDDEN), F32)
        y1 = jnp.zeros((ncp, CMP_HIDDEN), F32)
        for p in range(half):
            xp = xs_ref[pl.ds(p, ncp, stride=CMP_STRIDE), :]
            y0 = y0 + _dot((xp + pos_ref[kv, 0, p:p + 1, :]).astype(BF16), w1_ref[kv, 0, p])
            y1 = y1 + _dot((xp + pos_ref[kv, 0, half + p:half + p + 1, :]).astype(BF16), w1_ref[kv, 0, half + p])
        h1 = y0 + pltpu.roll(y1, ncp - 1, 0)
        act = (h1 * jax.nn.sigmoid(h1)).astype(BF16)
        if kv == 0:
            kc_ref[0, 0] = _dot(act, w2k_ref[...]).astype(BF16)
        else:
            out_t = _dot_nt(w2vt_ref[...], act)
            row = lax.broadcasted_iota(jnp.int32, out_t.shape, 0)
            vct_ref[0, 0] = (out_t + jnp.where(row == ONES_ROW, 1.0, 0.0)).astype(BF16)


def _compress(pk, pos, w1, w2k, w2vt):
    b, s, _ = pk.shape
    ncp = s // CMP_STRIDE
    full = lambda shape: pl.BlockSpec(shape, lambda bi, g: (0,) * len(shape))
    return pl.pallas_call(
        functools.partial(_compress_body, ncp=ncp),
        grid=(b, NSA_GROUPS),
        in_specs=[
            pl.BlockSpec((1, s, HEAD_PAD), lambda bi, g: (bi, 0, KSLOT_KC)),
            pl.BlockSpec((1, s, HEAD_PAD), lambda bi, g: (bi, 0, KSLOT_VC)),
            pl.BlockSpec((2, 1, CMP_LEN, HEAD_PAD), lambda bi, g: (0, g, 0, 0)),
            pl.BlockSpec((2, 1, CMP_LEN, HEAD_PAD, CMP_HIDDEN), lambda bi, g: (0, g, 0, 0, 0)),
            full((CMP_HIDDEN, HEAD_PAD)),
            full((HEAD_PAD, CMP_HIDDEN)),
        ],
        out_specs=[
            pl.BlockSpec((1, 1, ncp, HEAD_PAD), lambda bi, g: (bi, g, 0, 0)),
            pl.BlockSpec((1, 1, HEAD_PAD, ncp), lambda bi, g: (bi, g, 0, 0)),
        ],
        out_shape=[
            jax.ShapeDtypeStruct((b, NSA_GROUPS, ncp, HEAD_PAD), BF16),
            jax.ShapeDtypeStruct((b, NSA_GROUPS, HEAD_PAD, ncp), BF16),
        ],
        scratch_shapes=[pltpu.VMEM((s, HEAD_PAD), F32)],
        compiler_params=pltpu.CompilerParams(
            dimension_semantics=("parallel", "arbitrary"), vmem_limit_bytes=VMEM_LIMIT),
        name="compress",
    )(pk, pk, pos, w1, w2k, w2vt)


def _flash_update(s_t, shift, pv_fn, m_ref, acc_ref):
    m_prev = m_ref[...]
    m_new = jnp.maximum(m_prev, jnp.max(s_t, axis=0, keepdims=True) + shift)
    alpha = jnp.exp2(m_prev - m_new)
    p = jnp.exp2(s_t - (m_new - shift)).astype(BF16)
    acc_ref[...] = alpha * acc_ref[...] + pv_fn(p)
    m_ref[...] = m_new


def _state_lanes(i):
    return pl.ds(pl.multiple_of(i * WIDE, WIDE), WIDE)


def _tile_phase(count, tile_fn, scores_fn, step_fn, sa_ref, sb_ref):
    if count == 0:
        return
    tile = lambda n: tile_fn(jnp.minimum(jnp.asarray(n, jnp.int32), count - 1))
    sa_ref[...] = scores_fn(*tile(0))

    def pair(p, carry):
        n = 2 * p
        sb_ref[...] = scores_fn(*tile(n + 1))
        step_fn(sa_ref, *tile(n))
        sa_ref[...] = scores_fn(*tile(n + 2))
        step_fn(sb_ref, *tile(n + 1))
        return carry

    lax.fori_loop(0, count // 2, pair, 0, unroll=4)
    if count % 2:
        step_fn(sa_ref, *tile(count - 1))


def _flat_stream(nq, n_far, far_i_ref, far_j_ref, scores_fn, pv_fn, bias_ref, shift_far,
                 sa_ref, sb_ref, m_ref, acc_ref):
    zero = jnp.zeros((1, WIDE), F32)

    def own_step(s_ref, i, j):
        s_t = s_ref[...] + bias_ref[0, 2 * TQ:3 * TQ, :]
        m = jnp.max(s_t, axis=0, keepdims=True)
        m_ref[:, _state_lanes(i)] = m
        acc_ref[:, _state_lanes(i)] = pv_fn(j)(jnp.exp2(s_t - m).astype(BF16))

    def prev_step(s_ref, i, j):
        _flash_update(s_ref[...] + bias_ref[0, TQ:2 * TQ, :], zero, pv_fn(j),
                      m_ref.at[:, _state_lanes(i)], acc_ref.at[:, _state_lanes(i)])

    def far_step(s_ref, i, j):
        _flash_update(s_ref[...], shift_far, pv_fn(j), m_ref.at[:, _state_lanes(i)], acc_ref.at[:, _state_lanes(i)])

    _tile_phase(nq, lambda n: (n, n), scores_fn, own_step, sa_ref, sb_ref)
    _tile_phase(nq - 1, lambda n: (n + 1, n), scores_fn, prev_step, sa_ref, sb_ref)
    _tile_phase(n_far, lambda n: (far_i_ref[n], far_j_ref[n]), scores_fn, far_step, sa_ref, sb_ref)


def _lane_block(x, r):
    return x[:, r * TQ:(r + 1) * TQ]


def _shift_row(c31_ref, head0):
    return jnp.concatenate([jnp.full((1, TQ), c31_ref[head0 + r], F32) for r in range(HG)], axis=1)


def _rank_select(score, n_rows, top, row_weight=None):
    if row_weight is not None:
        blk = lax.broadcasted_iota(jnp.int32, score.shape, 0)
        cnt = jnp.zeros(score.shape, F32)
        for j in range(n_rows):
            row = score[j:j + 1, :]
            ahead = jnp.where(row > score, 1.0, jnp.where((row == score) & (blk > j), 1.0, 0.0))
            cnt = cnt + ahead * row_weight(j)
        return cnt < top
    sub = 8
    n_groups = score.shape[0] // sub
    groups = [score[g * sub:(g + 1) * sub] for g in range(n_groups)]
    counts = [jnp.zeros(groups[0].shape, F32) for _ in range(n_groups)]
    local = lax.broadcasted_iota(jnp.int32, groups[0].shape, 0)
    for j in range(n_rows):
        row = score[j:j + 1, :]
        weight = None if row_weight is None else row_weight(j)
        for g in range(n_groups):
            one = 1.0 if weight is None else weight[g * sub:(g + 1) * sub]
            if j < g * sub:
                ahead = jnp.where(row >= groups[g], one, 0.0)
            elif j >= (g + 1) * sub:
                ahead = jnp.where(row > groups[g], one, 0.0)
            else:
                ahead = jnp.where(local > j - g * sub, jnp.where(row >= groups[g], one, 0.0),
                                  jnp.where(row > groups[g], one, 0.0))
            counts[g] = counts[g] + ahead
    return jnp.concatenate(counts, axis=0) < top


def _key_operand(slot, half, pat):
    if not (isinstance(half, int) and half == 0):
        slot = pltpu.roll(slot.astype(F32), HEAD_DIM * half, 1).astype(BF16)
    lane = lax.broadcasted_iota(jnp.int32, slot.shape, 1)
    return jnp.where(lane < HEAD_DIM, slot, pat)


def _value_operand(v_t):
    rows = lax.broadcasted_iota(jnp.int32, (ACC_ROWS - HEAD_DIM, v_t.shape[1]), 0)
    return jnp.concatenate([v_t, jnp.where(rows == 0, 1.0, 0.0).astype(BF16)], axis=0)


def _moba_body(c31_ref, far_i_ref, far_j_ref, q_ref, k_ref, vt_ref, pat_ref, bias_ref, o_ref,
               kmean_ref, kx_ref, vx_ref, qx_ref, sa_ref, sb_ref, m_ref, acc_ref, *, nb, n_far):
    hg = pl.program_id(1)
    s = q_ref.shape[2]
    nbp = -(-nb // 16) * 16

    kmean_ref[...] = jnp.zeros_like(kmean_ref)
    blk = lax.broadcasted_iota(jnp.int32, (nbp, s), 0)
    cur = lax.broadcasted_iota(jnp.int32, (nbp, s), 1) // MOBA_BLOCK
    zero_rows = jnp.zeros((HEAD_PAD - HEAD_DIM, s), BF16)
    for r in range(HG):
        kx = _key_operand(k_ref[0, :, (r // 2) * HEAD_PAD:(r // 2 + 1) * HEAD_PAD], r % 2, pat_ref[...])
        kx_ref[:, r * HEAD_PAD:(r + 1) * HEAD_PAD] = kx
        vx_ref[r] = _value_operand(vt_ref[0, r * HEAD_DIM:(r + 1) * HEAD_DIM, :])
        for n in range(nb):
            kb = kx[n * MOBA_BLOCK:(n + 1) * MOBA_BLOCK, :].astype(F32)
            kmean_ref[r, n:n + 1, :] = jnp.mean(kb, axis=0, keepdims=True)
        q_t = q_ref[0, r * HEAD_DIM:(r + 1) * HEAD_DIM, :]
        q_pad = jnp.concatenate([q_t, zero_rows], axis=0)
        kh, kl = _split_bf16(kmean_ref[r, :nbp, :])
        gate = _dot(kh, q_pad) + _dot(kl, q_pad)
        top = _rank_select(gate, nb, MOBA_TOPK, lambda n: jnp.where(n < cur, 1.0, 0.0))
        sel = ((blk < cur) & top) | (blk == cur)
        mask_rows = jnp.where(sel, 0.0, NEG).astype(BF16)
        pad_rows = jnp.zeros((HEAD_PAD - EXT0 - nbp, s), BF16)
        qx_ref[r] = jnp.concatenate([q_t, mask_rows, pad_rows], axis=0)

    def scores(i, j):
        qoff = pl.multiple_of(i * TQ, TQ)
        koff = pl.multiple_of(j * TQ, TQ)
        return jnp.concatenate(
            [_dot(kx_ref[pl.ds(koff, TQ), r * HEAD_PAD:(r + 1) * HEAD_PAD], qx_ref[r, :, pl.ds(qoff, TQ)])
             for r in range(HG)], axis=1)

    def pv_fn(j):
        off = pl.multiple_of(j * TQ, TQ)

        def fn(p):
            return jnp.concatenate(
                [_dot(vx_ref[r, :, pl.ds(off, TQ)], _lane_block(p, r)) for r in range(HG)], axis=1)
        return fn

    _flat_stream(nb, n_far, far_i_ref, far_j_ref, scores, pv_fn, bias_ref, _shift_row(c31_ref, hg * HG),
                 sa_ref, sb_ref, m_ref, acc_ref)

    def finish(i, carry):
        acc = acc_ref[:, _state_lanes(i)]
        out = (acc[:HEAD_DIM] / acc[ONES_ROW:ONES_ROW + 1, :]).astype(BF16)
        for r in range(HG):
            o_ref[0, r * HEAD_DIM:(r + 1) * HEAD_DIM, pl.ds(pl.multiple_of(i * TQ, TQ), TQ)] = _lane_block(out, r)
        return carry

    lax.fori_loop(0, nb, finish, 0)


def _moba(pk, pt, pat, bias_win, c31, far_i, far_j):
    b, s, _ = pk.shape
    nb = s // MOBA_BLOCK
    hrows = HG * HEAD_DIM
    smem = pl.BlockSpec(memory_space=pltpu.SMEM)
    return pl.pallas_call(
        functools.partial(_moba_body, nb=nb, n_far=far_i.shape[0]),
        grid=(b, MOBA_HEADS // HG),
        in_specs=[
            smem, smem, smem,
            pl.BlockSpec((1, hrows, s), lambda bi, hg: (bi, TROW_QA // HG + hg, 0)),
            pl.BlockSpec((1, s, hrows), lambda bi, hg: (bi, 0, KSLOT_KA * HEAD_PAD // hrows + hg)),
            pl.BlockSpec((1, hrows, s), lambda bi, hg: (bi, TROW_VA // HG + hg, 0)),
            pl.BlockSpec((s, HEAD_PAD), lambda bi, hg: (0, 0), pipeline_mode=pl.Buffered(1)),
            pl.BlockSpec((1, 3 * TQ, WIDE), lambda bi, hg: (hg, 0, 0)),
        ],
        out_specs=pl.BlockSpec((1, HG * HEAD_DIM, s), lambda bi, hg: (bi, hg, 0)),
        out_shape=jax.ShapeDtypeStruct((b, MOBA_HEADS * HEAD_DIM, s), BF16),
        scratch_shapes=[
            pltpu.VMEM((HG, HEAD_PAD, HEAD_PAD), F32),
            pltpu.VMEM((s, HG * HEAD_PAD), BF16),
            pltpu.VMEM((HG, ACC_ROWS, s), BF16),
            pltpu.VMEM((HG, HEAD_PAD, s), BF16),
            pltpu.VMEM((TQ, WIDE), F32),
            pltpu.VMEM((TQ, WIDE), F32),
            pltpu.VMEM((1, nb * WIDE), F32),
            pltpu.VMEM((ACC_ROWS, nb * WIDE), F32),
        ],
        compiler_params=pltpu.CompilerParams(
            dimension_semantics=("parallel", "arbitrary"), vmem_limit_bytes=VMEM_LIMIT),
        name="moba",
    )(c31, far_i, far_j, pt, pk, pt, pat, bias_win)


def _nsa_body(c31_ref, far_i_ref, far_j_ref, q_ref, ks_ref, kw_ref, vst_ref, vwt_ref, kc_ref, vct_ref,
              gates_ref, pat_ref, bias_ref, bcmp_ref, ovt_ref, o_ref,
              ksx_ref, kwx_ref, vsx_ref, vwx_ref, qx_ref, sa_ref, sb_ref, m_ref, accs_ref, part_ref,
              *, ncp, nslc, nq, n_far):
    g = pl.program_id(1)

    ksx_ref[...] = _key_operand(ks_ref[0], g, pat_ref[...])
    kwx_ref[...] = _key_operand(kw_ref[0], g, jnp.zeros((), BF16))
    grows = pl.ds(pl.multiple_of(g * HEAD_DIM, HEAD_DIM), HEAD_DIM)
    vsx_ref[...] = _value_operand(vst_ref[0, grows, :])
    vwx_ref[...] = _value_operand(vwt_ref[0, grows, :])
    zero_rows = jnp.zeros((HEAD_PAD - HEAD_DIM, WIDE), BF16)

    def front(i, n_win, n_act):
        qoff = pl.multiple_of(i * TQ, TQ)
        q4c = jnp.concatenate([q_ref[0, r * HEAD_DIM:(r + 1) * HEAD_DIM, pl.ds(qoff, TQ)] for r in range(HG)],
                              axis=1)
        q4 = jnp.concatenate([q4c, zero_rows], axis=0)

        band_off = pl.multiple_of((nq - 1 - i) * (TQ // CMP_STRIDE), TQ // CMP_STRIDE)
        s_t = _dot(kc_ref[0, 0], q4) + bcmp_ref[0, pl.ds(band_off, ncp), :]
        m = jnp.max(s_t, axis=0, keepdims=True)
        p = jnp.exp2(s_t - m)
        den = jnp.sum(p, axis=0, keepdims=True)
        pn = p * jnp.where(m > 0.5 * NEG, 1.0 / den, 0.0)
        o_cmp = _dot(vct_ref[0, 0, :ACC_ROWS, :], pn.astype(BF16))
        psum = _lane_block(pn, 0)
        for r in range(1, HG):
            psum = psum + _lane_block(pn, r)

        ph, plo = _split_bf16(psum)
        imp = (_dot(ovt_ref[...], ph) + _dot(ovt_ref[...], plo))[:n_act]
        blk = lax.broadcasted_iota(jnp.int32, imp.shape, 0)
        cur = (i * TQ + lax.broadcasted_iota(jnp.int32, imp.shape, 1)) // SLC_BLOCK
        forced = (blk == 0) | (blk == cur) | (blk == cur - 1)
        score = jnp.where(forced, 1e30, jnp.where(blk <= cur, imp, -1.0))
        sel = _rank_select(score, n_act, SLC_TOPN)
        mask_rows = jnp.where(sel, 0.0, NEG)
        if n_act < EXT0:
            mask_rows = jnp.concatenate([mask_rows, jnp.full((EXT0 - n_act, TQ), NEG, F32)], axis=0)
        mask_rows = mask_rows.astype(BF16)
        qx_ref[:, _state_lanes(i)] = jnp.concatenate(
            [q4c, jnp.concatenate([mask_rows] * HG, axis=1)], axis=0)

        start = pl.multiple_of((i + 1 - n_win) * TQ, TQ)
        s_w = _dot(kwx_ref[pl.ds(start, n_win * TQ), :], q4) + bias_ref[0, (3 - n_win) * TQ:, :]
        p_w = jnp.exp2(s_w - jnp.max(s_w, axis=0, keepdims=True)).astype(BF16)
        acc_w = _dot(vwx_ref[:, pl.ds(start, n_win * TQ)], p_w)
        o_win = acc_w / acc_w[ONES_ROW:ONES_ROW + 1, :]

        gt = gates_ref[0, :, pl.ds(qoff, TQ)]
        part_ref[:, _state_lanes(i)] = jnp.concatenate(
            [gt[N_GATES * r:N_GATES * r + 1, :] * _lane_block(o_cmp, r)
             + gt[N_GATES * r + 2:N_GATES * r + 3, :] * _lane_block(o_win, r) for r in range(HG)], axis=1)

    blocks_upto = lambda i_end: min(-(-nslc // 8) * 8, -(-i_end * TQ // SLC_BLOCK // 8) * 8)
    for i0 in range(min(2, nq)):
        front(jnp.asarray(i0, jnp.int32), i0 + 1, blocks_upto(i0 + 1))

    mid = max(2, nq // 2)
    for lo, hi in ((2, mid), (mid, nq)):
        def front_loop(i, carry, n_act=blocks_upto(hi)):
            front(i, 3, n_act)
            return carry

        if hi > lo:
            lax.fori_loop(lo, hi, front_loop, 0, unroll=2)

    def scores(i, j):
        return _dot(ksx_ref[pl.ds(pl.multiple_of(j * TQ, TQ), TQ), :], qx_ref[:, _state_lanes(i)])

    def pv_fn(j):
        off = pl.multiple_of(j * TQ, TQ)
        return lambda p: _dot(vsx_ref[:, pl.ds(off, TQ)], p)

    _flat_stream(nq, n_far, far_i_ref, far_j_ref, scores, pv_fn, bias_ref,
                 _shift_row(c31_ref, MOBA_HEADS + g * HG), sa_ref, sb_ref, m_ref, accs_ref)

    def finish(i, carry):
        qoff = pl.multiple_of(i * TQ, TQ)
        acc = accs_ref[:, _state_lanes(i)]
        o_slc = acc / acc[ONES_ROW:ONES_ROW + 1, :]
        part = part_ref[:, _state_lanes(i)]
        gt = gates_ref[0, :, pl.ds(qoff, TQ)]
        for r in range(HG):
            out = _lane_block(part, r) + gt[N_GATES * r + 1:N_GATES * r + 2, :] * _lane_block(o_slc, r)
            o_ref[0, r * HEAD_DIM:(r + 1) * HEAD_DIM, pl.ds(qoff, TQ)] = out[:HEAD_DIM].astype(BF16)
        return carry

    lax.fori_loop(0, nq, finish, 0)


def _nsa(pk, pt, kc, vct, gates_t, pat, bias_win, bias_cmp, ovt, c31, far_i, far_j):
    b, s, _ = pk.shape
    nq = s // TQ
    ncp = s // CMP_STRIDE
    nslc = s // SLC_BLOCK
    hrows = HG * HEAD_DIM
    band_rows = bias_cmp.shape[1]
    hgrp = MOBA_HEADS // HG
    smem = pl.BlockSpec(memory_space=pltpu.SMEM)
    k_spec = lambda slot: pl.BlockSpec((1, s, HEAD_PAD), lambda bi, g: (bi, 0, slot))
    vt_spec = lambda row: pl.BlockSpec((1, HEAD_PAD, s), lambda bi, g: (bi, row * HEAD_DIM // HEAD_PAD, 0))
    bias_spec = lambda rows: pl.BlockSpec((1, rows, WIDE), lambda bi, g: (hgrp + g, 0, 0),
                                          pipeline_mode=pl.Buffered(1))
    return pl.pallas_call(
        functools.partial(_nsa_body, ncp=ncp, nslc=nslc, nq=nq, n_far=far_i.shape[0]),
        grid=(b, NSA_GROUPS),
        in_specs=[
            smem, smem, smem,
            pl.BlockSpec((1, hrows, s), lambda bi, g: (bi, TROW_QB // HG + g, 0)),
            k_spec(KSLOT_KS), k_spec(KSLOT_KW), vt_spec(TROW_VS), vt_spec(TROW_VW),
            pl.BlockSpec((1, 1, ncp, HEAD_PAD), lambda bi, g: (bi, g, 0, 0)),
            pl.BlockSpec((1, 1, HEAD_PAD, ncp), lambda bi, g: (bi, g, 0, 0)),
            pl.BlockSpec((1, GATE_ROWS, s), lambda bi, g: (bi, g, 0)),
            pl.BlockSpec((s, HEAD_PAD), lambda bi, g: (0, 0), pipeline_mode=pl.Buffered(1)),
            bias_spec(3 * TQ), bias_spec(band_rows),
            pl.BlockSpec((HEAD_PAD, ncp), lambda bi, g: (0, 0)),
        ],
        out_specs=pl.BlockSpec((1, HG * HEAD_DIM, s), lambda bi, g: (bi, g, 0)),
        out_shape=jax.ShapeDtypeStruct((b, NSA_HEADS * HEAD_DIM, s), BF16),
        scratch_shapes=[
            pltpu.VMEM((s, HEAD_PAD), BF16),
            pltpu.VMEM((s, HEAD_PAD), BF16),
            pltpu.VMEM((ACC_ROWS, s), BF16),
            pltpu.VMEM((ACC_ROWS, s), BF16),
            pltpu.VMEM((HEAD_PAD, nq * WIDE), BF16),
            pltpu.VMEM((TQ, WIDE), F32),
            pltpu.VMEM((TQ, WIDE), F32),
            pltpu.VMEM((1, nq * WIDE), F32),
            pltpu.VMEM((ACC_ROWS, nq * WIDE), F32),
            pltpu.VMEM((ACC_ROWS, nq * WIDE), F32),
        ],
        compiler_params=pltpu.CompilerParams(
            dimension_semantics=("parallel", "arbitrary"), vmem_limit_bytes=VMEM_LIMIT),
        name="nsa",
    )(c31, far_i, far_j, pt, pk, pk, pt, pt, kc, vct, gates_t, pat, bias_win, bias_cmp, ovt)


def _rel_bucket(dist):
    n = jnp.maximum(dist, 0)
    nf = jnp.maximum(n, 1).astype(F32)
    large = REL_MAX_EXACT + (jnp.log(nf / REL_MAX_EXACT) / math.log(REL_MAX_DIST / REL_MAX_EXACT)
                             * (REL_BUCKETS - REL_MAX_EXACT)).astype(jnp.int32)
    return jnp.where(n < REL_MAX_EXACT, n, jnp.minimum(large, REL_BUCKETS - 1))


def _bias_body(tab_ref, *refs, runs):
    h = pl.program_id(0)
    n_tab = len(refs) // 2
    for bk_ref, out_ref, table_runs in zip(refs[:n_tab], refs[n_tab:], runs):
        for start, stop, far in table_runs:
            bk = bk_ref[start:stop, :]
            if far:
                acc = jnp.where(bk >= 0, tab_ref[h, REL_BUCKETS - 1], NEG)
            else:
                acc = jnp.full(bk.shape, NEG, F32)
                for bucket in range(REL_BUCKETS):
                    acc = jnp.where(bk == bucket, tab_ref[h, bucket], acc)
            out_ref[0, start:stop, :] = acc


def _far_runs(dist, masked, rows_per_run=8):
    far_rows = np.all(masked | (dist >= REL_MAX_DIST), axis=1)
    groups = far_rows.reshape(-1, rows_per_run).all(axis=1)
    runs, start = [], 0
    for g in range(1, len(groups) + 1):
        if g == len(groups) or groups[g] != groups[start]:
            runs.append((start * rows_per_run, g * rows_per_run, bool(groups[start])))
            start = g
    return tuple(runs)


def _bias_tables(rel_bias, ncp, nq):
    tab = rel_bias.T.astype(F32) * LOG2E
    n_heads = tab.shape[0]
    c = jnp.arange(TQ)[:, None]
    r = jnp.arange(TQ)[None, :]
    d_own = r - c
    bk_own = jnp.where(d_own >= 0, _rel_bucket(d_own), -1)
    bk_prev = _rel_bucket(d_own + TQ)
    bk_win2 = jnp.where(d_own + 2 * TQ < WINDOW, _rel_bucket(d_own + 2 * TQ), -1)
    band_rows = ncp + (nq - 1) * (TQ // CMP_STRIDE)
    u = jnp.arange(band_rows)[:, None] - (nq - 1) * (TQ // CMP_STRIDE)
    d_cmp = r - CMP_STRIDE * u - (CMP_LEN - 1)
    bk_cmp = jnp.where(d_cmp >= 0, _rel_bucket(d_cmp), -1)
    tables = [t.astype(jnp.int32) for t in (jnp.concatenate([bk_win2, bk_prev, bk_own], axis=0), bk_cmp)]
    cn, rn = np.arange(TQ)[:, None], np.arange(TQ)[None, :]
    d_win = np.concatenate([rn - cn + 2 * TQ, rn - cn + TQ, rn - cn], axis=0)
    m_win = np.concatenate([rn - cn + 2 * TQ >= WINDOW, np.zeros((TQ, TQ), bool), rn - cn < 0], axis=0)
    d_cmp_n = rn - CMP_STRIDE * (np.arange(band_rows)[:, None] - (nq - 1) * (TQ // CMP_STRIDE)) - (CMP_LEN - 1)
    runs = (_far_runs(d_win, m_win), _far_runs(d_cmp_n, d_cmp_n < 0))
    outs = pl.pallas_call(
        functools.partial(_bias_body, runs=runs),
        grid=(n_heads,),
        in_specs=[pl.BlockSpec(memory_space=pltpu.SMEM)]
        + [pl.BlockSpec(t.shape, lambda h: (0, 0)) for t in tables],
        out_specs=[pl.BlockSpec((1, t.shape[0], TQ), lambda h: (h // HG, 0, h % HG)) for t in tables],
        out_shape=[jax.ShapeDtypeStruct((n_heads // HG, t.shape[0], WIDE), F32) for t in tables],
        compiler_params=pltpu.CompilerParams(dimension_semantics=("parallel",), vmem_limit_bytes=VMEM_LIMIT),
        name="bias_tiles",
    )(tab, *tables)
    return outs, tab[:, REL_BUCKETS - 1]


def _slc_overlap_t(ncp, nslc):
    ratio = SLC_BLOCK // CMP_STRIDE
    mc = CMP_LEN // CMP_STRIDE
    w = np.zeros((HEAD_PAD, nslc * ratio + mc), np.float32)
    for j in range(nslc):
        for a in range(ratio):
            for m in range(mc):
                w[j, ratio * j + a + m] += 1.0
    n_cmp = ncp - 1
    out = np.zeros((HEAD_PAD, ncp), np.float32)
    out[:, :n_cmp] = w[:, :n_cmp]
    return out


def _block_onehot(s, block):
    pat = np.zeros((s, HEAD_PAD), np.float32)
    pat[np.arange(s), EXT0 + np.arange(s) // block] = 1.0
    return pat


def kernel(x, norm_ffn1, w_ffn1_gate, w_ffn1_up, w_ffn1_down, norm_mix, w_in, cmp_pos_k, cmp_w1_k, cmp_w2_k,
           cmp_pos_v, cmp_w1_v, cmp_w2_v, w_out, norm_ffn2, w_ffn2_gate, w_ffn2_up, w_ffn2_down, rel_bias,
           norm_final):
    b, s, d = x.shape
    assert d == D_MODEL and s % (2 * TQ) == 0 and norm_ffn1.shape[0] == 1
    assert s // SLC_BLOCK <= HEAD_PAD - EXT0, "selection-mask rows must fit beside the 64 head dims"
    n = b * s
    nq = s // TQ
    ncp = s // CMP_STRIDE
    nslc = s // SLC_BLOCK
    scale = HEAD_DIM ** -0.5 * LOG2E

    sizes = (512, 512, 512, 512) + (128,) * 6 + (24,)
    splits = np.cumsum(sizes)[:-1].tolist()
    wqa, wka, wva, wqb, wkc, wvc, wks, wvs, wkw, wvw, wgb = jnp.split(w_in[0], splits, axis=1)
    w_k = jnp.concatenate([wka, wkc, wvc, wks, wkw], axis=1).astype(BF16)
    w_t = jnp.concatenate([wqa * scale, wva, wqb * scale, wvs, wvw], axis=1).T.astype(BF16)
    assert w_k.shape[1] == N_KSLOTS * HEAD_PAD and w_t.shape[0] == N_TROWS * HEAD_DIM
    wgt = wgb.reshape(D_MODEL, NSA_GROUPS, NSA_REP * N_GATES)
    wgt = jnp.pad(wgt, ((0, 0), (0, 0), (0, GATE_ROWS - NSA_REP * N_GATES)))
    wgt = wgt.reshape(D_MODEL, NSA_GROUPS * GATE_ROWS).T.astype(BF16)

    def cmp_params(pos, w1, w2):
        w1_h = w1[0].reshape(CMP_LEN, HEAD_DIM, CMP_HIDDEN)
        pos_g, w1_g = [], []
        for grp in range(NSA_GROUPS):
            pad = ((0, 0), (grp * HEAD_DIM, HEAD_PAD - (grp + 1) * HEAD_DIM))
            pos_g.append(jnp.pad(pos[0], pad))
            w1_g.append(jnp.pad(w1_h, pad + ((0, 0),)))
        w2_p = jnp.pad(w2[0], ((0, 0), (0, HEAD_PAD - HEAD_DIM)))
        return jnp.stack(pos_g), jnp.stack(w1_g).astype(BF16), w2_p.astype(BF16)

    pos_k, w1k, w2k = cmp_params(cmp_pos_k, cmp_w1_k, cmp_w2_k)
    pos_v, w1v, w2v = cmp_params(cmp_pos_v, cmp_w1_v, cmp_w2_v)
    cmp_pos = jnp.stack([pos_k, pos_v])
    cmp_w1 = jnp.stack([w1k, w1v])

    wo = w_out[0]
    wo_a = wo[:MOBA_HEADS * HEAD_DIM].astype(BF16)
    wo_b = wo[MOBA_HEADS * HEAD_DIM:].astype(BF16)

    (bias_win, bias_cmp), c31 = _bias_tables(rel_bias, ncp, nq)
    ovt = jnp.asarray(_slc_overlap_t(ncp, nslc)).astype(BF16)
    pat_moba = jnp.asarray(_block_onehot(s, MOBA_BLOCK)).astype(BF16)
    pat_slc = jnp.asarray(_block_onehot(s, SLC_BLOCK)).astype(BF16)
    far = [(i, j) for j in range(nq) for i in range(j + 2, nq)]
    far_i = jnp.asarray(np.array([i for i, _ in far], np.int32))
    far_j = jnp.asarray(np.array([j for _, j in far], np.int32))

    x2d = x.reshape(n, D_MODEL)
    nf = norm_final.reshape(1, D_MODEL)
    x1 = _ffn(x2d, norm_ffn1, w_ffn1_gate[0].astype(BF16), w_ffn1_up[0].astype(BF16),
              w_ffn1_down[0].astype(BF16), nf, final_norm=False)
    pk, pt, gates_t = _inproj(x1, norm_mix, w_k, w_t, wgt, b, s)
    pk = pk.reshape(b, s, N_KSLOTS * HEAD_PAD)
    kc, vct = _compress(pk, cmp_pos, cmp_w1, w2k, w2v.T)
    o_a = _moba(pk, pt, pat_moba, bias_win, c31, far_i, far_j)
    o_b = _nsa(pk, pt, kc, vct, gates_t, pat_slc, bias_win, bias_cmp, ovt, c31, far_i, far_j)
    y = _ffn(x1, norm_ffn2, w_ffn2_gate[0].astype(BF16), w_ffn2_up[0].astype(BF16),
             w_ffn2_down[0].astype(BF16), nf, final_norm=True, mixer=(o_a, o_b, wo_a, wo_b))
    return y.reshape(b, s, D_MODEL)
```

```python
import functools
import math

import numpy as np
import jax
import jax.numpy as jnp
from jax import lax
from jax.experimental import pallas as pl
from jax.experimental.pallas import tpu as pltpu

D_MODEL = 1024
D_FF = 2816
HEAD_DIM = 64
HEAD_PAD = 128
ONES_ROW = HEAD_DIM
EXT0 = HEAD_DIM
MOBA_HEADS = 8
NSA_HEADS = 8
NSA_GROUPS = 2
NSA_REP = NSA_HEADS // NSA_GROUPS
HG = 4
MOBA_BLOCK = 256
MOBA_TOPK = 3
CMP_LEN = 32
CMP_STRIDE = 16
CMP_HIDDEN = 256
SLC_BLOCK = 64
SLC_TOPN = 16
WINDOW = 512
N_GATES = 3
GATE_ROWS = 16
REL_BUCKETS = 32
REL_MAX_EXACT = REL_BUCKETS // 2
REL_MAX_DIST = 128
RMS_EPS = 1e-6
TQ = 256
WIDE = HG * TQ
NEG = -1e30
LOG2E = 1.4426950408889634
ACC_ROWS = 80
VMEM_LIMIT = 56 * 1024 * 1024

KSLOT_KA, KSLOT_KC, KSLOT_VC, KSLOT_KS, KSLOT_KW = 0, 4, 5, 6, 7
N_KSLOTS = 8
TROW_QA, TROW_VA, TROW_QB, TROW_VS, TROW_VW = 0, 8, 16, 24, 26
N_TROWS = 28

F32 = jnp.float32
BF16 = jnp.bfloat16


def _dot(a, b):
    return jnp.dot(a, b, preferred_element_type=F32)


def _dot_nt(a, b):
    return lax.dot_general(a, b, (((1,), (1,)), ((), ())), preferred_element_type=F32)


def _split_bf16(x):
    hi = x.astype(BF16)
    lo = (x - hi.astype(F32)).astype(BF16)
    return hi, lo


def _rmsnorm(x, g):
    ms = jnp.mean(x * x, axis=-1, keepdims=True)
    return x * lax.rsqrt(ms + RMS_EPS) * g


def _dot_tn(a_t, b):
    return lax.dot_general(a_t, b, (((0,), (0,)), ((), ())), preferred_element_type=F32)


def _ffn_body(x_ref, g_ref, wg_ref, wu_ref, wd_ref, gf_ref, *rest, final_norm, n_chunks, with_mixer):
    x = x_ref[...]
    if with_mixer:
        oa_ref, ob_ref, woa_ref, wob_ref, o_ref = rest
        x = x + _dot_tn(oa_ref[0], woa_ref[...]) + _dot_tn(ob_ref[0], wob_ref[...])
    else:
        (o_ref,) = rest
    h = _rmsnorm(x, g_ref[...]).astype(BF16)
    tf = D_FF // n_chunks
    acc = None
    for c in range(n_chunks):
        a = _dot(h, wg_ref[:, c * tf:(c + 1) * tf])
        u = _dot(h, wu_ref[:, c * tf:(c + 1) * tf])
        z = ((a * jax.nn.sigmoid(a)) * u).astype(BF16)
        part = _dot(z, wd_ref[c * tf:(c + 1) * tf, :])
        acc = part if acc is None else acc + part
    y = x + 0.5 * acc
    if final_norm:
        y = _rmsnorm(y, gf_ref[...])
    o_ref[...] = y


def _ffn(x2d, g, wg, wu, wd, gf, *, final_norm, mixer=None, tm=512, n_chunks=1):
    n = x2d.shape[0]
    resident = lambda shape: pl.BlockSpec(shape, lambda i: (0, 0), pipeline_mode=pl.Buffered(1))
    in_specs = [
        pl.BlockSpec((tm, D_MODEL), lambda i: (i, 0)),
        resident((1, D_MODEL)),
        resident((D_MODEL, D_FF)), resident((D_MODEL, D_FF)), resident((D_FF, D_MODEL)),
        resident((1, D_MODEL)),
    ]
    args = [x2d, g, wg, wu, wd, gf]
    if mixer is not None:
        o_a, o_b, wo_a, wo_b = mixer
        spt = o_a.shape[2] // tm
        for o_t in (o_a, o_b):
            in_specs.append(pl.BlockSpec((1, o_t.shape[1], tm), lambda i: (i // spt, 0, i % spt)))
        in_specs += [resident(wo_a.shape), resident(wo_b.shape)]
        args += [o_a, o_b, wo_a, wo_b]
    return pl.pallas_call(
        functools.partial(_ffn_body, final_norm=final_norm, n_chunks=n_chunks, with_mixer=mixer is not None),
        grid=(n // tm,),
        in_specs=in_specs,
        out_specs=pl.BlockSpec((tm, D_MODEL), lambda i: (i, 0)),
        out_shape=jax.ShapeDtypeStruct((n, D_MODEL), F32),
        compiler_params=pltpu.CompilerParams(
            dimension_semantics=("parallel",), vmem_limit_bytes=VMEM_LIMIT),
        name="ffn_final" if final_norm else "ffn",
    )(*args)


def _inproj_body(x_ref, g_ref, wk_ref, wt_ref, wgt_ref, pk_ref, pt_ref, gates_ref):
    h = _rmsnorm(x_ref[...], g_ref[...]).astype(BF16)
    pk_ref[...] = _dot(h, wk_ref[...]).astype(BF16)
    pt_ref[0] = _dot_nt(wt_ref[...], h).astype(BF16)
    gates_ref[0] = jax.nn.sigmoid(_dot_nt(wgt_ref[...], h))


def _inproj(x2d, g, wk, wt, wgt, b, s, *, tm=512):
    n = x2d.shape[0]
    nk = wk.shape[1]
    nt = wt.shape[0]
    ng = wgt.shape[0]
    spt = s // tm
    const = lambda shape: pl.BlockSpec(shape, lambda i: (0, 0))
    return pl.pallas_call(
        _inproj_body,
        grid=(n // tm,),
        in_specs=[
            pl.BlockSpec((tm, D_MODEL), lambda i: (i, 0)),
            const((1, D_MODEL)), const((D_MODEL, nk)), const((nt, D_MODEL)), const((ng, D_MODEL)),
        ],
        out_specs=[
            pl.BlockSpec((tm, nk), lambda i: (i, 0)),
            pl.BlockSpec((1, nt, tm), lambda i: (i // spt, 0, i % spt)),
            pl.BlockSpec((1, ng, tm), lambda i: (i // spt, 0, i % spt)),
        ],
        out_shape=[jax.ShapeDtypeStruct((n, nk), BF16), jax.ShapeDtypeStruct((b, nt, s), BF16),
                   jax.ShapeDtypeStruct((b, ng, s), F32)],
        compiler_params=pltpu.CompilerParams(
            dimension_semantics=("parallel",), vmem_limit_bytes=VMEM_LIMIT),
        name="inproj",
    )(x2d, g, wk, wt, wgt)


def _compress_body(kraw_ref, vraw_ref, pos_ref, w1_ref, w2k_ref, w2vt_ref, kc_ref, vct_ref, xs_ref, *, ncp):
    half = CMP_LEN // 2
    for kv, raw_ref in enumerate((kraw_ref, vraw_ref)):
        xs_ref[...] = raw_ref[0].astype(F32)
        y0 = jnp.zeros((ncp, CMP_HIDDEN), F32)
        y1 = jnp.zeros((ncp, CMP_HIDDEN), F32)
        for p in range(half):
            xp = xs_ref[pl.ds(p, ncp, stride=CMP_STRIDE), :]
            y0 = y0 + _dot((xp + pos_ref[kv, 0, p:p + 1, :]).astype(BF16), w1_ref[kv, 0, p])
            y1 = y1 + _dot((xp + pos_ref[kv, 0, half + p:half + p + 1, :]).astype(BF16), w1_ref[kv, 0, half + p])
        h1 = y0 + pltpu.roll(y1, ncp - 1, 0)
        act = (h1 * jax.nn.sigmoid(h1)).astype(BF16)
        if kv == 0:
            kc_ref[0, 0] = _dot(act, w2k_ref[...]).astype(BF16)
        else:
            out_t = _dot_nt(w2vt_ref[...], act)
            row = lax.broadcasted_iota(jnp.int32, out_t.shape, 0)
            vct_ref[0, 0] = (out_t + jnp.where(row == ONES_ROW, 1.0, 0.0)).astype(BF16)


def _compress(pk, pos, w1, w2k, w2vt):
    b, s, _ = pk.shape
    ncp = s // CMP_STRIDE
    full = lambda shape: pl.BlockSpec(shape, lambda bi, g: (0,) * len(shape))
    return pl.pallas_call(
        functools.partial(_compress_body, ncp=ncp),
        grid=(b, NSA_GROUPS),
        in_specs=[
            pl.BlockSpec((1, s, HEAD_PAD), lambda bi, g: (bi, 0, KSLOT_KC)),
            pl.BlockSpec((1, s, HEAD_PAD), lambda bi, g: (bi, 0, KSLOT_VC)),
            pl.BlockSpec((2, 1, CMP_LEN, HEAD_PAD), lambda bi, g: (0, g, 0, 0)),
            pl.BlockSpec((2, 1, CMP_LEN, HEAD_PAD, CMP_HIDDEN), lambda bi, g: (0, g, 0, 0, 0)),
            full((CMP_HIDDEN, HEAD_PAD)),
            full((HEAD_PAD, CMP_HIDDEN)),
        ],
        out_specs=[
            pl.BlockSpec((1, 1, ncp, HEAD_PAD), lambda bi, g: (bi, g, 0, 0)),
            pl.BlockSpec((1, 1, HEAD_PAD, ncp), lambda bi, g: (bi, g, 0, 0)),
        ],
        out_shape=[
            jax.ShapeDtypeStruct((b, NSA_GROUPS, ncp, HEAD_PAD), BF16),
            jax.ShapeDtypeStruct((b, NSA_GROUPS, HEAD_PAD, ncp), BF16),
        ],
        scratch_shapes=[pltpu.VMEM((s, HEAD_PAD), F32)],
        compiler_params=pltpu.CompilerParams(
            dimension_semantics=("parallel", "arbitrary"), vmem_limit_bytes=VMEM_LIMIT),
        name="compress",
    )(pk, pk, pos, w1, w2k, w2vt)


def _flash_update(s_t, shift, pv_fn, m_ref, acc_ref):
    m_prev = m_ref[...]
    m_new = jnp.maximum(m_prev, jnp.max(s_t, axis=0, keepdims=True) + shift)
    alpha = jnp.exp2(m_prev - m_new)
    p = jnp.exp2(s_t - (m_new - shift)).astype(BF16)
    acc_ref[...] = alpha * acc_ref[...] + pv_fn(p)
    m_ref[...] = m_new


def _state_lanes(i):
    return pl.ds(pl.multiple_of(i * WIDE, WIDE), WIDE)


def _tile_phase(count, tile_fn, scores_fn, step_fn, sa_ref, sb_ref):
    if count == 0:
        return
    tile = lambda n: tile_fn(jnp.minimum(jnp.asarray(n, jnp.int32), count - 1))
    sa_ref[...] = scores_fn(*tile(0))

    def pair(p, carry):
        n = 2 * p
        sb_ref[...] = scores_fn(*tile(n + 1))
        step_fn(sa_ref, *tile(n))
        sa_ref[...] = scores_fn(*tile(n + 2))
        step_fn(sb_ref, *tile(n + 1))
        return carry

    lax.fori_loop(0, count // 2, pair, 0, unroll=4)
    if count % 2:
        step_fn(sa_ref, *tile(count - 1))


def _flat_stream(nq, n_far, far_i_ref, far_j_ref, scores_fn, pv_fn, bias_ref, shift_far,
                 sa_ref, sb_ref, m_ref, acc_ref):
    zero = jnp.zeros((1, WIDE), F32)

    def own_step(s_ref, i, j):
        s_t = s_ref[...] + bias_ref[0, 2 * TQ:3 * TQ, :]
        m = jnp.max(s_t, axis=0, keepdims=True)
        m_ref[:, _state_lanes(i)] = m
        acc_ref[:, _state_lanes(i)] = pv_fn(j)(jnp.exp2(s_t - m).astype(BF16))

    def prev_step(s_ref, i, j):
        _flash_update(s_ref[...] + bias_ref[0, TQ:2 * TQ, :], zero, pv_fn(j),
                      m_ref.at[:, _state_lanes(i)], acc_ref.at[:, _state_lanes(i)])

    def far_step(s_ref, i, j):
        _flash_update(s_ref[...], shift_far, pv_fn(j), m_ref.at[:, _state_lanes(i)], acc_ref.at[:, _state_lanes(i)])

    _tile_phase(nq, lambda n: (n, n), scores_fn, own_step, sa_ref, sb_ref)
    _tile_phase(nq - 1, lambda n: (n + 1, n), scores_fn, prev_step, sa_ref, sb_ref)
    _tile_phase(n_far, lambda n: (far_i_ref[n], far_j_ref[n]), scores_fn, far_step, sa_ref, sb_ref)


def _lane_block(x, r):
    return x[:, r * TQ:(r + 1) * TQ]


def _shift_row(c31_ref, head0):
    return jnp.concatenate([jnp.full((1, TQ), c31_ref[head0 + r], F32) for r in range(HG)], axis=1)


def _rank_select(score, n_rows, top, vote=None):
    sub = 8
    n_groups = score.shape[0] // sub
    groups = [score[g * sub:(g + 1) * sub] for g in range(n_groups)]
    counts = [jnp.zeros(groups[0].shape, F32) for _ in range(n_groups)]
    local = lax.broadcasted_iota(jnp.int32, groups[0].shape, 0)
    for j in range(n_rows):
        row = score[j:j + 1, :]
        one = 1.0 if vote is None else vote(j)
        for g in range(n_groups):
            if j < g * sub:
                ahead = jnp.where(row >= groups[g], one, 0.0)
            elif j >= (g + 1) * sub:
                ahead = jnp.where(row > groups[g], one, 0.0)
            else:
                ahead = jnp.where(local > j - g * sub, jnp.where(row >= groups[g], one, 0.0),
                                  jnp.where(row > groups[g], one, 0.0))
            counts[g] = counts[g] + ahead
    return jnp.concatenate(counts, axis=0) < top


def _key_operand(slot, half, pat):
    if not (isinstance(half, int) and half == 0):
        slot = pltpu.roll(slot.astype(F32), HEAD_DIM * half, 1).astype(BF16)
    lane = lax.broadcasted_iota(jnp.int32, slot.shape, 1)
    return jnp.where(lane < HEAD_DIM, slot, pat)


def _value_operand(v_t):
    rows = lax.broadcasted_iota(jnp.int32, (ACC_ROWS - HEAD_DIM, v_t.shape[1]), 0)
    return jnp.concatenate([v_t, jnp.where(rows == 0, 1.0, 0.0).astype(BF16)], axis=0)


def _moba_body(c31_ref, far_i_ref, far_j_ref, q_ref, k_ref, vt_ref, pat_ref, bias_ref, o_ref,
               kmean_ref, kx_ref, vx_ref, qx_ref, sa_ref, sb_ref, m_ref, acc_ref, *, nb, n_far):
    hg = pl.program_id(1)
    s = q_ref.shape[2]
    nbp = -(-nb // 16) * 16

    kmean_ref[...] = jnp.zeros_like(kmean_ref)
    blk = lax.broadcasted_iota(jnp.int32, (nbp, s), 0)
    cur = lax.broadcasted_iota(jnp.int32, (nbp, s), 1) // MOBA_BLOCK
    zero_rows = jnp.zeros((HEAD_PAD - HEAD_DIM, s), BF16)
    for r in range(HG):
        kx = _key_operand(k_ref[0, :, (r // 2) * HEAD_PAD:(r // 2 + 1) * HEAD_PAD], r % 2, pat_ref[...])
        kx_ref[:, r * HEAD_PAD:(r + 1) * HEAD_PAD] = kx
        vx_ref[r] = _value_operand(vt_ref[0, r * HEAD_DIM:(r + 1) * HEAD_DIM, :])
        for n in range(nb):
            kb = kx[n * MOBA_BLOCK:(n + 1) * MOBA_BLOCK, :].astype(F32)
            kmean_ref[r, n:n + 1, :] = jnp.mean(kb, axis=0, keepdims=True)
        q_t = q_ref[0, r * HEAD_DIM:(r + 1) * HEAD_DIM, :]
        q_pad = jnp.concatenate([q_t, zero_rows], axis=0)
        kh, kl = _split_bf16(kmean_ref[r, :nbp, :])
        gate = _dot(kh, q_pad) + _dot(kl, q_pad)
        top = _rank_select(gate, nb, MOBA_TOPK, lambda n: jnp.where(n < cur[:8], 1.0, 0.0))
        sel = ((blk < cur) & top) | (blk == cur)
        mask_rows = jnp.where(sel, 0.0, NEG).astype(BF16)
        pad_rows = jnp.zeros((HEAD_PAD - EXT0 - nbp, s), BF16)
        qx_ref[r] = jnp.concatenate([q_t, mask_rows, pad_rows], axis=0)

    def scores(i, j):
        qoff = pl.multiple_of(i * TQ, TQ)
        koff = pl.multiple_of(j * TQ, TQ)
        return jnp.concatenate(
            [_dot(kx_ref[pl.ds(koff, TQ), r * HEAD_PAD:(r + 1) * HEAD_PAD], qx_ref[r, :, pl.ds(qoff, TQ)])
             for r in range(HG)], axis=1)

    def pv_fn(j):
        off = pl.multiple_of(j * TQ, TQ)

        def fn(p):
            return jnp.concatenate(
                [_dot(vx_ref[r, :, pl.ds(off, TQ)], _lane_block(p, r)) for r in range(HG)], axis=1)
        return fn

    _flat_stream(nb, n_far, far_i_ref, far_j_ref, scores, pv_fn, bias_ref, _shift_row(c31_ref, hg * HG),
                 sa_ref, sb_ref, m_ref, acc_ref)

    def finish(i, carry):
        acc = acc_ref[:, _state_lanes(i)]
        out = (acc[:HEAD_DIM] / acc[ONES_ROW:ONES_ROW + 1, :]).astype(BF16)
        for r in range(HG):
            o_ref[0, r * HEAD_DIM:(r + 1) * HEAD_DIM, pl.ds(pl.multiple_of(i * TQ, TQ), TQ)] = _lane_block(out, r)
        return carry

    lax.fori_loop(0, nb, finish, 0)


def _moba(pk, pt, pat, bias_win, c31, far_i, far_j):
    b, s, _ = pk.shape
    nb = s // MOBA_BLOCK
    hrows = HG * HEAD_DIM
    smem = pl.BlockSpec(memory_space=pltpu.SMEM)
    return pl.pallas_call(
        functools.partial(_moba_body, nb=nb, n_far=far_i.shape[0]),
        grid=(b, MOBA_HEADS // HG),
        in_specs=[
            smem, smem, smem,
            pl.BlockSpec((1, hrows, s), lambda bi, hg: (bi, TROW_QA // HG + hg, 0)),
            pl.BlockSpec((1, s, hrows), lambda bi, hg: (bi, 0, KSLOT_KA * HEAD_PAD // hrows + hg)),
            pl.BlockSpec((1, hrows, s), lambda bi, hg: (bi, TROW_VA // HG + hg, 0)),
            pl.BlockSpec((s, HEAD_PAD), lambda bi, hg: (0, 0), pipeline_mode=pl.Buffered(1)),
            pl.BlockSpec((1, 3 * TQ, WIDE), lambda bi, hg: (hg, 0, 0)),
        ],
        out_specs=pl.BlockSpec((1, HG * HEAD_DIM, s), lambda bi, hg: (bi, hg, 0)),
        out_shape=jax.ShapeDtypeStruct((b, MOBA_HEADS * HEAD_DIM, s), BF16),
        scratch_shapes=[
            pltpu.VMEM((HG, HEAD_PAD, HEAD_PAD), F32),
            pltpu.VMEM((s, HG * HEAD_PAD), BF16),
            pltpu.VMEM((HG, ACC_ROWS, s), BF16),
            pltpu.VMEM((HG, HEAD_PAD, s), BF16),
            pltpu.VMEM((TQ, WIDE), F32),
            pltpu.VMEM((TQ, WIDE), F32),
            pltpu.VMEM((1, nb * WIDE), F32),
            pltpu.VMEM((ACC_ROWS, nb * WIDE), F32),
        ],
        compiler_params=pltpu.CompilerParams(
            dimension_semantics=("parallel", "arbitrary"), vmem_limit_bytes=VMEM_LIMIT),
        name="moba",
    )(c31, far_i, far_j, pt, pk, pt, pat, bias_win)


def _nsa_body(c31_ref, far_i_ref, far_j_ref, q_ref, ks_ref, kw_ref, vst_ref, vwt_ref, kc_ref, vct_ref,
              gates_ref, pat_ref, bias_ref, bcmp_ref, ovt_ref, o_ref,
              ksx_ref, kwx_ref, vsx_ref, vwx_ref, qx_ref, sa_ref, sb_ref, m_ref, accs_ref, part_ref,
              *, ncp, nslc, nq, n_far):
    g = pl.program_id(1)

    ksx_ref[...] = _key_operand(ks_ref[0], g, pat_ref[...])
    kwx_ref[...] = _key_operand(kw_ref[0], g, jnp.zeros((), BF16))
    grows = pl.ds(pl.multiple_of(g * HEAD_DIM, HEAD_DIM), HEAD_DIM)
    vsx_ref[...] = _value_operand(vst_ref[0, grows, :])
    vwx_ref[...] = _value_operand(vwt_ref[0, grows, :])
    zero_rows = jnp.zeros((HEAD_PAD - HEAD_DIM, WIDE), BF16)

    def front(i, n_win, n_act):
        qoff = pl.multiple_of(i * TQ, TQ)
        q4c = jnp.concatenate([q_ref[0, r * HEAD_DIM:(r + 1) * HEAD_DIM, pl.ds(qoff, TQ)] for r in range(HG)],
                              axis=1)
        q4 = jnp.concatenate([q4c, zero_rows], axis=0)

        band_off = pl.multiple_of((nq - 1 - i) * (TQ // CMP_STRIDE), TQ // CMP_STRIDE)
        s_t = _dot(kc_ref[0, 0], q4) + bcmp_ref[0, pl.ds(band_off, ncp), :]
        m = jnp.max(s_t, axis=0, keepdims=True)
        p = jnp.exp2(s_t - m)
        den = jnp.sum(p, axis=0, keepdims=True)
        pn = p * jnp.where(m > 0.5 * NEG, 1.0 / den, 0.0)
        o_cmp = _dot(vct_ref[0, 0, :ACC_ROWS, :], pn.astype(BF16))
        psum = _lane_block(pn, 0)
        for r in range(1, HG):
            psum = psum + _lane_block(pn, r)

        ph, plo = _split_bf16(psum)
        imp = (_dot(ovt_ref[...], ph) + _dot(ovt_ref[...], plo))[:n_act]
        blk = lax.broadcasted_iota(jnp.int32, imp.shape, 0)
        cur = (i * TQ + lax.broadcasted_iota(jnp.int32, imp.shape, 1)) // SLC_BLOCK
        forced = (blk == 0) | (blk == cur) | (blk == cur - 1)
        score = jnp.where(forced, 1e30, jnp.where(blk <= cur, imp, -1.0))
        sel = _rank_select(score, n_act, SLC_TOPN)
        mask_rows = jnp.where(sel, 0.0, NEG)
        if n_act < EXT0:
            mask_rows = jnp.concatenate([mask_rows, jnp.full((EXT0 - n_act, TQ), NEG, F32)], axis=0)
        mask_rows = mask_rows.astype(BF16)
        qx_ref[:, _state_lanes(i)] = jnp.concatenate(
            [q4c, jnp.concatenate([mask_rows] * HG, axis=1)], axis=0)

        start = pl.multiple_of((i + 1 - n_win) * TQ, TQ)
        s_w = _dot(kwx_ref[pl.ds(start, n_win * TQ), :], q4) + bias_ref[0, (3 - n_win) * TQ:, :]
        p_w = jnp.exp2(s_w - jnp.max(s_w, axis=0, keepdims=True)).astype(BF16)
        acc_w = _dot(vwx_ref[:, pl.ds(start, n_win * TQ)], p_w)
        o_win = acc_w / acc_w[ONES_ROW:ONES_ROW + 1, :]

        gt = gates_ref[0, :, pl.ds(qoff, TQ)]
        part_ref[:, _state_lanes(i)] = jnp.concatenate(
            [gt[N_GATES * r:N_GATES * r + 1, :] * _lane_block(o_cmp, r)
             + gt[N_GATES * r + 2:N_GATES * r + 3, :] * _lane_block(o_win, r) for r in range(HG)], axis=1)

    blocks_upto = lambda i_end: min(-(-nslc // 8) * 8, -(-i_end * TQ // SLC_BLOCK // 8) * 8)
    for i0 in range(min(2, nq)):
        front(jnp.asarray(i0, jnp.int32), i0 + 1, blocks_upto(i0 + 1))

    mid = max(2, nq // 2)
    for lo, hi in ((2, mid), (mid, nq)):
        def front_loop(i, carry, n_act=blocks_upto(hi)):
            front(i, 3, n_act)
            return carry

        if hi > lo:
            lax.fori_loop(lo, hi, front_loop, 0, unroll=2)

    def scores(i, j):
        return _dot(ksx_ref[pl.ds(pl.multiple_of(j * TQ, TQ), TQ), :], qx_ref[:, _state_lanes(i)])

    def pv_fn(j):
        off = pl.multiple_of(j * TQ, TQ)
        return lambda p: _dot(vsx_ref[:, pl.ds(off, TQ)], p)

    _flat_stream(nq, n_far, far_i_ref, far_j_ref, scores, pv_fn, bias_ref,
                 _shift_row(c31_ref, MOBA_HEADS + g * HG), sa_ref, sb_ref, m_ref, accs_ref)

    def finish(i, carry):
        qoff = pl.multiple_of(i * TQ, TQ)
        acc = accs_ref[:, _state_lanes(i)]
        o_slc = acc / acc[ONES_ROW:ONES_ROW + 1, :]
        part = part_ref[:, _state_lanes(i)]
        gt = gates_ref[0, :, pl.ds(qoff, TQ)]
        for r in range(HG):
            out = _lane_block(part, r) + gt[N_GATES * r + 1:N_GATES * r + 2, :] * _lane_block(o_slc, r)
            o_ref[0, r * HEAD_DIM:(r + 1) * HEAD_DIM, pl.ds(qoff, TQ)] = out[:HEAD_DIM].astype(BF16)
        return carry

    lax.fori_loop(0, nq, finish, 0)


def _nsa(pk, pt, kc, vct, gates_t, pat, bias_win, bias_cmp, ovt, c31, far_i, far_j):
    b, s, _ = pk.shape
    nq = s // TQ
    ncp = s // CMP_STRIDE
    nslc = s // SLC_BLOCK
    hrows = HG * HEAD_DIM
    band_rows = bias_cmp.shape[1]
    hgrp = MOBA_HEADS // HG
    smem = pl.BlockSpec(memory_space=pltpu.SMEM)
    k_spec = lambda slot: pl.BlockSpec((1, s, HEAD_PAD), lambda bi, g: (bi, 0, slot))
    vt_spec = lambda row: pl.BlockSpec((1, HEAD_PAD, s), lambda bi, g: (bi, row * HEAD_DIM // HEAD_PAD, 0))
    bias_spec = lambda rows: pl.BlockSpec((1, rows, WIDE), lambda bi, g: (hgrp + g, 0, 0),
                                          pipeline_mode=pl.Buffered(1))
    return pl.pallas_call(
        functools.partial(_nsa_body, ncp=ncp, nslc=nslc, nq=nq, n_far=far_i.shape[0]),
        grid=(b, NSA_GROUPS),
        in_specs=[
            smem, smem, smem,
            pl.BlockSpec((1, hrows, s), lambda bi, g: (bi, TROW_QB // HG + g, 0)),
            k_spec(KSLOT_KS), k_spec(KSLOT_KW), vt_spec(TROW_VS), vt_spec(TROW_VW),
            pl.BlockSpec((1, 1, ncp, HEAD_PAD), lambda bi, g: (bi, g, 0, 0)),
            pl.BlockSpec((1, 1, HEAD_PAD, ncp), lambda bi, g: (bi, g, 0, 0)),
            pl.BlockSpec((1, GATE_ROWS, s), lambda bi, g: (bi, g, 0)),
            pl.BlockSpec((s, HEAD_PAD), lambda bi, g: (0, 0), pipeline_mode=pl.Buffered(1)),
            bias_spec(3 * TQ), bias_spec(band_rows),
            pl.BlockSpec((HEAD_PAD, ncp), lambda bi, g: (0, 0)),
        ],
        out_specs=pl.BlockSpec((1, HG * HEAD_DIM, s), lambda bi, g: (bi, g, 0)),
        out_shape=jax.ShapeDtypeStruct((b, NSA_HEADS * HEAD_DIM, s), BF16),
        scratch_shapes=[
            pltpu.VMEM((s, HEAD_PAD), BF16),
            pltpu.VMEM((s, HEAD_PAD), BF16),
            pltpu.VMEM((ACC_ROWS, s), BF16),
            pltpu.VMEM((ACC_ROWS, s), BF16),
            pltpu.VMEM((HEAD_PAD, nq * WIDE), BF16),
            pltpu.VMEM((TQ, WIDE), F32),
            pltpu.VMEM((TQ, WIDE), F32),
            pltpu.VMEM((1, nq * WIDE), F32),
            pltpu.VMEM((ACC_ROWS, nq * WIDE), F32),
            pltpu.VMEM((ACC_ROWS, nq * WIDE), F32),
        ],
        compiler_params=pltpu.CompilerParams(
            dimension_semantics=("parallel", "arbitrary"), vmem_limit_bytes=VMEM_LIMIT),
        name="nsa",
    )(c31, far_i, far_j, pt, pk, pk, pt, pt, kc, vct, gates_t, pat, bias_win, bias_cmp, ovt)


def _rel_bucket(dist):
    n = jnp.maximum(dist, 0)
    nf = jnp.maximum(n, 1).astype(F32)
    large = REL_MAX_EXACT + (jnp.log(nf / REL_MAX_EXACT) / math.log(REL_MAX_DIST / REL_MAX_EXACT)
                             * (REL_BUCKETS - REL_MAX_EXACT)).astype(jnp.int32)
    return jnp.where(n < REL_MAX_EXACT, n, jnp.minimum(large, REL_BUCKETS - 1))


def _bias_body(tab_ref, *refs, runs):
    h = pl.program_id(0)
    n_tab = len(refs) // 2
    for bk_ref, out_ref, table_runs in zip(refs[:n_tab], refs[n_tab:], runs):
        for start, stop, far in table_runs:
            bk = bk_ref[start:stop, :]
            if far:
                acc = jnp.where(bk >= 0, tab_ref[h, REL_BUCKETS - 1], NEG)
            else:
                acc = jnp.full(bk.shape, NEG, F32)
                for bucket in range(REL_BUCKETS):
                    acc = jnp.where(bk == bucket, tab_ref[h, bucket], acc)
            out_ref[0, start:stop, :] = acc


def _far_runs(dist, masked, rows_per_run=8):
    far_rows = np.all(masked | (dist >= REL_MAX_DIST), axis=1)
    groups = far_rows.reshape(-1, rows_per_run).all(axis=1)
    runs, start = [], 0
    for g in range(1, len(groups) + 1):
        if g == len(groups) or groups[g] != groups[start]:
            runs.append((start * rows_per_run, g * rows_per_run, bool(groups[start])))
            start = g
    return tuple(runs)


def _bias_tables(rel_bias, ncp, nq):
    tab = rel_bias.T.astype(F32) * LOG2E
    n_heads = tab.shape[0]
    c = jnp.arange(TQ)[:, None]
    r = jnp.arange(TQ)[None, :]
    d_own = r - c
    bk_own = jnp.where(d_own >= 0, _rel_bucket(d_own), -1)
    bk_prev = _rel_bucket(d_own + TQ)
    bk_win2 = jnp.where(d_own + 2 * TQ < WINDOW, _rel_bucket(d_own + 2 * TQ), -1)
    band_rows = ncp + (nq - 1) * (TQ // CMP_STRIDE)
    u = jnp.arange(band_rows)[:, None] - (nq - 1) * (TQ // CMP_STRIDE)
    d_cmp = r - CMP_STRIDE * u - (CMP_LEN - 1)
    bk_cmp = jnp.where(d_cmp >= 0, _rel_bucket(d_cmp), -1)
    tables = [t.astype(jnp.int32) for t in (jnp.concatenate([bk_win2, bk_prev, bk_own], axis=0), bk_cmp)]
    cn, rn = np.arange(TQ)[:, None], np.arange(TQ)[None, :]
    d_win = np.concatenate([rn - cn + 2 * TQ, rn - cn + TQ, rn - cn], axis=0)
    m_win = np.concatenate([rn - cn + 2 * TQ >= WINDOW, np.zeros((TQ, TQ), bool), rn - cn < 0], axis=0)
    d_cmp_n = rn - CMP_STRIDE * (np.arange(band_rows)[:, None] - (nq - 1) * (TQ // CMP_STRIDE)) - (CMP_LEN - 1)
    runs = (_far_runs(d_win, m_win), _far_runs(d_cmp_n, d_cmp_n < 0))
    outs = pl.pallas_call(
        functools.partial(_bias_body, runs=runs),
        grid=(n_heads,),
        in_specs=[pl.BlockSpec(memory_space=pltpu.SMEM)]
        + [pl.BlockSpec(t.shape, lambda h: (0, 0)) for t in tables],
        out_specs=[pl.BlockSpec((1, t.shape[0], TQ), lambda h: (h // HG, 0, h % HG)) for t in tables],
        out_shape=[jax.ShapeDtypeStruct((n_heads // HG, t.shape[0], WIDE), F32) for t in tables],
        compiler_params=pltpu.CompilerParams(dimension_semantics=("parallel",), vmem_limit_bytes=VMEM_LIMIT),
        name="bias_tiles",
    )(tab, *tables)
    return outs, tab[:, REL_BUCKETS - 1]


def _slc_overlap_t(ncp, nslc):
    ratio = SLC_BLOCK // CMP_STRIDE
    mc = CMP_LEN // CMP_STRIDE
    w = np.zeros((HEAD_PAD, nslc * ratio + mc), np.float32)
    for j in range(nslc):
        for a in range(ratio):
            for m in range(mc):
                w[j, ratio * j + a + m] += 1.0
    n_cmp = ncp - 1
    out = np.zeros((HEAD_PAD, ncp), np.float32)
    out[:, :n_cmp] = w[:, :n_cmp]
    return out


def _block_onehot(s, block):
    pat = np.zeros((s, HEAD_PAD), np.float32)
    pat[np.arange(s), EXT0 + np.arange(s) // block] = 1.0
    return pat


def kernel(x, norm_ffn1, w_ffn1_gate, w_ffn1_up, w_ffn1_down, norm_mix, w_in, cmp_pos_k, cmp_w1_k, cmp_w2_k,
           cmp_pos_v, cmp_w1_v, cmp_w2_v, w_out, norm_ffn2, w_ffn2_gate, w_ffn2_up, w_ffn2_down, rel_bias,
           norm_final):
    b, s, d = x.shape
    assert d == D_MODEL and s % (2 * TQ) == 0 and norm_ffn1.shape[0] == 1
    assert s // SLC_BLOCK <= HEAD_PAD - EXT0, "selection-mask rows must fit beside the 64 head dims"
    n = b * s
    nq = s // TQ
    ncp = s // CMP_STRIDE
    nslc = s // SLC_BLOCK
    scale = HEAD_DIM ** -0.5 * LOG2E

    sizes = (512, 512, 512, 512) + (128,) * 6 + (24,)
    splits = np.cumsum(sizes)[:-1].tolist()
    wqa, wka, wva, wqb, wkc, wvc, wks, wvs, wkw, wvw, wgb = jnp.split(w_in[0], splits, axis=1)
    w_k = jnp.concatenate([wka, wkc, wvc, wks, wkw], axis=1).astype(BF16)
    w_t = jnp.concatenate([wqa * scale, wva, wqb * scale, wvs, wvw], axis=1).T.astype(BF16)
    assert w_k.shape[1] == N_KSLOTS * HEAD_PAD and w_t.shape[0] == N_TROWS * HEAD_DIM
    wgt = wgb.reshape(D_MODEL, NSA_GROUPS, NSA_REP * N_GATES)
    wgt = jnp.pad(wgt, ((0, 0), (0, 0), (0, GATE_ROWS - NSA_REP * N_GATES)))
    wgt = wgt.reshape(D_MODEL, NSA_GROUPS * GATE_ROWS).T.astype(BF16)

    def cmp_params(pos, w1, w2):
        w1_h = w1[0].reshape(CMP_LEN, HEAD_DIM, CMP_HIDDEN)
        pos_g, w1_g = [], []
        for grp in range(NSA_GROUPS):
            pad = ((0, 0), (grp * HEAD_DIM, HEAD_PAD - (grp + 1) * HEAD_DIM))
            pos_g.append(jnp.pad(pos[0], pad))
            w1_g.append(jnp.pad(w1_h, pad + ((0, 0),)))
        w2_p = jnp.pad(w2[0], ((0, 0), (0, HEAD_PAD - HEAD_DIM)))
        return jnp.stack(pos_g), jnp.stack(w1_g).astype(BF16), w2_p.astype(BF16)

    pos_k, w1k, w2k = cmp_params(cmp_pos_k, cmp_w1_k, cmp_w2_k)
    pos_v, w1v, w2v = cmp_params(cmp_pos_v, cmp_w1_v, cmp_w2_v)
    cmp_pos = jnp.stack([pos_k, pos_v])
    cmp_w1 = jnp.stack([w1k, w1v])

    wo = w_out[0]
    wo_a = wo[:MOBA_HEADS * HEAD_DIM].astype(BF16)
    wo_b = wo[MOBA_HEADS * HEAD_DIM:].astype(BF16)

    (bias_win, bias_cmp), c31 = _bias_tables(rel_bias, ncp, nq)
    ovt = jnp.asarray(_slc_overlap_t(ncp, nslc)).astype(BF16)
    pat_moba = jnp.asarray(_block_onehot(s, MOBA_BLOCK)).astype(BF16)
    pat_slc = jnp.asarray(_block_onehot(s, SLC_BLOCK)).astype(BF16)
    far = [(i, j) for j in range(nq) for i in range(j + 2, nq)]
    far_i = jnp.asarray(np.array([i for i, _ in far], np.int32))
    far_j = jnp.asarray(np.array([j for _, j in far], np.int32))

    x2d = x.reshape(n, D_MODEL)
    nf = norm_final.reshape(1, D_MODEL)
    x1 = _ffn(x2d, norm_ffn1, w_ffn1_gate[0].astype(BF16), w_ffn1_up[0].astype(BF16),
              w_ffn1_down[0].astype(BF16), nf, final_norm=False)
    pk, pt, gates_t = _inproj(x1, norm_mix, w_k, w_t, wgt, b, s)
    pk = pk.reshape(b, s, N_KSLOTS * HEAD_PAD)
    kc, vct = _compress(pk, cmp_pos, cmp_w1, w2k, w2v.T)
    o_a = _moba(pk, pt, pat_moba, bias_win, c31, far_i, far_j)
    o_b = _nsa(pk, pt, kc, vct, gates_t, pat_slc, bias_win, bias_cmp, ovt, c31, far_i, far_j)
    y = _ffn(x1, norm_ffn2, w_ffn2_gate[0].astype(BF16), w_ffn2_up[0].astype(BF16),
             w_ffn2_down[0].astype(BF16), nf, final_norm=True, mixer=(o_a, o_b, wo_a, wo_b))
    return y.reshape(b, s, D_MODEL)
```

```python
import functools
import math

import numpy as np
import jax
import jax.numpy as jnp
from jax import lax
from jax.experimental import pallas as pl
from jax.experimental.pallas import tpu as pltpu

D_MODEL = 1024
D_FF = 2816
HEAD_DIM = 64
HEAD_PAD = 128
ONES_ROW = HEAD_DIM
EXT0 = HEAD_DIM
MOBA_HEADS = 8
NSA_HEADS = 8
NSA_GROUPS = 2
NSA_REP = NSA_HEADS // NSA_GROUPS
HG = 4
MOBA_BLOCK = 256
MOBA_TOPK = 3
CMP_LEN = 32
CMP_STRIDE = 16
CMP_HIDDEN = 256
SLC_BLOCK = 64
SLC_TOPN = 16
WINDOW = 512
N_GATES = 3
GATE_ROWS = 16
REL_BUCKETS = 32
REL_MAX_EXACT = REL_BUCKETS // 2
REL_MAX_DIST = 128
RMS_EPS = 1e-6
TQ = 256
WIDE = HG * TQ
NEG = -1e30
LOG2E = 1.4426950408889634
ACC_ROWS = 80
MAX_PAIR_UNROLL = 13
VMEM_LIMIT = 56 * 1024 * 1024

KSLOT_KA, KSLOT_KC, KSLOT_VC, KSLOT_KS, KSLOT_KW = 0, 4, 5, 6, 7
N_KSLOTS = 8
TROW_QA, TROW_VA, TROW_QB, TROW_VS, TROW_VW = 0, 8, 16, 24, 26
N_TROWS = 28

F32 = jnp.float32
BF16 = jnp.bfloat16


def _dot(a, b):
    return jnp.dot(a, b, preferred_element_type=F32)


def _dot_nt(a, b):
    return lax.dot_general(a, b, (((1,), (1,)), ((), ())), preferred_element_type=F32)


def _split_bf16(x):
    hi = x.astype(BF16)
    lo = (x - hi.astype(F32)).astype(BF16)
    return hi, lo


def _rmsnorm(x, g):
    ms = jnp.mean(x * x, axis=-1, keepdims=True)
    return x * lax.rsqrt(ms + RMS_EPS) * g


def _dot_tn(a_t, b):
    return lax.dot_general(a_t, b, (((0,), (0,)), ((), ())), preferred_element_type=F32)


def _ffn_body(x_ref, g_ref, wg_ref, wu_ref, wd_ref, gf_ref, *rest, final_norm, n_chunks, with_mixer):
    x = x_ref[...]
    if with_mixer:
        oa_ref, ob_ref, woa_ref, wob_ref, o_ref = rest
        x = x + _dot_tn(oa_ref[0], woa_ref[...]) + _dot_tn(ob_ref[0], wob_ref[...])
    else:
        (o_ref,) = rest
    h = _rmsnorm(x, g_ref[...]).astype(BF16)
    tf = D_FF // n_chunks
    acc = None
    for c in range(n_chunks):
        a = _dot(h, wg_ref[:, c * tf:(c + 1) * tf])
        u = _dot(h, wu_ref[:, c * tf:(c + 1) * tf])
        z = ((a * jax.nn.sigmoid(a)) * u).astype(BF16)
        part = _dot(z, wd_ref[c * tf:(c + 1) * tf, :])
        acc = part if acc is None else acc + part
    y = x + 0.5 * acc
    if final_norm:
        y = _rmsnorm(y, gf_ref[...])
    o_ref[...] = y


def _ffn(x2d, g, wg, wu, wd, gf, *, final_norm, mixer=None, tm=512, n_chunks=1):
    n = x2d.shape[0]
    resident = lambda shape: pl.BlockSpec(shape, lambda i: (0, 0), pipeline_mode=pl.Buffered(1))
    in_specs = [
        pl.BlockSpec((tm, D_MODEL), lambda i: (i, 0)),
        resident((1, D_MODEL)),
        resident((D_MODEL, D_FF)), resident((D_MODEL, D_FF)), resident((D_FF, D_MODEL)),
        resident((1, D_MODEL)),
    ]
    args = [x2d, g, wg, wu, wd, gf]
    if mixer is not None:
        o_a, o_b, wo_a, wo_b = mixer
        spt = o_a.shape[2] // tm
        for o_t in (o_a, o_b):
            in_specs.append(pl.BlockSpec((1, o_t.shape[1], tm), lambda i: (i // spt, 0, i % spt)))
        in_specs += [resident(wo_a.shape), resident(wo_b.shape)]
        args += [o_a, o_b, wo_a, wo_b]
    return pl.pallas_call(
        functools.partial(_ffn_body, final_norm=final_norm, n_chunks=n_chunks, with_mixer=mixer is not None),
        grid=(n // tm,),
        in_specs=in_specs,
        out_specs=pl.BlockSpec((tm, D_MODEL), lambda i: (i, 0)),
        out_shape=jax.ShapeDtypeStruct((n, D_MODEL), F32),
        compiler_params=pltpu.CompilerParams(
            dimension_semantics=("parallel",), vmem_limit_bytes=VMEM_LIMIT),
        name="ffn_final" if final_norm else "ffn",
    )(*args)


def _inproj_body(x_ref, g_ref, wk_ref, wt_ref, wgt_ref, pk_ref, pt_ref, gates_ref):
    h = _rmsnorm(x_ref[...], g_ref[...]).astype(BF16)
    pk_ref[...] = _dot(h, wk_ref[...]).astype(BF16)
    pt_ref[0] = _dot_nt(wt_ref[...], h).astype(BF16)
    gates_ref[0] = jax.nn.sigmoid(_dot_nt(wgt_ref[...], h))


def _inproj(x2d, g, wk, wt, wgt, b, s, *, tm=512):
    n = x2d.shape[0]
    nk = wk.shape[1]
    nt = wt.shape[0]
    ng = wgt.shape[0]
    spt = s // tm
    const = lambda shape: pl.BlockSpec(shape, lambda i: (0, 0))
    return pl.pallas_call(
        _inproj_body,
        grid=(n // tm,),
        in_specs=[
            pl.BlockSpec((tm, D_MODEL), lambda i: (i, 0)),
            const((1, D_MODEL)), const((D_MODEL, nk)), const((nt, D_MODEL)), const((ng, D_MODEL)),
        ],
        out_specs=[
            pl.BlockSpec((tm, nk), lambda i: (i, 0)),
            pl.BlockSpec((1, nt, tm), lambda i: (i // spt, 0, i % spt)),
            pl.BlockSpec((1, ng, tm), lambda i: (i // spt, 0, i % spt)),
        ],
        out_shape=[jax.ShapeDtypeStruct((n, nk), BF16), jax.ShapeDtypeStruct((b, nt, s), BF16),
                   jax.ShapeDtypeStruct((b, ng, s), F32)],
        compiler_params=pltpu.CompilerParams(
            dimension_semantics=("parallel",), vmem_limit_bytes=VMEM_LIMIT),
        name="inproj",
    )(x2d, g, wk, wt, wgt)


def _compress_body(kraw_ref, vraw_ref, pos_ref, w1_ref, w2k_ref, w2vt_ref, kc_ref, vct_ref, xs_ref, *, ncp):
    half = CMP_LEN // 2
    for kv, raw_ref in enumerate((kraw_ref, vraw_ref)):
        xs_ref[...] = raw_ref[0].astype(F32)
        y0 = jnp.zeros((ncp, CMP_HIDDEN), F32)
        y1 = jnp.zeros((ncp, CMP_HIDDEN), F32)
        for p in range(half):
            xp = xs_ref[pl.ds(p, ncp, stride=CMP_STRIDE), :]
            y0 = y0 + _dot((xp + pos_ref[kv, 0, p:p + 1, :]).astype(BF16), w1_ref[kv, 0, p])
            y1 = y1 + _dot((xp + pos_ref[kv, 0, half + p:half + p + 1, :]).astype(BF16), w1_ref[kv, 0, half + p])
        h1 = y0 + pltpu.roll(y1, ncp - 1, 0)
        act = (h1 * jax.nn.sigmoid(h1)).astype(BF16)
        if kv == 0:
            kc_ref[0, 0] = _dot(act, w2k_ref[...]).astype(BF16)
        else:
            out_t = _dot_nt(w2vt_ref[...], act)
            row = lax.broadcasted_iota(jnp.int32, out_t.shape, 0)
            vct_ref[0, 0] = (out_t + jnp.where(row == ONES_ROW, 1.0, 0.0)).astype(BF16)


def _compress(pk, pos, w1, w2k, w2vt):
    b, s, _ = pk.shape
    ncp = s // CMP_STRIDE
    full = lambda shape: pl.BlockSpec(shape, lambda bi, g: (0,) * len(shape))
    return pl.pallas_call(
        functools.partial(_compress_body, ncp=ncp),
        grid=(b, NSA_GROUPS),
        in_specs=[
            pl.BlockSpec((1, s, HEAD_PAD), lambda bi, g: (bi, 0, KSLOT_KC)),
            pl.BlockSpec((1, s, HEAD_PAD), lambda bi, g: (bi, 0, KSLOT_VC)),
            pl.BlockSpec((2, 1, CMP_LEN, HEAD_PAD), lambda bi, g: (0, g, 0, 0)),
            pl.BlockSpec((2, 1, CMP_LEN, HEAD_PAD, CMP_HIDDEN), lambda bi, g: (0, g, 0, 0, 0)),
            full((CMP_HIDDEN, HEAD_PAD)),
            full((HEAD_PAD, CMP_HIDDEN)),
        ],
        out_specs=[
            pl.BlockSpec((1, 1, ncp, HEAD_PAD), lambda bi, g: (bi, g, 0, 0)),
            pl.BlockSpec((1, 1, HEAD_PAD, ncp), lambda bi, g: (bi, g, 0, 0)),
        ],
        out_shape=[
            jax.ShapeDtypeStruct((b, NSA_GROUPS, ncp, HEAD_PAD), BF16),
            jax.ShapeDtypeStruct((b, NSA_GROUPS, HEAD_PAD, ncp), BF16),
        ],
        scratch_shapes=[pltpu.VMEM((s, HEAD_PAD), F32)],
        compiler_params=pltpu.CompilerParams(
            dimension_semantics=("parallel", "arbitrary"), vmem_limit_bytes=VMEM_LIMIT),
        name="compress",
    )(pk, pk, pos, w1, w2k, w2vt)


def _flash_update(s_t, shift, pv_fn, m_ref, acc_ref):
    m_prev = m_ref[...]
    m_new = jnp.maximum(m_prev, jnp.max(s_t, axis=0, keepdims=True) + shift)
    alpha = jnp.exp2(m_prev - m_new)
    p = jnp.exp2(s_t - (m_new - shift)).astype(BF16)
    acc_ref[...] = alpha * acc_ref[...] + pv_fn(p)
    m_ref[...] = m_new


def _state_lanes(i):
    return pl.ds(pl.multiple_of(i * WIDE, WIDE), WIDE)


def _tile_phase(count, tile_fn, scores_fn, step_fn, sa_ref, sb_ref):
    if count == 0:
        return
    tile = lambda n: tile_fn(jnp.minimum(jnp.asarray(n, jnp.int32), count - 1))
    sa_ref[...] = scores_fn(*tile(0))

    def pair(p, carry):
        n = 2 * p
        sb_ref[...] = scores_fn(*tile(n + 1))
        step_fn(sa_ref, *tile(n))
        sa_ref[...] = scores_fn(*tile(n + 2))
        step_fn(sb_ref, *tile(n + 1))
        return carry

    n_pairs = count // 2
    unroll = max(u for u in range(1, MAX_PAIR_UNROLL + 1) if n_pairs % u == 0) if n_pairs else 1
    lax.fori_loop(0, n_pairs, pair, 0, unroll=unroll)
    if count % 2:
        step_fn(sa_ref, *tile(count - 1))


def _flat_stream(nq, n_far, far_i_ref, far_j_ref, scores_fn, pv_fn, bias_ref, shift_far,
                 sa_ref, sb_ref, m_ref, acc_ref):
    zero = jnp.zeros((1, WIDE), F32)

    def own_step(s_ref, i, j):
        s_t = s_ref[...] + bias_ref[0, 2 * TQ:3 * TQ, :]
        m = jnp.max(s_t, axis=0, keepdims=True)
        m_ref[:, _state_lanes(i)] = m
        acc_ref[:, _state_lanes(i)] = pv_fn(j)(jnp.exp2(s_t - m).astype(BF16))

    def prev_step(s_ref, i, j):
        _flash_update(s_ref[...] + bias_ref[0, TQ:2 * TQ, :], zero, pv_fn(j),
                      m_ref.at[:, _state_lanes(i)], acc_ref.at[:, _state_lanes(i)])

    def far_step(s_ref, i, j):
        _flash_update(s_ref[...], shift_far, pv_fn(j), m_ref.at[:, _state_lanes(i)], acc_ref.at[:, _state_lanes(i)])

    _tile_phase(nq, lambda n: (n, n), scores_fn, own_step, sa_ref, sb_ref)
    _tile_phase(nq - 1, lambda n: (n + 1, n), scores_fn, prev_step, sa_ref, sb_ref)
    _tile_phase(n_far, lambda n: (far_i_ref[n], far_j_ref[n]), scores_fn, far_step, sa_ref, sb_ref)


def _lane_block(x, r):
    return x[:, r * TQ:(r + 1) * TQ]


def _shift_row(c31_ref, head0):
    return jnp.concatenate([jnp.full((1, TQ), c31_ref[head0 + r], F32) for r in range(HG)], axis=1)


def _rank_select(score, n_rows, top, vote=None):
    sub = 8
    n_groups = score.shape[0] // sub
    groups = [score[g * sub:(g + 1) * sub] for g in range(n_groups)]
    counts = [jnp.zeros(groups[0].shape, F32) for _ in range(n_groups)]
    local = lax.broadcasted_iota(jnp.int32, groups[0].shape, 0)
    for j in range(n_rows):
        row = score[j:j + 1, :]
        one = 1.0 if vote is None else vote(j)
        for g in range(n_groups):
            if j < g * sub:
                ahead = jnp.where(row >= groups[g], one, 0.0)
            elif j >= (g + 1) * sub:
                ahead = jnp.where(row > groups[g], one, 0.0)
            else:
                ahead = jnp.where(local > j - g * sub, jnp.where(row >= groups[g], one, 0.0),
                                  jnp.where(row > groups[g], one, 0.0))
            counts[g] = counts[g] + ahead
    return jnp.concatenate(counts, axis=0) < top


def _key_operand(slot, half, pat):
    if not (isinstance(half, int) and half == 0):
        slot = pltpu.roll(slot.astype(F32), HEAD_DIM * half, 1).astype(BF16)
    lane = lax.broadcasted_iota(jnp.int32, slot.shape, 1)
    return jnp.where(lane < HEAD_DIM, slot, pat)


def _value_operand(v_t):
    rows = lax.broadcasted_iota(jnp.int32, (ACC_ROWS - HEAD_DIM, v_t.shape[1]), 0)
    return jnp.concatenate([v_t, jnp.where(rows == 0, 1.0, 0.0).astype(BF16)], axis=0)


def _moba_body(c31_ref, far_i_ref, far_j_ref, q_ref, k_ref, vt_ref, pat_ref, bias_ref, o_ref,
               kmean_ref, kx_ref, vx_ref, qx_ref, sa_ref, sb_ref, m_ref, acc_ref, *, nb, n_far):
    hg = pl.program_id(1)
    s = q_ref.shape[2]
    nbp = -(-nb // 16) * 16

    kmean_ref[...] = jnp.zeros_like(kmean_ref)
    blk = lax.broadcasted_iota(jnp.int32, (nbp, s), 0)
    cur = lax.broadcasted_iota(jnp.int32, (nbp, s), 1) // MOBA_BLOCK
    zero_rows = jnp.zeros((HEAD_PAD - HEAD_DIM, s), BF16)
    for r in range(HG):
        kx = _key_operand(k_ref[0, :, (r // 2) * HEAD_PAD:(r // 2 + 1) * HEAD_PAD], r % 2, pat_ref[...])
        kx_ref[:, r * HEAD_PAD:(r + 1) * HEAD_PAD] = kx
        vx_ref[r] = _value_operand(vt_ref[0, r * HEAD_DIM:(r + 1) * HEAD_DIM, :])
        for n in range(nb):
            kb = kx[n * MOBA_BLOCK:(n + 1) * MOBA_BLOCK, :].astype(F32)
            kmean_ref[r, n:n + 1, :] = jnp.mean(kb, axis=0, keepdims=True)
        q_t = q_ref[0, r * HEAD_DIM:(r + 1) * HEAD_DIM, :]
        q_pad = jnp.concatenate([q_t, zero_rows], axis=0)
        kh, kl = _split_bf16(kmean_ref[r, :nbp, :])
        gate = _dot(kh, q_pad) + _dot(kl, q_pad)
        top = _rank_select(gate, nb, MOBA_TOPK, lambda n: jnp.where(n < cur[:8], 1.0, 0.0))
        sel = ((blk < cur) & top) | (blk == cur)
        mask_rows = jnp.where(sel, 0.0, NEG).astype(BF16)
        pad_rows = jnp.zeros((HEAD_PAD - EXT0 - nbp, s), BF16)
        qx_ref[r] = jnp.concatenate([q_t, mask_rows, pad_rows], axis=0)

    def scores(i, j):
        qoff = pl.multiple_of(i * TQ, TQ)
        koff = pl.multiple_of(j * TQ, TQ)
        return jnp.concatenate(
            [_dot(kx_ref[pl.ds(koff, TQ), r * HEAD_PAD:(r + 1) * HEAD_PAD], qx_ref[r, :, pl.ds(qoff, TQ)])
             for r in range(HG)], axis=1)

    def pv_fn(j):
        off = pl.multiple_of(j * TQ, TQ)

        def fn(p):
            return jnp.concatenate(
                [_dot(vx_ref[r, :, pl.ds(off, TQ)], _lane_block(p, r)) for r in range(HG)], axis=1)
        return fn

    _flat_stream(nb, n_far, far_i_ref, far_j_ref, scores, pv_fn, bias_ref, _shift_row(c31_ref, hg * HG),
                 sa_ref, sb_ref, m_ref, acc_ref)

    def finish(i, carry):
        acc = acc_ref[:, _state_lanes(i)]
        out = (acc[:HEAD_DIM] / acc[ONES_ROW:ONES_ROW + 1, :]).astype(BF16)
        for r in range(HG):
            o_ref[0, r * HEAD_DIM:(r + 1) * HEAD_DIM, pl.ds(pl.multiple_of(i * TQ, TQ), TQ)] = _lane_block(out, r)
        return carry

    lax.fori_loop(0, nb, finish, 0)


def _moba(pk, pt, pat, bias_win, c31, far_i, far_j):
    b, s, _ = pk.shape
    nb = s // MOBA_BLOCK
    hrows = HG * HEAD_DIM
    smem = pl.BlockSpec(memory_space=pltpu.SMEM)
    return pl.pallas_call(
        functools.partial(_moba_body, nb=nb, n_far=far_i.shape[0]),
        grid=(b, MOBA_HEADS // HG),
        in_specs=[
            smem, smem, smem,
            pl.BlockSpec((1, hrows, s), lambda bi, hg: (bi, TROW_QA // HG + hg, 0)),
            pl.BlockSpec((1, s, hrows), lambda bi, hg: (bi, 0, KSLOT_KA * HEAD_PAD // hrows + hg)),
            pl.BlockSpec((1, hrows, s), lambda bi, hg: (bi, TROW_VA // HG + hg, 0)),
            pl.BlockSpec((s, HEAD_PAD), lambda bi, hg: (0, 0), pipeline_mode=pl.Buffered(1)),
            pl.BlockSpec((1, 3 * TQ, WIDE), lambda bi, hg: (hg, 0, 0)),
        ],
        out_specs=pl.BlockSpec((1, HG * HEAD_DIM, s), lambda bi, hg: (bi, hg, 0)),
        out_shape=jax.ShapeDtypeStruct((b, MOBA_HEADS * HEAD_DIM, s), BF16),
        scratch_shapes=[
            pltpu.VMEM((HG, HEAD_PAD, HEAD_PAD), F32),
            pltpu.VMEM((s, HG * HEAD_PAD), BF16),
            pltpu.VMEM((HG, ACC_ROWS, s), BF16),
            pltpu.VMEM((HG, HEAD_PAD, s), BF16),
            pltpu.VMEM((TQ, WIDE), F32),
            pltpu.VMEM((TQ, WIDE), F32),
            pltpu.VMEM((1, nb * WIDE), F32),
            pltpu.VMEM((ACC_ROWS, nb * WIDE), F32),
        ],
        compiler_params=pltpu.CompilerParams(
            dimension_semantics=("parallel", "arbitrary"), vmem_limit_bytes=VMEM_LIMIT),
        name="moba",
    )(c31, far_i, far_j, pt, pk, pt, pat, bias_win)


def _nsa_body(c31_ref, far_i_ref, far_j_ref, q_ref, ks_ref, kw_ref, vst_ref, vwt_ref, kc_ref, vct_ref,
              gates_ref, pat_ref, bias_ref, bcmp_ref, ovt_ref, o_ref,
              ksx_ref, kwx_ref, vsx_ref, vwx_ref, qx_ref, sa_ref, sb_ref, m_ref, accs_ref, part_ref,
              *, ncp, nslc, nq, n_far):
    g = pl.program_id(1)

    ksx_ref[...] = _key_operand(ks_ref[0], g, pat_ref[...])
    kwx_ref[...] = _key_operand(kw_ref[0], g, jnp.zeros((), BF16))
    grows = pl.ds(pl.multiple_of(g * HEAD_DIM, HEAD_DIM), HEAD_DIM)
    vsx_ref[...] = _value_operand(vst_ref[0, grows, :])
    vwx_ref[...] = _value_operand(vwt_ref[0, grows, :])
    zero_rows = jnp.zeros((HEAD_PAD - HEAD_DIM, WIDE), BF16)

    def front(i, n_win, n_act):
        qoff = pl.multiple_of(i * TQ, TQ)
        q4c = jnp.concatenate([q_ref[0, r * HEAD_DIM:(r + 1) * HEAD_DIM, pl.ds(qoff, TQ)] for r in range(HG)],
                              axis=1)
        q4 = jnp.concatenate([q4c, zero_rows], axis=0)

        band_off = pl.multiple_of((nq - 1 - i) * (TQ // CMP_STRIDE), TQ // CMP_STRIDE)
        s_t = _dot(kc_ref[0, 0], q4) + bcmp_ref[0, pl.ds(band_off, ncp), :]
        m = jnp.max(s_t, axis=0, keepdims=True)
        p = jnp.exp2(s_t - m)
        den = jnp.sum(p, axis=0, keepdims=True)
        pn = p * jnp.where(m > 0.5 * NEG, 1.0 / den, 0.0)
        o_cmp = _dot(vct_ref[0, 0, :ACC_ROWS, :], pn.astype(BF16))
        psum = _lane_block(pn, 0)
        for r in range(1, HG):
            psum = psum + _lane_block(pn, r)

        ph, plo = _split_bf16(psum)
        imp = (_dot(ovt_ref[...], ph) + _dot(ovt_ref[...], plo))[:n_act]
        blk = lax.broadcasted_iota(jnp.int32, imp.shape, 0)
        cur = (i * TQ + lax.broadcasted_iota(jnp.int32, imp.shape, 1)) // SLC_BLOCK
        forced = (blk == 0) | (blk == cur) | (blk == cur - 1)
        score = jnp.where(forced, 1e30, jnp.where(blk <= cur, imp, -1.0))
        sel = _rank_select(score, n_act, SLC_TOPN)
        mask_rows = jnp.where(sel, 0.0, NEG)
        if n_act < EXT0:
            mask_rows = jnp.concatenate([mask_rows, jnp.full((EXT0 - n_act, TQ), NEG, F32)], axis=0)
        mask_rows = mask_rows.astype(BF16)
        qx_ref[:, _state_lanes(i)] = jnp.concatenate(
            [q4c, jnp.concatenate([mask_rows] * HG, axis=1)], axis=0)

        start = pl.multiple_of((i + 1 - n_win) * TQ, TQ)
        s_w = _dot(kwx_ref[pl.ds(start, n_win * TQ), :], q4) + bias_ref[0, (3 - n_win) * TQ:, :]
        p_w = jnp.exp2(s_w - jnp.max(s_w, axis=0, keepdims=True)).astype(BF16)
        acc_w = _dot(vwx_ref[:, pl.ds(start, n_win * TQ)], p_w)
        o_win = acc_w / acc_w[ONES_ROW:ONES_ROW + 1, :]

        gt = gates_ref[0, :, pl.ds(qoff, TQ)]
        part_ref[:, _state_lanes(i)] = jnp.concatenate(
            [gt[N_GATES * r:N_GATES * r + 1, :] * _lane_block(o_cmp, r)
             + gt[N_GATES * r + 2:N_GATES * r + 3, :] * _lane_block(o_win, r) for r in range(HG)], axis=1)

    blocks_upto = lambda i_end: min(-(-nslc // 8) * 8, -(-i_end * TQ // SLC_BLOCK // 8) * 8)
    for i0 in range(min(2, nq)):
        front(jnp.asarray(i0, jnp.int32), i0 + 1, blocks_upto(i0 + 1))

    mid = max(2, nq // 2)
    for lo, hi in ((2, mid), (mid, nq)):
        def front_loop(i, carry, n_act=blocks_upto(hi)):
            front(i, 3, n_act)
            return carry

        if hi > lo:
            lax.fori_loop(lo, hi, front_loop, 0, unroll=2)

    def scores(i, j):
        return _dot(ksx_ref[pl.ds(pl.multiple_of(j * TQ, TQ), TQ), :], qx_ref[:, _state_lanes(i)])

    def pv_fn(j):
        off = pl.multiple_of(j * TQ, TQ)
        return lambda p: _dot(vsx_ref[:, pl.ds(off, TQ)], p)

    _flat_stream(nq, n_far, far_i_ref, far_j_ref, scores, pv_fn, bias_ref,
                 _shift_row(c31_ref, MOBA_HEADS + g * HG), sa_ref, sb_ref, m_ref, accs_ref)

    def finish(i, carry):
        qoff = pl.multiple_of(i * TQ, TQ)
        acc = accs_ref[:, _state_lanes(i)]
        o_slc = acc / acc[ONES_ROW:ONES_ROW + 1, :]
        part = part_ref[:, _state_lanes(i)]
        gt = gates_ref[0, :, pl.ds(qoff, TQ)]
        for r in range(HG):
            out = _lane_block(part, r) + gt[N_GATES * r + 1:N_GATES * r + 2, :] * _lane_block(o_slc, r)
            o_ref[0, r * HEAD_DIM:(r + 1) * HEAD_DIM, pl.ds(qoff, TQ)] = out[:HEAD_DIM].astype(BF16)
        return carry

    lax.fori_loop(0, nq, finish, 0)


def _nsa(pk, pt, kc, vct, gates_t, pat, bias_win, bias_cmp, ovt, c31, far_i, far_j):
    b, s, _ = pk.shape
    nq = s // TQ
    ncp = s // CMP_STRIDE
    nslc = s // SLC_BLOCK
    hrows = HG * HEAD_DIM
    band_rows = bias_cmp.shape[1]
    hgrp = MOBA_HEADS // HG
    smem = pl.BlockSpec(memory_space=pltpu.SMEM)
    k_spec = lambda slot: pl.BlockSpec((1, s, HEAD_PAD), lambda bi, g: (bi, 0, slot))
    vt_spec = lambda row: pl.BlockSpec((1, HEAD_PAD, s), lambda bi, g: (bi, row * HEAD_DIM // HEAD_PAD, 0))
    bias_spec = lambda rows: pl.BlockSpec((1, rows, WIDE), lambda bi, g: (hgrp + g, 0, 0),
                                          pipeline_mode=pl.Buffered(1))
    return pl.pallas_call(
        functools.partial(_nsa_body, ncp=ncp, nslc=nslc, nq=nq, n_far=far_i.shape[0]),
        grid=(b, NSA_GROUPS),
        in_specs=[
            smem, smem, smem,
            pl.BlockSpec((1, hrows, s), lambda bi, g: (bi, TROW_QB // HG + g, 0)),
            k_spec(KSLOT_KS), k_spec(KSLOT_KW), vt_spec(TROW_VS), vt_spec(TROW_VW),
            pl.BlockSpec((1, 1, ncp, HEAD_PAD), lambda bi, g: (bi, g, 0, 0)),
            pl.BlockSpec((1, 1, HEAD_PAD, ncp), lambda bi, g: (bi, g, 0, 0)),
            pl.BlockSpec((1, GATE_ROWS, s), lambda bi, g: (bi, g, 0)),
            pl.BlockSpec((s, HEAD_PAD), lambda bi, g: (0, 0), pipeline_mode=pl.Buffered(1)),
            bias_spec(3 * TQ), bias_spec(band_rows),
            pl.BlockSpec((HEAD_PAD, ncp), lambda bi, g: (0, 0)),
        ],
        out_specs=pl.BlockSpec((1, HG * HEAD_DIM, s), lambda bi, g: (bi, g, 0)),
        out_shape=jax.ShapeDtypeStruct((b, NSA_HEADS * HEAD_DIM, s), BF16),
        scratch_shapes=[
            pltpu.VMEM((s, HEAD_PAD), BF16),
            pltpu.VMEM((s, HEAD_PAD), BF16),
            pltpu.VMEM((ACC_ROWS, s), BF16),
            pltpu.VMEM((ACC_ROWS, s), BF16),
            pltpu.VMEM((HEAD_PAD, nq * WIDE), BF16),
            pltpu.VMEM((TQ, WIDE), F32),
            pltpu.VMEM((TQ, WIDE), F32),
            pltpu.VMEM((1, nq * WIDE), F32),
            pltpu.VMEM((ACC_ROWS, nq * WIDE), F32),
            pltpu.VMEM((ACC_ROWS, nq * WIDE), F32),
        ],
        compiler_params=pltpu.CompilerParams(
            dimension_semantics=("parallel", "arbitrary"), vmem_limit_bytes=VMEM_LIMIT),
        name="nsa",
    )(c31, far_i, far_j, pt, pk, pk, pt, pt, kc, vct, gates_t, pat, bias_win, bias_cmp, ovt)


def _rel_bucket(dist):
    n = jnp.maximum(dist, 0)
    nf = jnp.maximum(n, 1).astype(F32)
    large = REL_MAX_EXACT + (jnp.log(nf / REL_MAX_EXACT) / math.log(REL_MAX_DIST / REL_MAX_EXACT)
                             * (REL_BUCKETS - REL_MAX_EXACT)).astype(jnp.int32)
    return jnp.where(n < REL_MAX_EXACT, n, jnp.minimum(large, REL_BUCKETS - 1))


def _bias_body(tab_ref, *refs, runs):
    h = pl.program_id(0)
    n_tab = len(refs) // 2
    for bk_ref, out_ref, table_runs in zip(refs[:n_tab], refs[n_tab:], runs):
        for start, stop, far in table_runs:
            bk = bk_ref[start:stop, :]
            if far:
                acc = jnp.where(bk >= 0, tab_ref[h, REL_BUCKETS - 1], NEG)
            else:
                acc = jnp.full(bk.shape, NEG, F32)
                for bucket in range(REL_BUCKETS):
                    acc = jnp.where(bk == bucket, tab_ref[h, bucket], acc)
            out_ref[0, start:stop, :] = acc


def _far_runs(dist, masked, rows_per_run=8):
    far_rows = np.all(masked | (dist >= REL_MAX_DIST), axis=1)
    groups = far_rows.reshape(-1, rows_per_run).all(axis=1)
    runs, start = [], 0
    for g in range(1, len(groups) + 1):
        if g == len(groups) or groups[g] != groups[start]:
            runs.append((start * rows_per_run, g * rows_per_run, bool(groups[start])))
            start = g
    return tuple(runs)


def _bias_tables(rel_bias, ncp, nq):
    tab = rel_bias.T.astype(F32) * LOG2E
    n_heads = tab.shape[0]
    c = jnp.arange(TQ)[:, None]
    r = jnp.arange(TQ)[None, :]
    d_own = r - c
    bk_own = jnp.where(d_own >= 0, _rel_bucket(d_own), -1)
    bk_prev = _rel_bucket(d_own + TQ)
    bk_win2 = jnp.where(d_own + 2 * TQ < WINDOW, _rel_bucket(d_own + 2 * TQ), -1)
    band_rows = ncp + (nq - 1) * (TQ // CMP_STRIDE)
    u = jnp.arange(band_rows)[:, None] - (nq - 1) * (TQ // CMP_STRIDE)
    d_cmp = r - CMP_STRIDE * u - (CMP_LEN - 1)
    bk_cmp = jnp.where(d_cmp >= 0, _rel_bucket(d_cmp), -1)
    tables = [t.astype(jnp.int32) for t in (jnp.concatenate([bk_win2, bk_prev, bk_own], axis=0), bk_cmp)]
    cn, rn = np.arange(TQ)[:, None], np.arange(TQ)[None, :]
    d_win = np.concatenate([rn - cn + 2 * TQ, rn - cn + TQ, rn - cn], axis=0)
    m_win = np.concatenate([rn - cn + 2 * TQ >= WINDOW, np.zeros((TQ, TQ), bool), rn - cn < 0], axis=0)
    d_cmp_n = rn - CMP_STRIDE * (np.arange(band_rows)[:, None] - (nq - 1) * (TQ // CMP_STRIDE)) - (CMP_LEN - 1)
    runs = (_far_runs(d_win, m_win), _far_runs(d_cmp_n, d_cmp_n < 0))
    outs = pl.pallas_call(
        functools.partial(_bias_body, runs=runs),
        grid=(n_heads,),
        in_specs=[pl.BlockSpec(memory_space=pltpu.SMEM)]
        + [pl.BlockSpec(t.shape, lambda h: (0, 0)) for t in tables],
        out_specs=[pl.BlockSpec((1, t.shape[0], TQ), lambda h: (h // HG, 0, h % HG)) for t in tables],
        out_shape=[jax.ShapeDtypeStruct((n_heads // HG, t.shape[0], WIDE), F32) for t in tables],
        compiler_params=pltpu.CompilerParams(dimension_semantics=("parallel",), vmem_limit_bytes=VMEM_LIMIT),
        name="bias_tiles",
    )(tab, *tables)
    return outs, tab[:, REL_BUCKETS - 1]


def _slc_overlap_t(ncp, nslc):
    ratio = SLC_BLOCK // CMP_STRIDE
    mc = CMP_LEN // CMP_STRIDE
    w = np.zeros((HEAD_PAD, nslc * ratio + mc), np.float32)
    for j in range(nslc):
        for a in range(ratio):
            for m in range(mc):
                w[j, ratio * j + a + m] += 1.0
    n_cmp = ncp - 1
    out = np.zeros((HEAD_PAD, ncp), np.float32)
    out[:, :n_cmp] = w[:, :n_cmp]
    return out


def _block_onehot(s, block):
    pat = np.zeros((s, HEAD_PAD), np.float32)
    pat[np.arange(s), EXT0 + np.arange(s) // block] = 1.0
    return pat


def kernel(x, norm_ffn1, w_ffn1_gate, w_ffn1_up, w_ffn1_down, norm_mix, w_in, cmp_pos_k, cmp_w1_k, cmp_w2_k,
           cmp_pos_v, cmp_w1_v, cmp_w2_v, w_out, norm_ffn2, w_ffn2_gate, w_ffn2_up, w_ffn2_down, rel_bias,
           norm_final):
    b, s, d = x.shape
    assert d == D_MODEL and s % (2 * TQ) == 0 and norm_ffn1.shape[0] == 1
    assert s // SLC_BLOCK <= HEAD_PAD - EXT0, "selection-mask rows must fit beside the 64 head dims"
    n = b * s
    nq = s // TQ
    ncp = s // CMP_STRIDE
    nslc = s // SLC_BLOCK
    scale = HEAD_DIM ** -0.5 * LOG2E

    sizes = (512, 512, 512, 512) + (128,) * 6 + (24,)
    splits = np.cumsum(sizes)[:-1].tolist()
    wqa, wka, wva, wqb, wkc, wvc, wks, wvs, wkw, wvw, wgb = jnp.split(w_in[0], splits, axis=1)
    w_k = jnp.concatenate([wka, wkc, wvc, wks, wkw], axis=1).astype(BF16)
    w_t = jnp.concatenate([wqa * scale, wva, wqb * scale, wvs, wvw], axis=1).T.astype(BF16)
    assert w_k.shape[1] == N_KSLOTS * HEAD_PAD and w_t.shape[0] == N_TROWS * HEAD_DIM
    wgt = wgb.reshape(D_MODEL, NSA_GROUPS, NSA_REP * N_GATES)
    wgt = jnp.pad(wgt, ((0, 0), (0, 0), (0, GATE_ROWS - NSA_REP * N_GATES)))
    wgt = wgt.reshape(D_MODEL, NSA_GROUPS * GATE_ROWS).T.astype(BF16)

    def cmp_params(pos, w1, w2):
        w1_h = w1[0].reshape(CMP_LEN, HEAD_DIM, CMP_HIDDEN)
        pos_g, w1_g = [], []
        for grp in range(NSA_GROUPS):
            pad = ((0, 0), (grp * HEAD_DIM, HEAD_PAD - (grp + 1) * HEAD_DIM))
            pos_g.append(jnp.pad(pos[0], pad))
            w1_g.append(jnp.pad(w1_h, pad + ((0, 0),)))
        w2_p = jnp.pad(w2[0], ((0, 0), (0, HEAD_PAD - HEAD_DIM)))
        return jnp.stack(pos_g), jnp.stack(w1_g).astype(BF16), w2_p.astype(BF16)

    pos_k, w1k, w2k = cmp_params(cmp_pos_k, cmp_w1_k, cmp_w2_k)
    pos_v, w1v, w2v = cmp_params(cmp_pos_v, cmp_w1_v, cmp_w2_v)
    cmp_pos = jnp.stack([pos_k, pos_v])
    cmp_w1 = jnp.stack([w1k, w1v])

    wo = w_out[0]
    wo_a = wo[:MOBA_HEADS * HEAD_DIM].astype(BF16)
    wo_b = wo[MOBA_HEADS * HEAD_DIM:].astype(BF16)

    (bias_win, bias_cmp), c31 = _bias_tables(rel_bias, ncp, nq)
    ovt = jnp.asarray(_slc_overlap_t(ncp, nslc)).astype(BF16)
    pat_moba = jnp.asarray(_block_onehot(s, MOBA_BLOCK)).astype(BF16)
    pat_slc = jnp.asarray(_block_onehot(s, SLC_BLOCK)).astype(BF16)
    far = [(i, j) for j in range(nq) for i in range(j + 2, nq)]
    far_i = jnp.asarray(np.array([i for i, _ in far], np.int32))
    far_j = jnp.asarray(np.array([j for _, j in far], np.int32))

    x2d = x.reshape(n, D_MODEL)
    nf = norm_final.reshape(1, D_MODEL)
    x1 = _ffn(x2d, norm_ffn1, w_ffn1_gate[0].astype(BF16), w_ffn1_up[0].astype(BF16),
              w_ffn1_down[0].astype(BF16), nf, final_norm=False)
    pk, pt, gates_t = _inproj(x1, norm_mix, w_k, w_t, wgt, b, s)
    pk = pk.reshape(b, s, N_KSLOTS * HEAD_PAD)
    kc, vct = _compress(pk, cmp_pos, cmp_w1, w2k, w2v.T)
    o_a = _moba(pk, pt, pat_moba, bias_win, c31, far_i, far_j)
    o_b = _nsa(pk, pt, kc, vct, gates_t, pat_slc, bias_win, bias_cmp, ovt, c31, far_i, far_j)
    y = _ffn(x1, norm_ffn2, w_ffn2_gate[0].astype(BF16), w_ffn2_up[0].astype(BF16),
             w_ffn2_down[0].astype(BF16), nf, final_norm=True, mixer=(o_a, o_b, wo_a, wo_b))
    return y.reshape(b, s, D_MODEL)
```

```python
import functools
import math

import numpy as np
import jax
import jax.numpy as jnp
from jax import lax
from jax.experimental import pallas as pl
from jax.experimental.pallas import tpu as pltpu

D_MODEL = 1024
D_FF = 2816
HEAD_DIM = 64
HEAD_PAD = 128
ONES_ROW = HEAD_DIM
EXT0 = HEAD_DIM
MOBA_HEADS = 8
NSA_HEADS = 8
NSA_GROUPS = 2
NSA_REP = NSA_HEADS // NSA_GROUPS
HG = 4
MOBA_BLOCK = 256
MOBA_TOPK = 3
CMP_LEN = 32
CMP_STRIDE = 16
CMP_HIDDEN = 256
SLC_BLOCK = 64
SLC_TOPN = 16
WINDOW = 512
N_GATES = 3
GATE_ROWS = 16
REL_BUCKETS = 32
REL_MAX_EXACT = REL_BUCKETS // 2
REL_MAX_DIST = 128
RMS_EPS = 1e-6
TQ = 256
WIDE = HG * TQ
NEG = -1e30
LOG2E = 1.4426950408889634
ACC_ROWS = 80
MAX_PAIR_UNROLL = 13
VMEM_LIMIT = 56 * 1024 * 1024

KSLOT_KA, KSLOT_KC, KSLOT_VC, KSLOT_KS, KSLOT_KW = 0, 4, 5, 6, 7
N_KSLOTS = 8
TROW_QA, TROW_VA, TROW_QB, TROW_VS, TROW_VW = 0, 8, 16, 24, 26
N_TROWS = 28

F32 = jnp.float32
BF16 = jnp.bfloat16


def _dot(a, b):
    return jnp.dot(a, b, preferred_element_type=F32)


def _dot_nt(a, b):
    return lax.dot_general(a, b, (((1,), (1,)), ((), ())), preferred_element_type=F32)


def _split_bf16(x):
    hi = x.astype(BF16)
    lo = (x - hi.astype(F32)).astype(BF16)
    return hi, lo


def _rmsnorm(x, g):
    ms = jnp.mean(x * x, axis=-1, keepdims=True)
    return x * lax.rsqrt(ms + RMS_EPS) * g


def _dot_tn(a_t, b):
    return lax.dot_general(a_t, b, (((0,), (0,)), ((), ())), preferred_element_type=F32)


def _ffn_body(x_ref, g_ref, wg_ref, wu_ref, wd_ref, gf_ref, *rest, final_norm, n_chunks, with_mixer,
              with_proj=False):
    x = x_ref[...]
    if with_mixer:
        oa_ref, ob_ref, woa_ref, wob_ref, o_ref = rest
        x = x + _dot_tn(oa_ref[0], woa_ref[...]) + _dot_tn(ob_ref[0], wob_ref[...])
    elif with_proj:
        gm_ref, wk_ref, wt_ref, wgt_ref, o_ref, pk_ref, pt_ref, gates_ref = rest
    else:
        (o_ref,) = rest
    h = _rmsnorm(x, g_ref[...]).astype(BF16)
    tf = D_FF // n_chunks
    acc = None
    for c in range(n_chunks):
        a = _dot(h, wg_ref[:, c * tf:(c + 1) * tf])
        u = _dot(h, wu_ref[:, c * tf:(c + 1) * tf])
        z = ((a * jax.nn.sigmoid(a)) * u).astype(BF16)
        part = _dot(z, wd_ref[c * tf:(c + 1) * tf, :])
        acc = part if acc is None else acc + part
    y = x + 0.5 * acc
    if final_norm:
        y = _rmsnorm(y, gf_ref[...])
    o_ref[...] = y
    if with_proj:
        h2 = _rmsnorm(y, gm_ref[...]).astype(BF16)
        pk_ref[...] = _dot(h2, wk_ref[...]).astype(BF16)
        pt_ref[0] = _dot_nt(wt_ref[...], h2).astype(BF16)
        gates_ref[0] = jax.nn.sigmoid(_dot_nt(wgt_ref[...], h2))


def _ffn(x2d, g, wg, wu, wd, gf, *, final_norm, mixer=None, proj=None, tm=512, n_chunks=1):
    n = x2d.shape[0]
    resident = lambda shape: pl.BlockSpec(shape, lambda i: (0, 0), pipeline_mode=pl.Buffered(1))
    in_specs = [
        pl.BlockSpec((tm, D_MODEL), lambda i: (i, 0)),
        resident((1, D_MODEL)),
        resident((D_MODEL, D_FF)), resident((D_MODEL, D_FF)), resident((D_FF, D_MODEL)),
        resident((1, D_MODEL)),
    ]
    args = [x2d, g, wg, wu, wd, gf]
    if mixer is not None:
        o_a, o_b, wo_a, wo_b = mixer
        spt = o_a.shape[2] // tm
        for o_t in (o_a, o_b):
            in_specs.append(pl.BlockSpec((1, o_t.shape[1], tm), lambda i: (i // spt, 0, i % spt)))
        in_specs += [resident(wo_a.shape), resident(wo_b.shape)]
        args += [o_a, o_b, wo_a, wo_b]
    out_specs = pl.BlockSpec((tm, D_MODEL), lambda i: (i, 0))
    out_shape = jax.ShapeDtypeStruct((n, D_MODEL), F32)
    if proj is not None:
        gm, wk, wt, wgt, b, s = proj
        spt = s // tm
        in_specs += [resident(a.shape) for a in (gm, wk, wt, wgt)]
        args += [gm, wk, wt, wgt]
        out_specs = [out_specs, pl.BlockSpec((tm, wk.shape[1]), lambda i: (i, 0)),
                     pl.BlockSpec((1, wt.shape[0], tm), lambda i: (i // spt, 0, i % spt)),
                     pl.BlockSpec((1, wgt.shape[0], tm), lambda i: (i // spt, 0, i % spt))]
        out_shape = [out_shape, jax.ShapeDtypeStruct((n, wk.shape[1]), BF16),
                     jax.ShapeDtypeStruct((b, wt.shape[0], s), BF16),
                     jax.ShapeDtypeStruct((b, wgt.shape[0], s), F32)]
    return pl.pallas_call(
        functools.partial(_ffn_body, final_norm=final_norm, n_chunks=n_chunks, with_mixer=mixer is not None,
                          with_proj=proj is not None),
        grid=(n // tm,),
        in_specs=in_specs,
        out_specs=out_specs,
        out_shape=out_shape,
        compiler_params=pltpu.CompilerParams(
            dimension_semantics=("parallel",), vmem_limit_bytes=VMEM_LIMIT),
        name="ffn_final" if final_norm else "ffn",
    )(*args)


def _inproj_body(x_ref, g_ref, wk_ref, wt_ref, wgt_ref, pk_ref, pt_ref, gates_ref):
    h = _rmsnorm(x_ref[...], g_ref[...]).astype(BF16)
    pk_ref[...] = _dot(h, wk_ref[...]).astype(BF16)
    pt_ref[0] = _dot_nt(wt_ref[...], h).astype(BF16)
    gates_ref[0] = jax.nn.sigmoid(_dot_nt(wgt_ref[...], h))


def _inproj(x2d, g, wk, wt, wgt, b, s, *, tm=512):
    n = x2d.shape[0]
    nk = wk.shape[1]
    nt = wt.shape[0]
    ng = wgt.shape[0]
    spt = s // tm
    const = lambda shape: pl.BlockSpec(shape, lambda i: (0, 0))
    return pl.pallas_call(
        _inproj_body,
        grid=(n // tm,),
        in_specs=[
            pl.BlockSpec((tm, D_MODEL), lambda i: (i, 0)),
            const((1, D_MODEL)), const((D_MODEL, nk)), const((nt, D_MODEL)), const((ng, D_MODEL)),
        ],
        out_specs=[
            pl.BlockSpec((tm, nk), lambda i: (i, 0)),
            pl.BlockSpec((1, nt, tm), lambda i: (i // spt, 0, i % spt)),
            pl.BlockSpec((1, ng, tm), lambda i: (i // spt, 0, i % spt)),
        ],
        out_shape=[jax.ShapeDtypeStruct((n, nk), BF16), jax.ShapeDtypeStruct((b, nt, s), BF16),
                   jax.ShapeDtypeStruct((b, ng, s), F32)],
        compiler_params=pltpu.CompilerParams(
            dimension_semantics=("parallel",), vmem_limit_bytes=VMEM_LIMIT),
        name="inproj",
    )(x2d, g, wk, wt, wgt)


def _compress_body(kraw_ref, vraw_ref, pos_ref, w1_ref, w2k_ref, w2vt_ref, kc_ref, vct_ref, xs_ref, *, ncp):
    half = CMP_LEN // 2
    for kv, raw_ref in enumerate((kraw_ref, vraw_ref)):
        xs_ref[...] = raw_ref[0].astype(F32)
        y0 = jnp.zeros((ncp, CMP_HIDDEN), F32)
        y1 = jnp.zeros((ncp, CMP_HIDDEN), F32)
        for p in range(half):
            xp = xs_ref[pl.ds(p, ncp, stride=CMP_STRIDE), :]
            y0 = y0 + _dot((xp + pos_ref[kv, 0, p:p + 1, :]).astype(BF16), w1_ref[kv, 0, p])
            y1 = y1 + _dot((xp + pos_ref[kv, 0, half + p:half + p + 1, :]).astype(BF16), w1_ref[kv, 0, half + p])
        h1 = y0 + pltpu.roll(y1, ncp - 1, 0)
        act = (h1 * jax.nn.sigmoid(h1)).astype(BF16)
        if kv == 0:
            kc_ref[0, 0] = _dot(act, w2k_ref[...]).astype(BF16)
        else:
            out_t = _dot_nt(w2vt_ref[...], act)
            row = lax.broadcasted_iota(jnp.int32, out_t.shape, 0)
            vct_ref[0, 0] = (out_t + jnp.where(row == ONES_ROW, 1.0, 0.0)).astype(BF16)


def _compress(pk, pos, w1, w2k, w2vt):
    b, s, _ = pk.shape
    ncp = s // CMP_STRIDE
    full = lambda shape: pl.BlockSpec(shape, lambda bi, g: (0,) * len(shape))
    return pl.pallas_call(
        functools.partial(_compress_body, ncp=ncp),
        grid=(b, NSA_GROUPS),
        in_specs=[
            pl.BlockSpec((1, s, HEAD_PAD), lambda bi, g: (bi, 0, KSLOT_KC)),
            pl.BlockSpec((1, s, HEAD_PAD), lambda bi, g: (bi, 0, KSLOT_VC)),
            pl.BlockSpec((2, 1, CMP_LEN, HEAD_PAD), lambda bi, g: (0, g, 0, 0)),
            pl.BlockSpec((2, 1, CMP_LEN, HEAD_PAD, CMP_HIDDEN), lambda bi, g: (0, g, 0, 0, 0)),
            full((CMP_HIDDEN, HEAD_PAD)),
            full((HEAD_PAD, CMP_HIDDEN)),
        ],
        out_specs=[
            pl.BlockSpec((1, 1, ncp, HEAD_PAD), lambda bi, g: (bi, g, 0, 0)),
            pl.BlockSpec((1, 1, HEAD_PAD, ncp), lambda bi, g: (bi, g, 0, 0)),
        ],
        out_shape=[
            jax.ShapeDtypeStruct((b, NSA_GROUPS, ncp, HEAD_PAD), BF16),
            jax.ShapeDtypeStruct((b, NSA_GROUPS, HEAD_PAD, ncp), BF16),
        ],
        scratch_shapes=[pltpu.VMEM((s, HEAD_PAD), F32)],
        compiler_params=pltpu.CompilerParams(
            dimension_semantics=("parallel", "arbitrary"), vmem_limit_bytes=VMEM_LIMIT),
        name="compress",
    )(pk, pk, pos, w1, w2k, w2vt)


def _flash_update(s_t, shift, pv_fn, m_ref, acc_ref):
    m_prev = m_ref[...]
    m_new = jnp.maximum(m_prev, jnp.max(s_t, axis=0, keepdims=True) + shift)
    alpha = jnp.exp2(m_prev - m_new)
    p = jnp.exp2(s_t - (m_new - shift)).astype(BF16)
    acc_ref[...] = alpha * acc_ref[...] + pv_fn(p)
    m_ref[...] = m_new


def _state_lanes(i):
    return pl.ds(pl.multiple_of(i * WIDE, WIDE), WIDE)


def _tile_phase(count, tile_fn, scores_fn, step_fn, sa_ref, sb_ref):
    if count == 0:
        return
    tile = lambda n: tile_fn(jnp.minimum(jnp.asarray(n, jnp.int32), count - 1))
    sa_ref[...] = scores_fn(*tile(0))

    def pair(p, carry):
        n = 2 * p
        sb_ref[...] = scores_fn(*tile(n + 1))
        step_fn(sa_ref, *tile(n))
        sa_ref[...] = scores_fn(*tile(n + 2))
        step_fn(sb_ref, *tile(n + 1))
        return carry

    n_pairs = count // 2
    unroll = max(u for u in range(1, MAX_PAIR_UNROLL + 1) if n_pairs % u == 0) if n_pairs else 1
    lax.fori_loop(0, n_pairs, pair, 0, unroll=unroll)
    if count % 2:
        step_fn(sa_ref, *tile(count - 1))


def _flat_stream(nq, n_far, far_i_ref, far_j_ref, scores_fn, pv_fn, bias_ref, shift_far,
                 sa_ref, sb_ref, m_ref, acc_ref):
    zero = jnp.zeros((1, WIDE), F32)

    def own_step(s_ref, i, j):
        s_t = s_ref[...] + bias_ref[0, 2 * TQ:3 * TQ, :]
        m = jnp.max(s_t, axis=0, keepdims=True)
        m_ref[:, _state_lanes(i)] = m
        acc_ref[:, _state_lanes(i)] = pv_fn(j)(jnp.exp2(s_t - m).astype(BF16))

    def prev_step(s_ref, i, j):
        _flash_update(s_ref[...] + bias_ref[0, TQ:2 * TQ, :], zero, pv_fn(j),
                      m_ref.at[:, _state_lanes(i)], acc_ref.at[:, _state_lanes(i)])

    def far_step(s_ref, i, j):
        _flash_update(s_ref[...], shift_far, pv_fn(j), m_ref.at[:, _state_lanes(i)], acc_ref.at[:, _state_lanes(i)])

    _tile_phase(nq, lambda n: (n, n), scores_fn, own_step, sa_ref, sb_ref)
    _tile_phase(nq - 1, lambda n: (n + 1, n), scores_fn, prev_step, sa_ref, sb_ref)
    _tile_phase(n_far, lambda n: (far_i_ref[n], far_j_ref[n]), scores_fn, far_step, sa_ref, sb_ref)


def _lane_block(x, r):
    return x[:, r * TQ:(r + 1) * TQ]


def _shift_row(c31_ref, head0):
    return jnp.concatenate([jnp.full((1, TQ), c31_ref[head0 + r], F32) for r in range(HG)], axis=1)


def _rank_select(score, n_rows, top, vote=None):
    sub = 8
    n_groups = score.shape[0] // sub
    groups = [score[g * sub:(g + 1) * sub] for g in range(n_groups)]
    counts = [jnp.zeros(groups[0].shape, F32) for _ in range(n_groups)]
    local = lax.broadcasted_iota(jnp.int32, groups[0].shape, 0)
    for j in range(n_rows):
        row = score[j:j + 1, :]
        one = 1.0 if vote is None else vote(j)
        for g in range(n_groups):
            if j < g * sub:
                ahead = jnp.where(row >= groups[g], one, 0.0)
            elif j >= (g + 1) * sub:
                ahead = jnp.where(row > groups[g], one, 0.0)
            else:
                ahead = jnp.where(local > j - g * sub, jnp.where(row >= groups[g], one, 0.0),
                                  jnp.where(row > groups[g], one, 0.0))
            counts[g] = counts[g] + ahead
    return jnp.concatenate(counts, axis=0) < top


def _key_operand(slot, half, pat):
    if not (isinstance(half, int) and half == 0):
        slot = pltpu.roll(slot.astype(F32), HEAD_DIM * half, 1).astype(BF16)
    lane = lax.broadcasted_iota(jnp.int32, slot.shape, 1)
    return jnp.where(lane < HEAD_DIM, slot, pat)


def _value_operand(v_t):
    rows = lax.broadcasted_iota(jnp.int32, (ACC_ROWS - HEAD_DIM, v_t.shape[1]), 0)
    return jnp.concatenate([v_t, jnp.where(rows == 0, 1.0, 0.0).astype(BF16)], axis=0)


def _moba_body(c31_ref, far_i_ref, far_j_ref, q_ref, k_ref, vt_ref, pat_ref, bias_ref, o_ref,
               kmean_ref, kx_ref, vx_ref, qx_ref, sa_ref, sb_ref, m_ref, acc_ref, *, nb, n_far):
    hg = pl.program_id(1)
    s = q_ref.shape[2]
    nbp = -(-nb // 16) * 16

    kmean_ref[...] = jnp.zeros_like(kmean_ref)
    blk = lax.broadcasted_iota(jnp.int32, (nbp, s), 0)
    cur = lax.broadcasted_iota(jnp.int32, (nbp, s), 1) // MOBA_BLOCK
    zero_rows = jnp.zeros((HEAD_PAD - HEAD_DIM, s), BF16)
    for r in range(HG):
        kx = _key_operand(k_ref[0, :, (r // 2) * HEAD_PAD:(r // 2 + 1) * HEAD_PAD], r % 2, pat_ref[...])
        kx_ref[:, r * HEAD_PAD:(r + 1) * HEAD_PAD] = kx
        vx_ref[r] = _value_operand(vt_ref[0, r * HEAD_DIM:(r + 1) * HEAD_DIM, :])
        for n in range(nb):
            kb = kx[n * MOBA_BLOCK:(n + 1) * MOBA_BLOCK, :].astype(F32)
            kmean_ref[r, n:n + 1, :] = jnp.mean(kb, axis=0, keepdims=True)
        q_t = q_ref[0, r * HEAD_DIM:(r + 1) * HEAD_DIM, :]
        q_pad = jnp.concatenate([q_t, zero_rows], axis=0)
        kh, kl = _split_bf16(kmean_ref[r, :nbp, :])
        gate = _dot(kh, q_pad) + _dot(kl, q_pad)
        top = _rank_select(gate, nb, MOBA_TOPK, lambda n: jnp.where(n < cur[:8], 1.0, 0.0))
        sel = ((blk < cur) & top) | (blk == cur)
        mask_rows = jnp.where(sel, 0.0, NEG).astype(BF16)
        pad_rows = jnp.zeros((HEAD_PAD - EXT0 - nbp, s), BF16)
        qx_ref[r] = jnp.concatenate([q_t, mask_rows, pad_rows], axis=0)

    def scores(i, j):
        qoff = pl.multiple_of(i * TQ, TQ)
        koff = pl.multiple_of(j * TQ, TQ)
        return jnp.concatenate(
            [_dot(kx_ref[pl.ds(koff, TQ), r * HEAD_PAD:(r + 1) * HEAD_PAD], qx_ref[r, :, pl.ds(qoff, TQ)])
             for r in range(HG)], axis=1)

    def pv_fn(j):
        off = pl.multiple_of(j * TQ, TQ)

        def fn(p):
            return jnp.concatenate(
                [_dot(vx_ref[r, :, pl.ds(off, TQ)], _lane_block(p, r)) for r in range(HG)], axis=1)
        return fn

    _flat_stream(nb, n_far, far_i_ref, far_j_ref, scores, pv_fn, bias_ref, _shift_row(c31_ref, hg * HG),
                 sa_ref, sb_ref, m_ref, acc_ref)

    def finish(i, carry):
        acc = acc_ref[:, _state_lanes(i)]
        out = (acc[:HEAD_DIM] / acc[ONES_ROW:ONES_ROW + 1, :]).astype(BF16)
        for r in range(HG):
            o_ref[0, r * HEAD_DIM:(r + 1) * HEAD_DIM, pl.ds(pl.multiple_of(i * TQ, TQ), TQ)] = _lane_block(out, r)
        return carry

    lax.fori_loop(0, nb, finish, 0)


def _moba(pk, pt, pat, bias_win, c31, far_i, far_j):
    b, s, _ = pk.shape
    nb = s // MOBA_BLOCK
    hrows = HG * HEAD_DIM
    smem = pl.BlockSpec(memory_space=pltpu.SMEM)
    return pl.pallas_call(
        functools.partial(_moba_body, nb=nb, n_far=far_i.shape[0]),
        grid=(b, MOBA_HEADS // HG),
        in_specs=[
            smem, smem, smem,
            pl.BlockSpec((1, hrows, s), lambda bi, hg: (bi, TROW_QA // HG + hg, 0)),
            pl.BlockSpec((1, s, hrows), lambda bi, hg: (bi, 0, KSLOT_KA * HEAD_PAD // hrows + hg)),
            pl.BlockSpec((1, hrows, s), lambda bi, hg: (bi, TROW_VA // HG + hg, 0)),
            pl.BlockSpec((s, HEAD_PAD), lambda bi, hg: (0, 0), pipeline_mode=pl.Buffered(1)),
            pl.BlockSpec((1, 3 * TQ, WIDE), lambda bi, hg: (hg, 0, 0)),
        ],
        out_specs=pl.BlockSpec((1, HG * HEAD_DIM, s), lambda bi, hg: (bi, hg, 0)),
        out_shape=jax.ShapeDtypeStruct((b, MOBA_HEADS * HEAD_DIM, s), BF16),
        scratch_shapes=[
            pltpu.VMEM((HG, HEAD_PAD, HEAD_PAD), F32),
            pltpu.VMEM((s, HG * HEAD_PAD), BF16),
            pltpu.VMEM((HG, ACC_ROWS, s), BF16),
            pltpu.VMEM((HG, HEAD_PAD, s), BF16),
            pltpu.VMEM((TQ, WIDE), F32),
            pltpu.VMEM((TQ, WIDE), F32),
            pltpu.VMEM((1, nb * WIDE), F32),
            pltpu.VMEM((ACC_ROWS, nb * WIDE), F32),
        ],
        compiler_params=pltpu.CompilerParams(
            dimension_semantics=("parallel", "arbitrary"), vmem_limit_bytes=VMEM_LIMIT),
        name="moba",
    )(c31, far_i, far_j, pt, pk, pt, pat, bias_win)


def _nsa_body(c31_ref, far_i_ref, far_j_ref, q_ref, ks_ref, kw_ref, vst_ref, vwt_ref, kc_ref, vct_ref,
              gates_ref, pat_ref, bias_ref, bcmp_ref, ovt_ref, o_ref,
              ksx_ref, kwx_ref, vsx_ref, vwx_ref, qx_ref, sa_ref, sb_ref, m_ref, accs_ref, part_ref,
              *, ncp, nslc, nq, n_far):
    g = pl.program_id(1)

    ksx_ref[...] = _key_operand(ks_ref[0], g, pat_ref[...])
    kwx_ref[...] = _key_operand(kw_ref[0], g, jnp.zeros((), BF16))
    grows = pl.ds(pl.multiple_of(g * HEAD_DIM, HEAD_DIM), HEAD_DIM)
    vsx_ref[...] = _value_operand(vst_ref[0, grows, :])
    vwx_ref[...] = _value_operand(vwt_ref[0, grows, :])
    zero_rows = jnp.zeros((HEAD_PAD - HEAD_DIM, WIDE), BF16)

    def front(i, n_win, n_act):
        qoff = pl.multiple_of(i * TQ, TQ)
        q4c = jnp.concatenate([q_ref[0, r * HEAD_DIM:(r + 1) * HEAD_DIM, pl.ds(qoff, TQ)] for r in range(HG)],
                              axis=1)
        q4 = jnp.concatenate([q4c, zero_rows], axis=0)

        band_off = pl.multiple_of((nq - 1 - i) * (TQ // CMP_STRIDE), TQ // CMP_STRIDE)
        s_t = _dot(kc_ref[0, 0], q4) + bcmp_ref[0, pl.ds(band_off, ncp), :]
        m = jnp.max(s_t, axis=0, keepdims=True)
        p = jnp.exp2(s_t - m)
        den = jnp.sum(p, axis=0, keepdims=True)
        pn = p * jnp.where(m > 0.5 * NEG, 1.0 / den, 0.0)
        o_cmp = _dot(vct_ref[0, 0, :ACC_ROWS, :], pn.astype(BF16))
        psum = _lane_block(pn, 0)
        for r in range(1, HG):
            psum = psum + _lane_block(pn, r)

        ph, plo = _split_bf16(psum)
        imp = (_dot(ovt_ref[...], ph) + _dot(ovt_ref[...], plo))[:n_act]
        blk = lax.broadcasted_iota(jnp.int32, imp.shape, 0)
        cur = (i * TQ + lax.broadcasted_iota(jnp.int32, imp.shape, 1)) // SLC_BLOCK
        forced = (blk == 0) | (blk == cur) | (blk == cur - 1)
        score = jnp.where(forced, 1e30, jnp.where(blk <= cur, imp, -1.0))
        sel = _rank_select(score, n_act, SLC_TOPN)
        mask_rows = jnp.where(sel, 0.0, NEG)
        if n_act < EXT0:
            mask_rows = jnp.concatenate([mask_rows, jnp.full((EXT0 - n_act, TQ), NEG, F32)], axis=0)
        mask_rows = mask_rows.astype(BF16)
        qx_ref[:, _state_lanes(i)] = jnp.concatenate(
            [q4c, jnp.concatenate([mask_rows] * HG, axis=1)], axis=0)

        start = pl.multiple_of((i + 1 - n_win) * TQ, TQ)
        s_w = _dot(kwx_ref[pl.ds(start, n_win * TQ), :], q4) + bias_ref[0, (3 - n_win) * TQ:, :]
        p_w = jnp.exp2(s_w - jnp.max(s_w, axis=0, keepdims=True)).astype(BF16)
        acc_w = _dot(vwx_ref[:, pl.ds(start, n_win * TQ)], p_w)
        o_win = acc_w / acc_w[ONES_ROW:ONES_ROW + 1, :]

        gt = gates_ref[0, :, pl.ds(qoff, TQ)]
        part_ref[:, _state_lanes(i)] = jnp.concatenate(
            [gt[N_GATES * r:N_GATES * r + 1, :] * _lane_block(o_cmp, r)
             + gt[N_GATES * r + 2:N_GATES * r + 3, :] * _lane_block(o_win, r) for r in range(HG)], axis=1)

    blocks_upto = lambda i_end: min(-(-nslc // 8) * 8, -(-i_end * TQ // SLC_BLOCK // 8) * 8)
    for i0 in range(min(2, nq)):
        front(jnp.asarray(i0, jnp.int32), i0 + 1, blocks_upto(i0 + 1))

    mid = max(2, nq // 2)
    for lo, hi in ((2, mid), (mid, nq)):
        def front_loop(i, carry, n_act=blocks_upto(hi)):
            front(i, 3, n_act)
            return carry

        if hi > lo:
            lax.fori_loop(lo, hi, front_loop, 0, unroll=2)

    def scores(i, j):
        return _dot(ksx_ref[pl.ds(pl.multiple_of(j * TQ, TQ), TQ), :], qx_ref[:, _state_lanes(i)])

    def pv_fn(j):
        off = pl.multiple_of(j * TQ, TQ)
        return lambda p: _dot(vsx_ref[:, pl.ds(off, TQ)], p)

    _flat_stream(nq, n_far, far_i_ref, far_j_ref, scores, pv_fn, bias_ref,
                 _shift_row(c31_ref, MOBA_HEADS + g * HG), sa_ref, sb_ref, m_ref, accs_ref)

    def finish(i, carry):
        qoff = pl.multiple_of(i * TQ, TQ)
        acc = accs_ref[:, _state_lanes(i)]
        o_slc = acc / acc[ONES_ROW:ONES_ROW + 1, :]
        part = part_ref[:, _state_lanes(i)]
        gt = gates_ref[0, :, pl.ds(qoff, TQ)]
        for r in range(HG):
            out = _lane_block(part, r) + gt[N_GATES * r + 1:N_GATES * r + 2, :] * _lane_block(o_slc, r)
            o_ref[0, r * HEAD_DIM:(r + 1) * HEAD_DIM, pl.ds(qoff, TQ)] = out[:HEAD_DIM].astype(BF16)
        return carry

    lax.fori_loop(0, nq, finish, 0)


def _nsa(pk, pt, kc, vct, gates_t, pat, bias_win, bias_cmp, ovt, c31, far_i, far_j):
    b, s, _ = pk.shape
    nq = s // TQ
    ncp = s // CMP_STRIDE
    nslc = s // SLC_BLOCK
    hrows = HG * HEAD_DIM
    band_rows = bias_cmp.shape[1]
    hgrp = MOBA_HEADS // HG
    smem = pl.BlockSpec(memory_space=pltpu.SMEM)
    k_spec = lambda slot: pl.BlockSpec((1, s, HEAD_PAD), lambda bi, g: (bi, 0, slot))
    vt_spec = lambda row: pl.BlockSpec((1, HEAD_PAD, s), lambda bi, g: (bi, row * HEAD_DIM // HEAD_PAD, 0))
    bias_spec = lambda rows: pl.BlockSpec((1, rows, WIDE), lambda bi, g: (hgrp + g, 0, 0),
                                          pipeline_mode=pl.Buffered(1))
    return pl.pallas_call(
        functools.partial(_nsa_body, ncp=ncp, nslc=nslc, nq=nq, n_far=far_i.shape[0]),
        grid=(b, NSA_GROUPS),
        in_specs=[
            smem, smem, smem,
            pl.BlockSpec((1, hrows, s), lambda bi, g: (bi, TROW_QB // HG + g, 0)),
            k_spec(KSLOT_KS), k_spec(KSLOT_KW), vt_spec(TROW_VS), vt_spec(TROW_VW),
            pl.BlockSpec((1, 1, ncp, HEAD_PAD), lambda bi, g: (bi, g, 0, 0)),
            pl.BlockSpec((1, 1, HEAD_PAD, ncp), lambda bi, g: (bi, g, 0, 0)),
            pl.BlockSpec((1, GATE_ROWS, s), lambda bi, g: (bi, g, 0)),
            pl.BlockSpec((s, HEAD_PAD), lambda bi, g: (0, 0), pipeline_mode=pl.Buffered(1)),
            bias_spec(3 * TQ), bias_spec(band_rows),
            pl.BlockSpec((HEAD_PAD, ncp), lambda bi, g: (0, 0)),
        ],
        out_specs=pl.BlockSpec((1, HG * HEAD_DIM, s), lambda bi, g: (bi, g, 0)),
        out_shape=jax.ShapeDtypeStruct((b, NSA_HEADS * HEAD_DIM, s), BF16),
        scratch_shapes=[
            pltpu.VMEM((s, HEAD_PAD), BF16),
            pltpu.VMEM((s, HEAD_PAD), BF16),
            pltpu.VMEM((ACC_ROWS, s), BF16),
            pltpu.VMEM((ACC_ROWS, s), BF16),
            pltpu.VMEM((HEAD_PAD, nq * WIDE), BF16),
            pltpu.VMEM((TQ, WIDE), F32),
            pltpu.VMEM((TQ, WIDE), F32),
            pltpu.VMEM((1, nq * WIDE), F32),
            pltpu.VMEM((ACC_ROWS, nq * WIDE), F32),
            pltpu.VMEM((ACC_ROWS, nq * WIDE), F32),
        ],
        compiler_params=pltpu.CompilerParams(
            dimension_semantics=("parallel", "arbitrary"), vmem_limit_bytes=VMEM_LIMIT),
        name="nsa",
    )(c31, far_i, far_j, pt, pk, pk, pt, pt, kc, vct, gates_t, pat, bias_win, bias_cmp, ovt)


def _rel_bucket(dist):
    n = jnp.maximum(dist, 0)
    nf = jnp.maximum(n, 1).astype(F32)
    large = REL_MAX_EXACT + (jnp.log(nf / REL_MAX_EXACT) / math.log(REL_MAX_DIST / REL_MAX_EXACT)
                             * (REL_BUCKETS - REL_MAX_EXACT)).astype(jnp.int32)
    return jnp.where(n < REL_MAX_EXACT, n, jnp.minimum(large, REL_BUCKETS - 1))


def _bias_body(tab_ref, *refs, runs):
    h = pl.program_id(0)
    n_tab = len(refs) // 2
    for bk_ref, out_ref, table_runs in zip(refs[:n_tab], refs[n_tab:], runs):
        for start, stop, far in table_runs:
            bk = bk_ref[start:stop, :]
            if far:
                acc = jnp.where(bk >= 0, tab_ref[h, REL_BUCKETS - 1], NEG)
            else:
                acc = jnp.full(bk.shape, NEG, F32)
                for bucket in range(REL_BUCKETS):
                    acc = jnp.where(bk == bucket, tab_ref[h, bucket], acc)
            out_ref[0, start:stop, :] = acc


def _far_runs(dist, masked, rows_per_run=8):
    far_rows = np.all(masked | (dist >= REL_MAX_DIST), axis=1)
    groups = far_rows.reshape(-1, rows_per_run).all(axis=1)
    runs, start = [], 0
    for g in range(1, len(groups) + 1):
        if g == len(groups) or groups[g] != groups[start]:
            runs.append((start * rows_per_run, g * rows_per_run, bool(groups[start])))
            start = g
    return tuple(runs)


def _bias_tables(rel_bias, ncp, nq):
    tab = rel_bias.T.astype(F32) * LOG2E
    n_heads = tab.shape[0]
    c = jnp.arange(TQ)[:, None]
    r = jnp.arange(TQ)[None, :]
    d_own = r - c
    bk_own = jnp.where(d_own >= 0, _rel_bucket(d_own), -1)
    bk_prev = _rel_bucket(d_own + TQ)
    bk_win2 = jnp.where(d_own + 2 * TQ < WINDOW, _rel_bucket(d_own + 2 * TQ), -1)
    band_rows = ncp + (nq - 1) * (TQ // CMP_STRIDE)
    u = jnp.arange(band_rows)[:, None] - (nq - 1) * (TQ // CMP_STRIDE)
    d_cmp = r - CMP_STRIDE * u - (CMP_LEN - 1)
    bk_cmp = jnp.where(d_cmp >= 0, _rel_bucket(d_cmp), -1)
    tables = [t.astype(jnp.int32) for t in (jnp.concatenate([bk_win2, bk_prev, bk_own], axis=0), bk_cmp)]
    cn, rn = np.arange(TQ)[:, None], np.arange(TQ)[None, :]
    d_win = np.concatenate([rn - cn + 2 * TQ, rn - cn + TQ, rn - cn], axis=0)
    m_win = np.concatenate([rn - cn + 2 * TQ >= WINDOW, np.zeros((TQ, TQ), bool), rn - cn < 0], axis=0)
    d_cmp_n = rn - CMP_STRIDE * (np.arange(band_rows)[:, None] - (nq - 1) * (TQ // CMP_STRIDE)) - (CMP_LEN - 1)
    runs = (_far_runs(d_win, m_win), _far_runs(d_cmp_n, d_cmp_n < 0))
    outs = pl.pallas_call(
        functools.partial(_bias_body, runs=runs),
        grid=(n_heads,),
        in_specs=[pl.BlockSpec(memory_space=pltpu.SMEM)]
        + [pl.BlockSpec(t.shape, lambda h: (0, 0)) for t in tables],
        out_specs=[pl.BlockSpec((1, t.shape[0], TQ), lambda h: (h // HG, 0, h % HG)) for t in tables],
        out_shape=[jax.ShapeDtypeStruct((n_heads // HG, t.shape[0], WIDE), F32) for t in tables],
        compiler_params=pltpu.CompilerParams(dimension_semantics=("parallel",), vmem_limit_bytes=VMEM_LIMIT),
        name="bias_tiles",
    )(tab, *tables)
    return outs, tab[:, REL_BUCKETS - 1]


def _slc_overlap_t(ncp, nslc):
    ratio = SLC_BLOCK // CMP_STRIDE
    mc = CMP_LEN // CMP_STRIDE
    w = np.zeros((HEAD_PAD, nslc * ratio + mc), np.float32)
    for j in range(nslc):
        for a in range(ratio):
            for m in range(mc):
                w[j, ratio * j + a + m] += 1.0
    n_cmp = ncp - 1
    out = np.zeros((HEAD_PAD, ncp), np.float32)
    out[:, :n_cmp] = w[:, :n_cmp]
    return out


def _block_onehot(s, block):
    pat = np.zeros((s, HEAD_PAD), np.float32)
    pat[np.arange(s), EXT0 + np.arange(s) // block] = 1.0
    return pat


def kernel(x, norm_ffn1, w_ffn1_gate, w_ffn1_up, w_ffn1_down, norm_mix, w_in, cmp_pos_k, cmp_w1_k, cmp_w2_k,
           cmp_pos_v, cmp_w1_v, cmp_w2_v, w_out, norm_ffn2, w_ffn2_gate, w_ffn2_up, w_ffn2_down, rel_bias,
           norm_final):
    b, s, d = x.shape
    assert d == D_MODEL and s % (2 * TQ) == 0 and norm_ffn1.shape[0] == 1
    assert s // SLC_BLOCK <= HEAD_PAD - EXT0, "selection-mask rows must fit beside the 64 head dims"
    n = b * s
    nq = s // TQ
    ncp = s // CMP_STRIDE
    nslc = s // SLC_BLOCK
    scale = HEAD_DIM ** -0.5 * LOG2E

    sizes = (512, 512, 512, 512) + (128,) * 6 + (24,)
    splits = np.cumsum(sizes)[:-1].tolist()
    wqa, wka, wva, wqb, wkc, wvc, wks, wvs, wkw, wvw, wgb = jnp.split(w_in[0], splits, axis=1)
    w_k = jnp.concatenate([wka, wkc, wvc, wks, wkw], axis=1).astype(BF16)
    w_t = jnp.concatenate([wqa * scale, wva, wqb * scale, wvs, wvw], axis=1).T.astype(BF16)
    assert w_k.shape[1] == N_KSLOTS * HEAD_PAD and w_t.shape[0] == N_TROWS * HEAD_DIM
    wgt = wgb.reshape(D_MODEL, NSA_GROUPS, NSA_REP * N_GATES)
    wgt = jnp.pad(wgt, ((0, 0), (0, 0), (0, GATE_ROWS - NSA_REP * N_GATES)))
    wgt = wgt.reshape(D_MODEL, NSA_GROUPS * GATE_ROWS).T.astype(BF16)

    def cmp_params(pos, w1, w2):
        w1_h = w1[0].reshape(CMP_LEN, HEAD_DIM, CMP_HIDDEN)
        pos_g, w1_g = [], []
        for grp in range(NSA_GROUPS):
            pad = ((0, 0), (grp * HEAD_DIM, HEAD_PAD - (grp + 1) * HEAD_DIM))
            pos_g.append(jnp.pad(pos[0], pad))
            w1_g.append(jnp.pad(w1_h, pad + ((0, 0),)))
        w2_p = jnp.pad(w2[0], ((0, 0), (0, HEAD_PAD - HEAD_DIM)))
        return jnp.stack(pos_g), jnp.stack(w1_g).astype(BF16), w2_p.astype(BF16)

    pos_k, w1k, w2k = cmp_params(cmp_pos_k, cmp_w1_k, cmp_w2_k)
    pos_v, w1v, w2v = cmp_params(cmp_pos_v, cmp_w1_v, cmp_w2_v)
    cmp_pos = jnp.stack([pos_k, pos_v])
    cmp_w1 = jnp.stack([w1k, w1v])

    wo = w_out[0]
    wo_a = wo[:MOBA_HEADS * HEAD_DIM].astype(BF16)
    wo_b = wo[MOBA_HEADS * HEAD_DIM:].astype(BF16)

    (bias_win, bias_cmp), c31 = _bias_tables(rel_bias, ncp, nq)
    ovt = jnp.asarray(_slc_overlap_t(ncp, nslc)).astype(BF16)
    pat_moba = jnp.asarray(_block_onehot(s, MOBA_BLOCK)).astype(BF16)
    pat_slc = jnp.asarray(_block_onehot(s, SLC_BLOCK)).astype(BF16)
    far = [(i, j) for j in range(nq) for i in range(j + 2, nq)]
    far_i = jnp.asarray(np.array([i for i, _ in far], np.int32))
    far_j = jnp.asarray(np.array([j for _, j in far], np.int32))

    x2d = x.reshape(n, D_MODEL)
    nf = norm_final.reshape(1, D_MODEL)
    x1, pk, pt, gates_t = _ffn(x2d, norm_ffn1, w_ffn1_gate[0].astype(BF16), w_ffn1_up[0].astype(BF16),
                               w_ffn1_down[0].astype(BF16), nf, final_norm=False,
                               proj=(norm_mix, w_k, w_t, wgt, b, s))
    pk = pk.reshape(b, s, N_KSLOTS * HEAD_PAD)
    kc, vct = _compress(pk, cmp_pos, cmp_w1, w2k, w2v.T)
    o_a = _moba(pk, pt, pat_moba, bias_win, c31, far_i, far_j)
    o_b = _nsa(pk, pt, kc, vct, gates_t, pat_slc, bias_win, bias_cmp, ovt, c31, far_i, far_j)
    y = _ffn(x1, norm_ffn2, w_ffn2_gate[0].astype(BF16), w_ffn2_up[0].astype(BF16),
             w_ffn2_down[0].astype(BF16), nf, final_norm=True, mixer=(o_a, o_b, wo_a, wo_b))
    return y.reshape(b, s, D_MODEL)
```
